```python
import math
import jax, jax.numpy as jnp
from jax import lax
import numpy as np

D_MODEL = 1024
BATCH = 32
SEQ = 2048
DEPTH = 1

GDN_HEADS = 4
GDN_DK = 128
GDN_DV = 128
GDN_CONV = 4
GDN_CHUNK = 64
SWA_Q_HEADS = 8
SWA_KV_HEADS = 2
SWA_HEAD_DIM = 64
WINDOW = 128
REL_BUCKETS = 32
REL_MAX_DIST = 128
D_FF = 2816
FFN_CONV = 3
NORM_EPS = 1e-6
NEG_INF = -1e30

GDN_QK = GDN_HEADS * GDN_DK
GDN_QKV = GDN_HEADS * (2 * GDN_DK + GDN_DV)
GDN_OUT = GDN_HEADS * GDN_DV
SWA_Q = SWA_Q_HEADS * SWA_HEAD_DIM
SWA_KV = SWA_KV_HEADS * SWA_HEAD_DIM
D_MIX = GDN_OUT + SWA_Q
D_IN = GDN_QKV + GDN_OUT + 2 * GDN_HEADS + SWA_Q + 2 * SWA_KV

kernel_name = "hybrid_gdn_swa_sink_t5bias_convffn_sandwich"


def rms_norm(x, w):
    xf = x.astype(jnp.float32)
    y = xf * lax.rsqrt(jnp.mean(xf * xf, axis=-1, keepdims=True) + NORM_EPS)
    return (y * w.astype(jnp.float32)).astype(x.dtype)


def l2_norm(x):
    return x * lax.rsqrt(jnp.sum(x * x, axis=-1, keepdims=True) + NORM_EPS)


def causal_dwconv(x, w):
    K, C = w.shape
    return lax.conv_general_dilated(
        x, w.reshape(K, 1, C).astype(x.dtype), window_strides=(1,),
        padding=[(K - 1, 0)], dimension_numbers=("NWC", "WIO", "NWC"),
        feature_group_count=C)


def gated_delta_rule_chunked(q, k, v, g, beta):
    B, T, H, dk = q.shape
    dv = v.shape[-1]
    C = GDN_CHUNK
    N = T // C

    def chunks(t):
        return jnp.moveaxis(t.reshape(B, N, C, H, *t.shape[3:]), 3, 2)

    qc = chunks(q * (dk ** -0.5))
    kc, vc, gc, bc = chunks(k), chunks(v), chunks(g), chunks(beta)
    G = jnp.cumsum(gc, axis=-1)
    causal = jnp.tril(jnp.ones((C, C), dtype=bool))
    strict = jnp.tril(jnp.ones((C, C), dtype=bool), k=-1)
    diff = G[..., :, None] - G[..., None, :]
    decay = jnp.where(causal, jnp.exp(jnp.where(causal, diff, 0.0)), 0.0)

    kb = kc * bc[..., None]
    A = jnp.where(strict, jnp.einsum("bnhid,bnhjd->bnhij", kb, kc) * decay, 0.0)
    eye = jnp.broadcast_to(jnp.eye(C, dtype=q.dtype), A.shape)
    Tinv = lax.linalg.triangular_solve(eye + A, eye, left_side=True, lower=True,
                                       unit_diagonal=True)
    u = jnp.einsum("bnhij,bnhjd->bnhid", Tinv, vc * bc[..., None])
    w = jnp.einsum("bnhij,bnhjd->bnhid", Tinv, kb * jnp.exp(G)[..., None])
    qk = jnp.einsum("bnhid,bnhjd->bnhij", qc, kc) * decay
    q_dec = qc * jnp.exp(G)[..., None]
    k_dec = kc * jnp.exp(G[..., -1:] - G)[..., None]
    g_last = jnp.exp(G[..., -1])

    xs = tuple(jnp.moveaxis(t, 1, 0) for t in (u, w, qk, q_dec, k_dec, g_last))

    def step(S, inp):
        u_n, w_n, qk_n, qd_n, kd_n, gl_n = inp
        v_new = u_n - jnp.einsum("bhck,bhkv->bhcv", w_n, S)
        o = jnp.einsum("bhck,bhkv->bhcv", qd_n, S) + jnp.einsum("bhij,bhjv->bhiv", qk_n, v_new)
        S = S * gl_n[..., None, None] + jnp.einsum("bhck,bhcv->bhkv", kd_n, v_new)
        return S, o

    S0 = jnp.zeros((B, H, dk, dv), dtype=q.dtype)
    _, o = lax.scan(step, S0, xs)
    return jnp.transpose(o, (1, 0, 3, 2, 4)).reshape(B, T, H, dv)


def t5_causal_bucket(dist):
    max_exact = REL_BUCKETS // 2
    is_small = dist < max_exact
    large = max_exact + (jnp.log(jnp.maximum(dist, 1).astype(jnp.float32) / max_exact)
                         / math.log(REL_MAX_DIST / max_exact)
                         * (REL_BUCKETS - max_exact)).astype(jnp.int32)
    large = jnp.minimum(large, REL_BUCKETS - 1)
    return jnp.where(is_small, dist, large)


def band_geometry():
    qi = jnp.arange(WINDOW, dtype=jnp.int32)[:, None]
    sj = jnp.arange(2 * WINDOW, dtype=jnp.int32)[None, :]
    dist = qi + WINDOW - sj
    in_band = (dist >= 0) & (dist < WINDOW)
    return dist, in_band


def sliding_window_attention(q, k, v, sinks, bias):
    B, T, Hq, hd = q.shape
    Hkv = k.shape[2]
    G = Hq // Hkv
    W = WINDOW
    NB = T // W
    qb = q.reshape(B, NB, W, Hkv, G, hd)

    def band(t):
        tp = jnp.pad(t, ((0, 0), (W, 0), (0, 0), (0, 0))).reshape(B, NB + 1, W, Hkv, hd)
        return jnp.concatenate([tp[:, :-1], tp[:, 1:]], axis=2)

    kb, vb = band(k), band(v)
    s = jnp.einsum("bnqkgd,bnskd->bnkgqs", qb, kb).astype(jnp.float32) * (hd ** -0.5)
    s = s + bias.reshape(Hkv, G, W, 2 * W)
    _, in_band = band_geometry()
    key_pos = (jnp.arange(NB, dtype=jnp.int32)[:, None, None] * W
               + jnp.arange(2 * W, dtype=jnp.int32)[None, None, :] - W)
    mask = in_band[None] & (key_pos >= 0)
    s = jnp.where(mask[None, :, None, None], s, NEG_INF)
    sink = sinks.astype(jnp.float32).reshape(Hkv, G)[..., None, None]
    m = jnp.maximum(jnp.max(s, axis=-1, keepdims=True), sink)
    p = jnp.exp(s - m)
    p = p / (jnp.sum(p, axis=-1, keepdims=True) + jnp.exp(sink - m))
    o = jnp.einsum("bnkgqs,bnskd->bnqkgd", p.astype(vb.dtype), vb)
    return o.reshape(B, T, Hq, hd)


def setup_inputs(seed: int = 0) -> dict:
    key = jax.random.key(seed)
    ks = jax.random.split(key, 20)
    f32 = jnp.float32

    def nrm(k, shape, scale):
        return jax.random.normal(k, shape, f32) * scale

    def gain(k, shape):
        return 1.0 + 0.02 * jax.random.normal(k, shape, f32)

    dt = jnp.exp(jax.random.uniform(ks[5], (DEPTH, GDN_HEADS), f32)
                 * (math.log(0.1) - math.log(0.001)) + math.log(0.001))
    return {
        "x": jax.random.normal(ks[0], (BATCH, SEQ, D_MODEL), f32),
        "pre_mix_norm_w": gain(ks[1], (DEPTH, D_MODEL)),
        "w_in": nrm(ks[2], (DEPTH, D_MODEL, D_IN), D_MODEL ** -0.5),
        "gdn_conv_w": nrm(ks[3], (DEPTH, GDN_CONV, GDN_QKV), GDN_CONV ** -0.5),
        "gdn_a_log": jnp.log(jax.random.uniform(ks[4], (DEPTH, GDN_HEADS), f32, 1.0, 16.0)),
        "gdn_dt_bias": dt + jnp.log(-jnp.expm1(-dt)),
        "gdn_norm_w": gain(ks[6], (DEPTH, GDN_DV)),
        "swa_sinks": nrm(ks[7], (DEPTH, SWA_Q_HEADS), 1.0),
        "rel_bias_table": nrm(ks[8], (REL_BUCKETS, SWA_Q_HEADS), 0.5),
        "swa_norm_w": gain(ks[9], (DEPTH, SWA_Q)),
        "w_out": nrm(ks[10], (DEPTH, D_MIX, D_MODEL), D_MIX ** -0.5),
        "post_mix_norm_w": gain(ks[11], (DEPTH, D_MODEL)),
        "pre_ffn_norm_w": gain(ks[12], (DEPTH, D_MODEL)),
        "w_gate": nrm(ks[13], (DEPTH, D_MODEL, D_FF), D_MODEL ** -0.5),
        "w_up": nrm(ks[14], (DEPTH, D_MODEL, D_FF), D_MODEL ** -0.5),
        "ffn_conv_w": nrm(ks[15], (DEPTH, FFN_CONV, D_FF), FFN_CONV ** -0.5),
        "ffn_conv_b": nrm(ks[16], (DEPTH, D_FF), 0.02),
        "w_down": nrm(ks[17], (DEPTH, D_FF, D_MODEL), D_FF ** -0.5),
        "post_ffn_norm_w": gain(ks[18], (DEPTH, D_MODEL)),
    }


def reference(x, pre_mix_norm_w, w_in, gdn_conv_w, gdn_a_log, gdn_dt_bias, gdn_norm_w,
              swa_sinks, rel_bias_table, swa_norm_w, w_out, post_mix_norm_w,
              pre_ffn_norm_w, w_gate, w_up, ffn_conv_w, ffn_conv_b, w_down,
              post_ffn_norm_w):
    f32 = jnp.float32
    B, T, _ = x.shape
    dist, _ = band_geometry()
    bucket = t5_causal_bucket(jnp.maximum(dist, 0))
    rel_bias = jnp.transpose(rel_bias_table.astype(f32)[bucket], (2, 0, 1))

    sizes = [GDN_QKV, GDN_OUT, GDN_HEADS, GDN_HEADS, SWA_Q, SWA_KV, SWA_KV]
    split_at = [int(s) for s in np.cumsum(sizes)[:-1]]

    for l in range(DEPTH):
        h = rms_norm(x, pre_mix_norm_w[l])
        proj = h @ w_in[l]
        qkv_g, z, a, b, q_s, k_s, v_s = jnp.split(proj, split_at, axis=-1)

        qkv_g = jax.nn.silu(causal_dwconv(qkv_g, gdn_conv_w[l]).astype(f32))
        qg, kg, vg = jnp.split(qkv_g, [GDN_QK, 2 * GDN_QK], axis=-1)
        qg = l2_norm(qg.reshape(B, T, GDN_HEADS, GDN_DK))
        kg = l2_norm(kg.reshape(B, T, GDN_HEADS, GDN_DK))
        vg = vg.reshape(B, T, GDN_HEADS, GDN_DV)
        beta = jax.nn.sigmoid(b.astype(f32))
        g = -jnp.exp(gdn_a_log[l].astype(f32)) * jax.nn.softplus(
            a.astype(f32) + gdn_dt_bias[l].astype(f32))
        o_g = gated_delta_rule_chunked(qg, kg, vg, g, beta)
        o_g = rms_norm(o_g, gdn_norm_w[l]) * jax.nn.silu(
            z.astype(f32).reshape(B, T, GDN_HEADS, GDN_DV))
        o_g = o_g.reshape(B, T, GDN_OUT).astype(x.dtype)

        o_s = sliding_window_attention(
            q_s.reshape(B, T, SWA_Q_HEADS, SWA_HEAD_DIM),
            k_s.reshape(B, T, SWA_KV_HEADS, SWA_HEAD_DIM),
            v_s.reshape(B, T, SWA_KV_HEADS, SWA_HEAD_DIM),
            swa_sinks[l], rel_bias)
        o_s = rms_norm(o_s.reshape(B, T, SWA_Q), swa_norm_w[l])

        mix = jnp.concatenate([o_g, o_s], axis=-1) @ w_out[l]
        x = x + rms_norm(mix, post_mix_norm_w[l])

        h = rms_norm(x, pre_ffn_norm_w[l])
        gate = causal_dwconv(h @ w_gate[l], ffn_conv_w[l]) + ffn_conv_b[l]
        y = (jax.nn.gelu(gate, approximate=True) * (h @ w_up[l])) @ w_down[l]
        x = x + rms_norm(y, post_ffn_norm_w[l])
    return x
```

```python
import functools
import math

import numpy as np
import jax
import jax.numpy as jnp
from jax import lax
from jax.experimental import pallas as pl
from jax.experimental.pallas import tpu as pltpu

F32 = jnp.float32
BF16 = jnp.bfloat16

GDN_HEADS = 4
GDN_DK = 128
GDN_CONV = 4
GDN_CHUNK = 64
SWA_Q_HEADS = 8
SWA_KV_HEADS = 2
SWA_HEAD_DIM = 64
WINDOW = 128
REL_BUCKETS = 32
REL_MAX_DIST = 128
FFN_CONV = 3
NORM_EPS = 1e-6
NEG_INF = -1e30

GDN_QK = GDN_HEADS * GDN_DK
SWA_Q = SWA_Q_HEADS * SWA_HEAD_DIM
SWA_KV = SWA_KV_HEADS * SWA_HEAD_DIM

LANES = 128
SUBLANES = 8
VMEM_LIMIT_BYTES = 56 * 1024 * 1024

GDN_SUPER = 4 * GDN_CHUNK
FF_CHUNK = 256


def _rms(x, w):
    return x * lax.rsqrt(jnp.mean(x * x, axis=-1, keepdims=True) + NORM_EPS) * w


def _sigmoid(x):
    return 1.0 / (1.0 + jnp.exp(-x))


def _silu(x):
    return x * _sigmoid(x)


def _dot(a, b):
    return jnp.dot(a, b, preferred_element_type=F32)


def _dot_nt(a, b):
    return lax.dot_general(a, b, (((1,), (1,)), ((), ())), preferred_element_type=F32)


def _inproj_kernel(x_ref, nw_ref, wmain_ref, wab_ref, convw_ref, alog_ref, dtb_ref,
                   q_ref, k_ref, v_ref, z_ref, gb_ref, qs_ref, kvs_ref, pad_ref):
    tm = x_ref.shape[0]
    halo = SUBLANES

    @pl.when(pl.program_id(1) == 0)
    def _():
        pad_ref[:, 0:halo, :] = jnp.zeros((3, halo, GDN_QK), F32)

    h = _rms(x_ref[...], nw_ref[...]).astype(BF16)

    ab = _dot(h, wab_ref[...])
    lane = lax.broadcasted_iota(jnp.int32, ab.shape, 1)
    pre = ab + dtb_ref[...]
    softplus = jnp.maximum(pre, 0.0) + jnp.log(1.0 + jnp.exp(-jnp.abs(pre)))
    g = -jnp.exp(alog_ref[...]) * softplus
    gb_ref[...] = jnp.where(lane < GDN_HEADS, g, _sigmoid(ab))

    outs = (q_ref, k_ref, v_ref)
    for grp in range(3):
        c0 = grp * GDN_QK
        raw = _dot(h, wmain_ref[:, c0:c0 + GDN_QK])
        pad_ref[grp, halo:halo + tm, :] = raw
        cw = convw_ref[:, c0:c0 + GDN_QK]
        y = raw * cw[GDN_CONV - 1:GDN_CONV]
        for j in range(GDN_CONV - 1):
            r0 = halo - (GDN_CONV - 1) + j
            y = y + pad_ref[grp, r0:r0 + tm, :] * cw[j:j + 1]
        pad_ref[grp, 0:halo, :] = pad_ref[grp, tm:tm + halo, :]
        y = _silu(y)
        if grp < 2:
            scale = GDN_DK ** -0.5 if grp == 0 else 1.0
            for hd in range(GDN_HEADS):
                yh = y[:, hd * GDN_DK:(hd + 1) * GDN_DK]
                inv = lax.rsqrt(jnp.sum(yh * yh, axis=-1, keepdims=True) + NORM_EPS) * scale
                outs[grp][:, hd * GDN_DK:(hd + 1) * GDN_DK] = (yh * inv).astype(BF16)
        else:
            v_ref[...] = y.astype(BF16)

    c0 = 3 * GDN_QK
    z_ref[...] = _dot(h, wmain_ref[:, c0:c0 + GDN_QK]).astype(BF16)
    c0 += GDN_QK
    qs_ref[...] = _dot(h, wmain_ref[:, c0:c0 + SWA_Q]).astype(BF16)
    c0 += SWA_Q
    kvs_ref[...] = _dot(h, wmain_ref[:, c0:c0 + 2 * SWA_KV]).astype(BF16)


def _inproj(x, nw, wmain, wab, convw, alog_row, dtb_row, tm):
    B, T, D = x.shape
    n_main = wmain.shape[1]
    tok = lambda w: pl.BlockSpec((None, tm, w), lambda b, i: (b, i, 0))
    full = lambda a: pl.BlockSpec(a.shape, lambda b, i: (0,) * a.ndim)
    sds = lambda w, dt: jax.ShapeDtypeStruct((B, T, w), dt)
    return pl.pallas_call(
        _inproj_kernel,
        grid=(B, T // tm),
        in_specs=[tok(D), full(nw), full(wmain), full(wab), full(convw), full(alog_row), full(dtb_row)],
        out_specs=[tok(GDN_QK), tok(GDN_QK), tok(GDN_QK), tok(GDN_QK), tok(LANES), tok(SWA_Q), tok(2 * SWA_KV)],
        out_shape=[sds(GDN_QK, BF16), sds(GDN_QK, BF16), sds(GDN_QK, BF16), sds(GDN_QK, BF16),
                   sds(LANES, F32), sds(SWA_Q, BF16), sds(2 * SWA_KV, BF16)],
        scratch_shapes=[pltpu.VMEM((3, tm + SUBLANES, GDN_QK), F32)],
        compiler_params=pltpu.CompilerParams(
            dimension_semantics=("arbitrary", "arbitrary"), vmem_limit_bytes=VMEM_LIMIT_BYTES),
        name="inproj",
    )(x, nw, wmain, wab, convw, alog_row, dtb_row)


def _split3(a):
    hi = a.astype(BF16)
    r1 = a - hi.astype(F32)
    mid = r1.astype(BF16)
    lo = (r1 - mid.astype(F32)).astype(BF16)
    return hi, mid, lo


def _lane_col(a, idx):
    lane = lax.broadcasted_iota(jnp.int32, a.shape, 1)
    return jnp.sum(jnp.where(lane == idx, a, 0.0), axis=-1, keepdims=True)


def _gdn_kernel(q_ref, k_ref, v_ref, z_ref, gb_ref, nw_ref, o_ref, s_ref):
    R = GDN_SUPER
    C = GDN_CHUNK
    n_chunks = R // C
    shift = int(math.log2(C))

    @pl.when(pl.program_id(1) == 0)
    def _():
        s_ref[...] = jnp.zeros(s_ref.shape, F32)

    row = lax.broadcasted_iota(jnp.int32, (R, R), 0)
    col = lax.broadcasted_iota(jnp.int32, (R, R), 1)
    same = (row >> shift) == (col >> shift)
    causal = jnp.logical_and(same, row >= col)
    strict = jnp.logical_and(same, row > col)
    lower_bd = jnp.where(causal, 1.0, 0.0).astype(BF16)
    upper_bd = jnp.where(jnp.logical_and(same, row <= col), 1.0, 0.0).astype(BF16)
    ones_bd = jnp.where(same, 1.0, 0.0).astype(BF16)
    eye = jnp.where(row == col, 1.0, 0.0)

    gb = gb_ref[...]
    gb_parts = _split3(gb)
    gcum_cols = sum(_dot(lower_bd, p) for p in gb_parts)
    glast_cols = sum(_dot(ones_bd, p) for p in gb_parts)
    gcum_rows = sum(_dot(p, upper_bd) for p in _split3(gb.T))

    for hd in range(GDN_HEADS):
        lanes = slice(hd * GDN_DK, (hd + 1) * GDN_DK)
        gc = _lane_col(gcum_cols, hd)
        gl = _lane_col(glast_cols, hd)
        beta = _lane_col(gb, GDN_HEADS + hd)
        gr = gcum_rows[hd:hd + 1, :]
        decay = jnp.where(causal, jnp.exp(jnp.where(causal, gc - gr, 0.0)), 0.0)

        qh = q_ref[:, lanes]
        kh = k_ref[:, lanes]
        kf = kh.astype(F32)
        kb = kf * beta
        vb = v_ref[:, lanes].astype(F32) * beta
        eg = jnp.exp(jnp.broadcast_to(gc, (R, GDN_DK)))

        a = jnp.where(strict, _dot_nt(kb.astype(BF16), kh) * decay, 0.0)
        p = eye - a
        xpow = _dot(a.astype(BF16), a.astype(BF16))
        for _ in range(shift - 2):
            xb = xpow.astype(BF16)
            y = _dot(jnp.concatenate([p.astype(BF16), xb], axis=0), xb)
            p = p + y[:R]
            xpow = y[R:]
        tinv = p + _dot(p.astype(BF16), xpow.astype(BF16))

        uw = _dot(tinv.astype(BF16), jnp.concatenate([vb, kb * eg], axis=1).astype(BF16))
        u = uw[:, :GDN_DK]
        w = uw[:, GDN_DK:]
        qk = jnp.where(causal, _dot_nt(qh, kh) * decay, 0.0).astype(BF16)
        q_dec = (qh.astype(F32) * eg).astype(BF16)
        kd_t = (kf * jnp.exp(jnp.broadcast_to(gl - gc, (R, GDN_DK)))).T.astype(BF16)
        g_last = jnp.exp(jnp.broadcast_to(gl, (R, GDN_DK)))

        state = s_ref[hd]
        o_chunks = []
        for c in range(n_chunks):
            rows = slice(c * C, (c + 1) * C)
            wq = jnp.concatenate([w[rows].astype(BF16), q_dec[rows]], axis=0)
            r1 = _dot(wq, state.astype(BF16))
            v_new = (u[rows] - r1[:C]).astype(BF16)
            pieces = []
            if c > 0:
                pieces.append(jnp.zeros((c * C, GDN_DK), BF16))
            pieces.append(v_new)
            if c < n_chunks - 1:
                pieces.append(jnp.zeros((R - (c + 1) * C, GDN_DK), BF16))
            v_pad = jnp.concatenate(pieces, axis=0) if len(pieces) > 1 else pieces[0]
            r2 = _dot(jnp.concatenate([qk[rows], kd_t], axis=0), v_pad)
            o_chunks.append(r1[C:] + r2[:C])
            state = state * g_last[c * C:c * C + 1, :] + r2[C:]
        s_ref[hd] = state

        o = jnp.concatenate(o_chunks, axis=0)
        o = _rms(o, nw_ref[...]) * _silu(z_ref[:, lanes].astype(F32))
        o_ref[:, lanes] = o.astype(BF16)


def _gdn(q, k, v, z, gb, nw):
    B, T, _ = q.shape
    R = GDN_SUPER
    tok = lambda w: pl.BlockSpec((None, R, w), lambda b, i: (b, i, 0))
    return pl.pallas_call(
        _gdn_kernel,
        grid=(B, T // R),
        in_specs=[tok(GDN_QK), tok(GDN_QK), tok(GDN_QK), tok(GDN_QK), tok(LANES),
                  pl.BlockSpec(nw.shape, lambda b, i: (0, 0))],
        out_specs=tok(GDN_QK),
        out_shape=jax.ShapeDtypeStruct((B, T, GDN_QK), BF16),
        scratch_shapes=[pltpu.VMEM((GDN_HEADS, GDN_DK, GDN_DK), F32)],
        compiler_params=pltpu.CompilerParams(
            dimension_semantics=("arbitrary", "arbitrary"), vmem_limit_bytes=VMEM_LIMIT_BYTES),
        name="gdn",
    )(q, k, v, z, gb, nw)


def _band_tables():
    qi = np.arange(WINDOW, dtype=np.int64)[:, None]
    sj = np.arange(2 * WINDOW, dtype=np.int64)[None, :]
    dist = qi + WINDOW - sj
    in_band = (dist >= 0) & (dist < WINDOW)
    d = np.maximum(dist, 0)
    max_exact = REL_BUCKETS // 2
    ratio = np.log(np.maximum(d, 1).astype(np.float32) / np.float32(max_exact)) / np.float32(
        math.log(REL_MAX_DIST / max_exact))
    large = max_exact + (ratio.astype(np.float32) * np.float32(REL_BUCKETS - max_exact)).astype(np.int32)
    large = np.minimum(large, REL_BUCKETS - 1)
    bucket = np.where(d < max_exact, d, large).astype(np.int32)
    valid = np.stack([in_band & (sj >= WINDOW), in_band]).astype(np.int32)
    return bucket, valid


def _bias_kernel(table_ref, bucket_ref, valid_ref, o_ref):
    bucket = bucket_ref[...]
    for hd in range(SWA_Q_HEADS):
        acc = jnp.zeros(bucket.shape, F32)
        for b in range(REL_BUCKETS):
            acc = jnp.where(bucket == b, table_ref[b, hd], acc)
        for var in range(2):
            o_ref[var, hd] = jnp.where(valid_ref[var] != 0, acc, NEG_INF)


def _rel_bias(table):
    bucket, valid = _band_tables()
    return pl.pallas_call(
        _bias_kernel,
        in_specs=[pl.BlockSpec(memory_space=pltpu.SMEM),
                  pl.BlockSpec(memory_space=pltpu.VMEM), pl.BlockSpec(memory_space=pltpu.VMEM)],
        out_specs=pl.BlockSpec(memory_space=pltpu.VMEM),
        out_shape=jax.ShapeDtypeStruct((2, SWA_Q_HEADS, WINDOW, 2 * WINDOW), F32),
        name="rel_bias",
    )(table.astype(F32), jnp.asarray(bucket), jnp.asarray(valid))


def _swa_kernel(sinks_ref, qs_ref, kv_ref, bias_ref, nw_ref, o_ref):
    W = WINDOW
    T = qs_ref.shape[0]
    group = SWA_Q_HEADS // SWA_KV_HEADS
    scale = SWA_HEAD_DIM ** -0.5
    lo_half = lax.broadcasted_iota(jnp.int32, (W, LANES), 1) < SWA_HEAD_DIM

    def block(n, carry):
        r0 = pl.multiple_of(n * W, W)
        p0 = pl.multiple_of(jnp.maximum(n - 1, 0) * W, W)
        var = jnp.minimum(n, 1)
        q = qs_ref[pl.ds(r0, W), :]
        kv = jnp.concatenate([kv_ref[pl.ds(p0, W), :], kv_ref[pl.ds(r0, W), :]], axis=0).astype(F32)
        k_pair = kv[:, :LANES]
        v_pair = kv[:, LANES:]
        k_opts = (k_pair.astype(BF16), pltpu.roll(k_pair, SWA_HEAD_DIM, axis=1).astype(BF16))
        v_opts = (v_pair.astype(BF16), pltpu.roll(v_pair, SWA_HEAD_DIM, axis=1).astype(BF16))

        pairs = []
        ss = jnp.zeros((W, 1), F32)
        for pr in range(SWA_Q_HEADS // 2):
            halves = []
            for half in range(2):
                hd = 2 * pr + half
                kvh = hd // group
                sel = 0 if half == kvh else 1
                qp = q[:, pr * LANES:(pr + 1) * LANES]
                qm = jnp.where(lo_half if half == 0 else jnp.logical_not(lo_half), qp, jnp.zeros_like(qp))
                s = _dot_nt(qm, k_opts[sel]) * scale + bias_ref[var, hd]
                sink = sinks_ref[hd]
                m = jnp.maximum(jnp.max(s, axis=-1, keepdims=True), sink)
                e = jnp.exp(s - m)
                denom = jnp.sum(e, axis=-1, keepdims=True) + jnp.exp(sink - m)
                halves.append(_dot(e.astype(BF16), v_opts[sel]) / denom)
            o_pair = jnp.where(lo_half, halves[0], halves[1])
            ss = ss + jnp.sum(o_pair * o_pair, axis=-1, keepdims=True)
            pairs.append(o_pair)
        inv = lax.rsqrt(ss * (1.0 / SWA_Q) + NORM_EPS)
        for pr, o_pair in enumerate(pairs):
            lanes = slice(pr * LANES, (pr + 1) * LANES)
            o_ref[pl.ds(r0, W), lanes] = (o_pair * inv * nw_ref[:, lanes]).astype(BF16)
        return carry

    lax.fori_loop(0, T // W, block, 0)


def _swa(sinks, qs, kvs, bias, nw):
    B, T, _ = qs.shape
    seq = lambda w: pl.BlockSpec((None, T, w), lambda b: (b, 0, 0))
    return pl.pallas_call(
        _swa_kernel,
        grid=(B,),
        in_specs=[pl.BlockSpec(memory_space=pltpu.SMEM), seq(SWA_Q), seq(2 * SWA_KV),
                  pl.BlockSpec(bias.shape, lambda b: (0, 0, 0, 0)),
                  pl.BlockSpec(nw.shape, lambda b: (0, 0))],
        out_specs=seq(SWA_Q),
        out_shape=jax.ShapeDtypeStruct((B, T, SWA_Q), BF16),
        compiler_params=pltpu.CompilerParams(
            dimension_semantics=("arbitrary",), vmem_limit_bytes=VMEM_LIMIT_BYTES),
        name="swa",
    )(sinks, qs, kvs, bias, nw)


def _gelu_tanh(x):
    return 0.5 * x * (1.0 + jnp.tanh(math.sqrt(2.0 / math.pi) * (x + 0.044715 * (x * x * x))))


def _mlp_kernel(x_ref, og_ref, os_ref, wout_ref, pmw_ref, pfw_ref, wgate_ref, wup_ref, convw_ref, convb_ref,
                wdown_ref, postw_ref, o_ref, pad_ref, carry_ref, acc_ref):
    tm = x_ref.shape[0]
    halo = SUBLANES
    n_ff = wgate_ref.shape[0]

    @pl.when(pl.program_id(1) == 0)
    def _():
        carry_ref[...] = jnp.zeros(carry_ref.shape, F32)

    mix = _dot(og_ref[...], wout_ref[0:GDN_QK, :]) + _dot(os_ref[...], wout_ref[GDN_QK:, :])
    x1 = x_ref[...] + _rms(mix, pmw_ref[...])
    h = _rms(x1, pfw_ref[...]).astype(BF16)

    for c in range(n_ff):
        gate = _dot(h, wgate_ref[c])
        up = _dot(h, wup_ref[c])
        pad_ref[0:halo, :] = carry_ref[c]
        pad_ref[halo:halo + tm, :] = gate
        carry_ref[c] = gate[tm - halo:, :]
        cw = convw_ref[c]
        y = gate * cw[FFN_CONV - 1:FFN_CONV] + convb_ref[c]
        for j in range(FFN_CONV - 1):
            r0 = halo - (FFN_CONV - 1) + j
            y = y + pad_ref[r0:r0 + tm, :] * cw[j:j + 1]
        act = (_gelu_tanh(y) * up).astype(BF16)
        contrib = _dot(act, wdown_ref[c])
        if c == 0:
            acc_ref[...] = contrib
        else:
            acc_ref[...] += contrib

    o_ref[...] = x1 + _rms(acc_ref[...], postw_ref[...])


def _mlp(x, og, osw, wout, pmw, pfw, wgate, wup, convw, convb, wdown, postw, tm):
    B, T, D = x.shape
    n_ff = wgate.shape[0]
    tok = lambda w: pl.BlockSpec((None, tm, w), lambda b, i: (b, i, 0))
    full = lambda a: pl.BlockSpec(a.shape, lambda b, i: (0,) * a.ndim, pipeline_mode=pl.Buffered(1))
    return pl.pallas_call(
        _mlp_kernel,
        grid=(B, T // tm),
        in_specs=[tok(D), tok(GDN_QK), tok(SWA_Q), full(wout), full(pmw), full(pfw), full(wgate), full(wup),
                  full(convw), full(convb), full(wdown), full(postw)],
        out_specs=tok(D),
        out_shape=jax.ShapeDtypeStruct((B, T, D), x.dtype),
        scratch_shapes=[pltpu.VMEM((tm + SUBLANES, FF_CHUNK), F32),
                        pltpu.VMEM((n_ff, SUBLANES, FF_CHUNK), F32),
                        pltpu.VMEM((tm, D), F32)],
        compiler_params=pltpu.CompilerParams(
            dimension_semantics=("arbitrary", "arbitrary"), vmem_limit_bytes=VMEM_LIMIT_BYTES),
        name="mlp",
    )(x, og, osw, wout, pmw, pfw, wgate, wup, convw, convb, wdown, postw)


def _token_tile(T):
    for tm in (512, 256, 128):
        if T % tm == 0:
            return tm
    raise ValueError(f"sequence length {T} must be a multiple of 128")


def kernel(x, pre_mix_norm_w, w_in, gdn_conv_w, gdn_a_log, gdn_dt_bias, gdn_norm_w, swa_sinks, rel_bias_table,
           swa_norm_w, w_out, post_mix_norm_w, pre_ffn_norm_w, w_gate, w_up, ffn_conv_w, ffn_conv_b, w_down,
           post_ffn_norm_w):
    B, T, D = x.shape
    depth = w_in.shape[0]
    d_ff = w_gate.shape[-1]
    assert T % GDN_SUPER == 0 and T % WINDOW == 0 and d_ff % FF_CHUNK == 0
    tm = _token_tile(T)
    n_ff = d_ff // FF_CHUNK
    gdn_qkv = 3 * GDN_QK
    n_gate = 2 * GDN_HEADS
    ab0 = gdn_qkv + GDN_QK
    row = lambda a: a.reshape(1, -1).astype(F32)

    bias = _rel_bias(rel_bias_table)

    for l in range(depth):
        wl = w_in[l]
        wmain = jnp.concatenate([wl[:, :ab0], wl[:, ab0 + n_gate:]], axis=1).astype(BF16)
        wab = jnp.pad(wl[:, ab0:ab0 + n_gate], ((0, 0), (0, LANES - n_gate))).astype(BF16)
        alog_row = jnp.pad(row(gdn_a_log[l]), ((0, 0), (0, LANES - GDN_HEADS)))
        dtb_row = jnp.pad(row(gdn_dt_bias[l]), ((0, 0), (0, LANES - GDN_HEADS)))

        q, k, v, z, gb, qs, kvs = _inproj(x, row(pre_mix_norm_w[l]), wmain, wab, gdn_conv_w[l].astype(F32),
                                          alog_row, dtb_row, tm)
        o_g = _gdn(q, k, v, z, gb, row(gdn_norm_w[l]))
        o_s = _swa(swa_sinks[l].astype(F32), qs, kvs, bias, row(swa_norm_w[l]))

        chunked = lambda w: jnp.transpose(w.reshape(w.shape[0], n_ff, FF_CHUNK), (1, 0, 2))
        x = _mlp(x, o_g, o_s, w_out[l].astype(BF16), row(post_mix_norm_w[l]), row(pre_ffn_norm_w[l]),
                 chunked(w_gate[l]).astype(BF16), chunked(w_up[l]).astype(BF16),
                 chunked(ffn_conv_w[l].astype(F32)), chunked(row(ffn_conv_b[l])),
                 w_down[l].reshape(n_ff, FF_CHUNK, D).astype(BF16), row(post_ffn_norm_w[l]), tm)
    return x
```

```python
import functools
import math

import numpy as np
import jax
import jax.numpy as jnp
from jax import lax
from jax.experimental import pallas as pl
from jax.experimental.pallas import tpu as pltpu

F32 = jnp.float32
BF16 = jnp.bfloat16

GDN_HEADS = 4
GDN_DK = 128
GDN_CONV = 4
GDN_CHUNK = 64
SWA_Q_HEADS = 8
SWA_KV_HEADS = 2
SWA_HEAD_DIM = 64
WINDOW = 128
REL_BUCKETS = 32
REL_MAX_DIST = 128
FFN_CONV = 3
NORM_EPS = 1e-6
NEG_INF = -1e30

GDN_QK = GDN_HEADS * GDN_DK
SWA_Q = SWA_Q_HEADS * SWA_HEAD_DIM
SWA_KV = SWA_KV_HEADS * SWA_HEAD_DIM

LANES = 128
SUBLANES = 8
VMEM_LIMIT_BYTES = 56 * 1024 * 1024

GDN_SUPER = 4 * GDN_CHUNK
FF_CHUNK = 256


def _rms(x, w):
    return x * lax.rsqrt(jnp.mean(x * x, axis=-1, keepdims=True) + NORM_EPS) * w


def _sigmoid(x):
    return 1.0 / (1.0 + jnp.exp(-x))


def _silu(x):
    return x * _sigmoid(x)


def _dot(a, b):
    return jnp.dot(a, b, preferred_element_type=F32)


def _dot_nt(a, b):
    return lax.dot_general(a, b, (((1,), (1,)), ((), ())), preferred_element_type=F32)


def _inproj_kernel(x_ref, nw_ref, wmain_ref, wab_ref, convw_ref, alog_ref, dtb_ref,
                   q_ref, k_ref, v_ref, z_ref, gb_ref, qs_ref, kvs_ref, pad_ref):
    tm = x_ref.shape[0]
    halo = SUBLANES

    @pl.when(pl.program_id(1) == 0)
    def _():
        pad_ref[:, 0:halo, :] = jnp.zeros((3, halo, GDN_QK), F32)

    h = _rms(x_ref[...], nw_ref[...]).astype(BF16)

    ab = _dot(h, wab_ref[...])
    lane = lax.broadcasted_iota(jnp.int32, ab.shape, 1)
    pre = ab + dtb_ref[...]
    softplus = jnp.maximum(pre, 0.0) + jnp.log(1.0 + jnp.exp(-jnp.abs(pre)))
    g = -jnp.exp(alog_ref[...]) * softplus
    gb_ref[...] = jnp.where(lane < GDN_HEADS, g, _sigmoid(ab))

    outs = (q_ref, k_ref, v_ref)
    for grp in range(3):
        c0 = grp * GDN_QK
        raw = _dot(h, wmain_ref[:, c0:c0 + GDN_QK])
        pad_ref[grp, halo:halo + tm, :] = raw
        cw = convw_ref[:, c0:c0 + GDN_QK]
        y = raw * cw[GDN_CONV - 1:GDN_CONV]
        for j in range(GDN_CONV - 1):
            r0 = halo - (GDN_CONV - 1) + j
            y = y + pad_ref[grp, r0:r0 + tm, :] * cw[j:j + 1]
        pad_ref[grp, 0:halo, :] = pad_ref[grp, tm:tm + halo, :]
        y = _silu(y)
        if grp < 2:
            scale = GDN_DK ** -0.5 if grp == 0 else 1.0
            for hd in range(GDN_HEADS):
                yh = y[:, hd * GDN_DK:(hd + 1) * GDN_DK]
                inv = lax.rsqrt(jnp.sum(yh * yh, axis=-1, keepdims=True) + NORM_EPS) * scale
                outs[grp][:, hd * GDN_DK:(hd + 1) * GDN_DK] = (yh * inv).astype(BF16)
        else:
            v_ref[...] = y.astype(BF16)

    c0 = 3 * GDN_QK
    z_ref[...] = _dot(h, wmain_ref[:, c0:c0 + GDN_QK]).astype(BF16)
    c0 += GDN_QK
    qs_ref[...] = _dot(h, wmain_ref[:, c0:c0 + SWA_Q]).astype(BF16)
    c0 += SWA_Q
    kvs_ref[...] = _dot(h, wmain_ref[:, c0:c0 + 2 * SWA_KV]).astype(BF16)


def _inproj(x, nw, wmain, wab, convw, alog_row, dtb_row, tm):
    B, T, D = x.shape
    n_main = wmain.shape[1]
    tok = lambda w: pl.BlockSpec((None, tm, w), lambda b, i: (b, i, 0))
    full = lambda a: pl.BlockSpec(a.shape, lambda b, i: (0,) * a.ndim)
    sds = lambda w, dt: jax.ShapeDtypeStruct((B, T, w), dt)
    return pl.pallas_call(
        _inproj_kernel,
        grid=(B, T // tm),
        in_specs=[tok(D), full(nw), full(wmain), full(wab), full(convw), full(alog_row), full(dtb_row)],
        out_specs=[tok(GDN_QK), tok(GDN_QK), tok(GDN_QK), tok(GDN_QK), tok(LANES), tok(SWA_Q), tok(2 * SWA_KV)],
        out_shape=[sds(GDN_QK, BF16), sds(GDN_QK, BF16), sds(GDN_QK, BF16), sds(GDN_QK, BF16),
                   sds(LANES, F32), sds(SWA_Q, BF16), sds(2 * SWA_KV, BF16)],
        scratch_shapes=[pltpu.VMEM((3, tm + SUBLANES, GDN_QK), F32)],
        compiler_params=pltpu.CompilerParams(
            dimension_semantics=("arbitrary", "arbitrary"), vmem_limit_bytes=VMEM_LIMIT_BYTES),
        name="inproj",
    )(x, nw, wmain, wab, convw, alog_row, dtb_row)


def _split3(a):
    hi = a.astype(BF16)
    r1 = a - hi.astype(F32)
    mid = r1.astype(BF16)
    lo = (r1 - mid.astype(F32)).astype(BF16)
    return hi, mid, lo


def _lane_col(a, idx):
    lane = lax.broadcasted_iota(jnp.int32, a.shape, 1)
    return jnp.sum(jnp.where(lane == idx, a, 0.0), axis=-1, keepdims=True)


def _gdn_kernel(q_ref, k_ref, v_ref, z_ref, gb_ref, nw_ref, o_ref, s_ref):
    R = GDN_SUPER
    C = GDN_CHUNK
    n_chunks = R // C
    shift = int(math.log2(C))

    @pl.when(pl.program_id(1) == 0)
    def _():
        s_ref[...] = jnp.zeros(s_ref.shape, F32)

    row = lax.broadcasted_iota(jnp.int32, (R, R), 0)
    col = lax.broadcasted_iota(jnp.int32, (R, R), 1)
    same = (row >> shift) == (col >> shift)
    causal = jnp.logical_and(same, row >= col)
    strict = jnp.logical_and(same, row > col)
    lower_bd = jnp.where(causal, 1.0, 0.0).astype(BF16)
    upper_bd = jnp.where(jnp.logical_and(same, row <= col), 1.0, 0.0).astype(BF16)
    ones_bd = jnp.where(same, 1.0, 0.0).astype(BF16)
    eye = jnp.where(row == col, 1.0, 0.0)

    gb = gb_ref[...]
    gb_parts = _split3(gb)
    gcum_cols = sum(_dot(lower_bd, p) for p in gb_parts)
    glast_cols = sum(_dot(ones_bd, p) for p in gb_parts)
    gcum_rows = sum(_dot(p, upper_bd) for p in _split3(gb.T))

    heads = range(GDN_HEADS)
    lanes = [slice(hd * GDN_DK, (hd + 1) * GDN_DK) for hd in heads]
    wide = lambda col: jnp.broadcast_to(col, (R, GDN_DK))
    gc = [_lane_col(gcum_cols, hd) for hd in heads]
    gl = [_lane_col(glast_cols, hd) for hd in heads]
    beta = [_lane_col(gb, GDN_HEADS + hd) for hd in heads]
    decay = [jnp.where(causal, jnp.exp(jnp.where(causal, gc[hd] - gcum_rows[hd:hd + 1, :], 0.0)), 0.0)
             for hd in heads]
    qh = [q_ref[:, lanes[hd]] for hd in heads]
    kh = [k_ref[:, lanes[hd]] for hd in heads]
    kf = [kh[hd].astype(F32) for hd in heads]
    kb = [kf[hd] * beta[hd] for hd in heads]
    eg = [jnp.exp(wide(gc[hd])) for hd in heads]
    rhs = [jnp.concatenate([v_ref[:, lanes[hd]].astype(F32) * beta[hd], kb[hd] * eg[hd]], axis=1).astype(BF16)
           for hd in heads]

    a = [jnp.where(strict, _dot_nt(kb[hd].astype(BF16), kh[hd]) * decay[hd], 0.0) for hd in heads]
    p = [eye - a[hd] for hd in heads]
    xb = [a[hd].astype(BF16) for hd in heads]
    xpow = [_dot(xb[hd], xb[hd]) for hd in heads]
    for _ in range(shift - 2):
        xb = [xpow[hd].astype(BF16) for hd in heads]
        y = [_dot(jnp.concatenate([p[hd].astype(BF16), xb[hd]], axis=0), xb[hd]) for hd in heads]
        p = [p[hd] + y[hd][:R] for hd in heads]
        xpow = [y[hd][R:] for hd in heads]
    tinv = [p[hd] + _dot(p[hd].astype(BF16), xpow[hd].astype(BF16)) for hd in heads]

    uw = [_dot(tinv[hd].astype(BF16), rhs[hd]).astype(BF16) for hd in heads]
    qk = [jnp.where(causal, _dot_nt(qh[hd], kh[hd]) * decay[hd], 0.0).astype(BF16) for hd in heads]
    qkuw = [_dot(qk[hd], uw[hd]) for hd in heads]
    q_eff = [(qh[hd].astype(F32) * eg[hd] - qkuw[hd][:, GDN_DK:]).astype(BF16) for hd in heads]
    kd_t = [(kf[hd] * jnp.exp(wide(gl[hd] - gc[hd]))).T.astype(BF16) for hd in heads]
    g_last = [jnp.exp(wide(gl[hd])) for hd in heads]
    row_chunk = lax.broadcasted_iota(jnp.int32, (R, 2 * GDN_DK), 0) >> shift
    zero_uw = jnp.zeros((R, 2 * GDN_DK), BF16)
    nm = [[_dot(kd_t[hd], jnp.where(row_chunk == c, uw[hd], zero_uw)) for c in range(n_chunks)] for hd in heads]

    state = [s_ref[hd] for hd in heads]
    o_chunks = [[] for _ in heads]
    for c in range(n_chunks):
        rows = slice(c * C, (c + 1) * C)
        for hd in heads:
            lhs = jnp.concatenate([q_eff[hd][rows], nm[hd][c][:, GDN_DK:].astype(BF16)], axis=0)
            r = _dot(lhs, state[hd].astype(BF16))
            o_chunks[hd].append(r[:C] + qkuw[hd][rows, :GDN_DK])
            state[hd] = state[hd] * g_last[hd][c * C:c * C + 1, :] - r[C:] + nm[hd][c][:, :GDN_DK]

    for hd in heads:
        s_ref[hd] = state[hd]
        o = jnp.concatenate(o_chunks[hd], axis=0)
        o = _rms(o, nw_ref[...]) * _silu(z_ref[:, lanes[hd]].astype(F32))
        o_ref[:, lanes[hd]] = o.astype(BF16)


def _gdn(q, k, v, z, gb, nw):
    B, T, _ = q.shape
    R = GDN_SUPER
    tok = lambda w: pl.BlockSpec((None, R, w), lambda b, i: (b, i, 0))
    return pl.pallas_call(
        _gdn_kernel,
        grid=(B, T // R),
        in_specs=[tok(GDN_QK), tok(GDN_QK), tok(GDN_QK), tok(GDN_QK), tok(LANES),
                  pl.BlockSpec(nw.shape, lambda b, i: (0, 0))],
        out_specs=tok(GDN_QK),
        out_shape=jax.ShapeDtypeStruct((B, T, GDN_QK), BF16),
        scratch_shapes=[pltpu.VMEM((GDN_HEADS, GDN_DK, GDN_DK), F32)],
        compiler_params=pltpu.CompilerParams(
            dimension_semantics=("arbitrary", "arbitrary"), vmem_limit_bytes=VMEM_LIMIT_BYTES),
        name="gdn",
    )(q, k, v, z, gb, nw)


def _band_tables():
    qi = np.arange(WINDOW, dtype=np.int64)[:, None]
    sj = np.arange(2 * WINDOW, dtype=np.int64)[None, :]
    dist = qi + WINDOW - sj
    in_band = (dist >= 0) & (dist < WINDOW)
    d = np.maximum(dist, 0)
    max_exact = REL_BUCKETS // 2
    ratio = np.log(np.maximum(d, 1).astype(np.float32) / np.float32(max_exact)) / np.float32(
        math.log(REL_MAX_DIST / max_exact))
    large = max_exact + (ratio.astype(np.float32) * np.float32(REL_BUCKETS - max_exact)).astype(np.int32)
    large = np.minimum(large, REL_BUCKETS - 1)
    bucket = np.where(d < max_exact, d, large).astype(np.int32)
    valid = np.stack([in_band & (sj >= WINDOW), in_band]).astype(np.int32)
    return bucket, valid


def _bias_kernel(table_ref, bucket_ref, valid_ref, o_ref):
    bucket = bucket_ref[...]
    for hd in range(SWA_Q_HEADS):
        acc = jnp.zeros(bucket.shape, F32)
        for b in range(REL_BUCKETS):
            acc = jnp.where(bucket == b, table_ref[b, hd], acc)
        for var in range(2):
            o_ref[var, hd] = jnp.where(valid_ref[var] != 0, acc, NEG_INF)


def _rel_bias(table):
    bucket, valid = _band_tables()
    return pl.pallas_call(
        _bias_kernel,
        in_specs=[pl.BlockSpec(memory_space=pltpu.SMEM),
                  pl.BlockSpec(memory_space=pltpu.VMEM), pl.BlockSpec(memory_space=pltpu.VMEM)],
        out_specs=pl.BlockSpec(memory_space=pltpu.VMEM),
        out_shape=jax.ShapeDtypeStruct((2, SWA_Q_HEADS, WINDOW, 2 * WINDOW), F32),
        name="rel_bias",
    )(table.astype(F32), jnp.asarray(bucket), jnp.asarray(valid))


def _swa_kernel(sinks_ref, qs_ref, kv_ref, bias_ref, nw_ref, o_ref):
    W = WINDOW
    T = qs_ref.shape[0]
    group = SWA_Q_HEADS // SWA_KV_HEADS
    scale = SWA_HEAD_DIM ** -0.5
    lo_half = lax.broadcasted_iota(jnp.int32, (W, LANES), 1) < SWA_HEAD_DIM

    def block(n, carry):
        r0 = pl.multiple_of(n * W, W)
        p0 = pl.multiple_of(jnp.maximum(n - 1, 0) * W, W)
        var = jnp.minimum(n, 1)
        q = qs_ref[pl.ds(r0, W), :]
        kv = jnp.concatenate([kv_ref[pl.ds(p0, W), :], kv_ref[pl.ds(r0, W), :]], axis=0).astype(F32)
        k_pair = kv[:, :LANES]
        v_pair = kv[:, LANES:]
        k_opts = (k_pair.astype(BF16), pltpu.roll(k_pair, SWA_HEAD_DIM, axis=1).astype(BF16))
        v_opts = (v_pair.astype(BF16), pltpu.roll(v_pair, SWA_HEAD_DIM, axis=1).astype(BF16))

        pairs = []
        ss = jnp.zeros((W, 1), F32)
        for pr in range(SWA_Q_HEADS // 2):
            halves = []
            for half in range(2):
                hd = 2 * pr + half
                kvh = hd // group
                sel = 0 if half == kvh else 1
                qp = q[:, pr * LANES:(pr + 1) * LANES]
                qm = jnp.where(lo_half if half == 0 else jnp.logical_not(lo_half), qp, jnp.zeros_like(qp))
                s = _dot_nt(qm, k_opts[sel]) * scale + bias_ref[var, hd]
                sink = sinks_ref[hd]
                m = jnp.maximum(jnp.max(s, axis=-1, keepdims=True), sink)
                e = jnp.exp(s - m)
                denom = jnp.sum(e, axis=-1, keepdims=True) + jnp.exp(sink - m)
                halves.append(_dot(e.astype(BF16), v_opts[sel]) / denom)
            o_pair = jnp.where(lo_half, halves[0], halves[1])
            ss = ss + jnp.sum(o_pair * o_pair, axis=-1, keepdims=True)
            pairs.append(o_pair)
        inv = lax.rsqrt(ss * (1.0 / SWA_Q) + NORM_EPS)
        for pr, o_pair in enumerate(pairs):
            lanes = slice(pr * LANES, (pr + 1) * LANES)
            o_ref[pl.ds(r0, W), lanes] = (o_pair * inv * nw_ref[:, lanes]).astype(BF16)
        return carry

    lax.fori_loop(0, T // W, block, 0)


def _swa(sinks, qs, kvs, bias, nw):
    B, T, _ = qs.shape
    seq = lambda w: pl.BlockSpec((None, T, w), lambda b: (b, 0, 0))
    return pl.pallas_call(
        _swa_kernel,
        grid=(B,),
        in_specs=[pl.BlockSpec(memory_space=pltpu.SMEM), seq(SWA_Q), seq(2 * SWA_KV),
                  pl.BlockSpec(bias.shape, lambda b: (0, 0, 0, 0)),
                  pl.BlockSpec(nw.shape, lambda b: (0, 0))],
        out_specs=seq(SWA_Q),
        out_shape=jax.ShapeDtypeStruct((B, T, SWA_Q), BF16),
        compiler_params=pltpu.CompilerParams(
            dimension_semantics=("arbitrary",), vmem_limit_bytes=VMEM_LIMIT_BYTES),
        name="swa",
    )(sinks, qs, kvs, bias, nw)


def _gelu_tanh(x):
    return 0.5 * x * (1.0 + jnp.tanh(math.sqrt(2.0 / math.pi) * (x + 0.044715 * (x * x * x))))


def _mlp_kernel(x_ref, og_ref, os_ref, wout_ref, pmw_ref, pfw_ref, wgate_ref, wup_ref, convw_ref, convb_ref,
                wdown_ref, postw_ref, o_ref, pad_ref, carry_ref, acc_ref):
    tm = x_ref.shape[0]
    halo = SUBLANES
    n_ff = wgate_ref.shape[0]

    @pl.when(pl.program_id(1) == 0)
    def _():
        carry_ref[...] = jnp.zeros(carry_ref.shape, F32)

    mix = _dot(og_ref[...], wout_ref[0:GDN_QK, :]) + _dot(os_ref[...], wout_ref[GDN_QK:, :])
    x1 = x_ref[...] + _rms(mix, pmw_ref[...])
    h = _rms(x1, pfw_ref[...]).astype(BF16)

    for c in range(n_ff):
        gate = _dot(h, wgate_ref[c])
        up = _dot(h, wup_ref[c])
        pad_ref[0:halo, :] = carry_ref[c]
        pad_ref[halo:halo + tm, :] = gate
        carry_ref[c] = gate[tm - halo:, :]
        cw = convw_ref[c]
        y = gate * cw[FFN_CONV - 1:FFN_CONV] + convb_ref[c]
        for j in range(FFN_CONV - 1):
            r0 = halo - (FFN_CONV - 1) + j
            y = y + pad_ref[r0:r0 + tm, :] * cw[j:j + 1]
        act = (_gelu_tanh(y) * up).astype(BF16)
        contrib = _dot(act, wdown_ref[c])
        if c == 0:
            acc_ref[...] = contrib
        else:
            acc_ref[...] += contrib

    o_ref[...] = x1 + _rms(acc_ref[...], postw_ref[...])


def _mlp(x, og, osw, wout, pmw, pfw, wgate, wup, convw, convb, wdown, postw, tm):
    B, T, D = x.shape
    n_ff = wgate.shape[0]
    tok = lambda w: pl.BlockSpec((None, tm, w), lambda b, i: (b, i, 0))
    full = lambda a: pl.BlockSpec(a.shape, lambda b, i: (0,) * a.ndim, pipeline_mode=pl.Buffered(1))
    return pl.pallas_call(
        _mlp_kernel,
        grid=(B, T // tm),
        in_specs=[tok(D), tok(GDN_QK), tok(SWA_Q), full(wout), full(pmw), full(pfw), full(wgate), full(wup),
                  full(convw), full(convb), full(wdown), full(postw)],
        out_specs=tok(D),
        out_shape=jax.ShapeDtypeStruct((B, T, D), x.dtype),
        scratch_shapes=[pltpu.VMEM((tm + SUBLANES, FF_CHUNK), F32),
                        pltpu.VMEM((n_ff, SUBLANES, FF_CHUNK), F32),
                        pltpu.VMEM((tm, D), F32)],
        compiler_params=pltpu.CompilerParams(
            dimension_semantics=("arbitrary", "arbitrary"), vmem_limit_bytes=VMEM_LIMIT_BYTES),
        name="mlp",
    )(x, og, osw, wout, pmw, pfw, wgate, wup, convw, convb, wdown, postw)


def _token_tile(T):
    for tm in (512, 256, 128):
        if T % tm == 0:
            return tm
    raise ValueError(f"sequence length {T} must be a multiple of 128")


def kernel(x, pre_mix_norm_w, w_in, gdn_conv_w, gdn_a_log, gdn_dt_bias, gdn_norm_w, swa_sinks, rel_bias_table,
           swa_norm_w, w_out, post_mix_norm_w, pre_ffn_norm_w, w_gate, w_up, ffn_conv_w, ffn_conv_b, w_down,
           post_ffn_norm_w):
    B, T, D = x.shape
    depth = w_in.shape[0]
    d_ff = w_gate.shape[-1]
    assert T % GDN_SUPER == 0 and T % WINDOW == 0 and d_ff % FF_CHUNK == 0
    tm = _token_tile(T)
    n_ff = d_ff // FF_CHUNK
    gdn_qkv = 3 * GDN_QK
    n_gate = 2 * GDN_HEADS
    ab0 = gdn_qkv + GDN_QK
    row = lambda a: a.reshape(1, -1).astype(F32)

    bias = _rel_bias(rel_bias_table)

    for l in range(depth):
        wl = w_in[l]
        wmain = jnp.concatenate([wl[:, :ab0], wl[:, ab0 + n_gate:]], axis=1).astype(BF16)
        wab = jnp.pad(wl[:, ab0:ab0 + n_gate], ((0, 0), (0, LANES - n_gate))).astype(BF16)
        alog_row = jnp.pad(row(gdn_a_log[l]), ((0, 0), (0, LANES - GDN_HEADS)))
        dtb_row = jnp.pad(row(gdn_dt_bias[l]), ((0, 0), (0, LANES - GDN_HEADS)))

        q, k, v, z, gb, qs, kvs = _inproj(x, row(pre_mix_norm_w[l]), wmain, wab, gdn_conv_w[l].astype(F32),
                                          alog_row, dtb_row, tm)
        o_g = _gdn(q, k, v, z, gb, row(gdn_norm_w[l]))
        o_s = _swa(swa_sinks[l].astype(F32), qs, kvs, bias, row(swa_norm_w[l]))

        chunked = lambda w: jnp.transpose(w.reshape(w.shape[0], n_ff, FF_CHUNK), (1, 0, 2))
        x = _mlp(x, o_g, o_s, w_out[l].astype(BF16), row(post_mix_norm_w[l]), row(pre_ffn_norm_w[l]),
                 chunked(w_gate[l]).astype(BF16), chunked(w_up[l]).astype(BF16),
                 chunked(ffn_conv_w[l].astype(F32)), chunked(row(ffn_conv_b[l])),
                 w_down[l].reshape(n_ff, FF_CHUNK, D).astype(BF16), row(post_ffn_norm_w[l]), tm)
    return x
```

```python
import functools
import math

import numpy as np
import jax
import jax.numpy as jnp
from jax import lax
from jax.experimental import pallas as pl
from jax.experimental.pallas import tpu as pltpu

F32 = jnp.float32
BF16 = jnp.bfloat16

GDN_HEADS = 4
GDN_DK = 128
GDN_CONV = 4
GDN_CHUNK = 64
SWA_Q_HEADS = 8
SWA_KV_HEADS = 2
SWA_HEAD_DIM = 64
WINDOW = 128
REL_BUCKETS = 32
REL_MAX_DIST = 128
FFN_CONV = 3
NORM_EPS = 1e-6
NEG_INF = -1e30

GDN_QK = GDN_HEADS * GDN_DK
SWA_Q = SWA_Q_HEADS * SWA_HEAD_DIM
SWA_KV = SWA_KV_HEADS * SWA_HEAD_DIM

LANES = 128
SUBLANES = 8
VMEM_LIMIT_BYTES = 56 * 1024 * 1024

GDN_SUPER = 4 * GDN_CHUNK
GDN_STEP = 2 * GDN_SUPER
FF_CHUNK = 256


def _rms(x, w):
    return x * lax.rsqrt(jnp.mean(x * x, axis=-1, keepdims=True) + NORM_EPS) * w


def _sigmoid(x):
    return 1.0 / (1.0 + jnp.exp(-x))


def _silu(x):
    return x * _sigmoid(x)


def _dot(a, b):
    return jnp.dot(a, b, preferred_element_type=F32)


def _dot_nt(a, b):
    return lax.dot_general(a, b, (((1,), (1,)), ((), ())), preferred_element_type=F32)


def _inproj_kernel(x_ref, nw_ref, wmain_ref, wab_ref, convw_ref, alog_ref, dtb_ref,
                   q_ref, k_ref, v_ref, z_ref, gb_ref, qs_ref, kvs_ref, pad_ref):
    tm = x_ref.shape[0]
    halo = SUBLANES

    @pl.when(pl.program_id(1) == 0)
    def _():
        pad_ref[:, 0:halo, :] = jnp.zeros((3, halo, GDN_QK), F32)

    h = _rms(x_ref[...], nw_ref[...]).astype(BF16)

    ab = _dot(h, wab_ref[...])
    lane = lax.broadcasted_iota(jnp.int32, ab.shape, 1)
    pre = ab + dtb_ref[...]
    softplus = jnp.maximum(pre, 0.0) + jnp.log(1.0 + jnp.exp(-jnp.abs(pre)))
    g = -jnp.exp(alog_ref[...]) * softplus
    gb_ref[...] = jnp.where(lane < GDN_HEADS, g, _sigmoid(ab))

    outs = (q_ref, k_ref, v_ref)
    for grp in range(3):
        c0 = grp * GDN_QK
        raw = _dot(h, wmain_ref[:, c0:c0 + GDN_QK])
        pad_ref[grp, halo:halo + tm, :] = raw
        cw = convw_ref[:, c0:c0 + GDN_QK]
        y = raw * cw[GDN_CONV - 1:GDN_CONV]
        for j in range(GDN_CONV - 1):
            r0 = halo - (GDN_CONV - 1) + j
            y = y + pad_ref[grp, r0:r0 + tm, :] * cw[j:j + 1]
        pad_ref[grp, 0:halo, :] = pad_ref[grp, tm:tm + halo, :]
        y = _silu(y)
        if grp < 2:
            scale = GDN_DK ** -0.5 if grp == 0 else 1.0
            for hd in range(GDN_HEADS):
                yh = y[:, hd * GDN_DK:(hd + 1) * GDN_DK]
                inv = lax.rsqrt(jnp.sum(yh * yh, axis=-1, keepdims=True) + NORM_EPS) * scale
                outs[grp][:, hd * GDN_DK:(hd + 1) * GDN_DK] = (yh * inv).astype(BF16)
        else:
            v_ref[...] = y.astype(BF16)

    c0 = 3 * GDN_QK
    z_ref[...] = _dot(h, wmain_ref[:, c0:c0 + GDN_QK]).astype(BF16)
    c0 += GDN_QK
    qs_ref[...] = _dot(h, wmain_ref[:, c0:c0 + SWA_Q]).astype(BF16)
    c0 += SWA_Q
    kvs_ref[...] = _dot(h, wmain_ref[:, c0:c0 + 2 * SWA_KV]).astype(BF16)


def _inproj(x, nw, wmain, wab, convw, alog_row, dtb_row, tm):
    B, T, D = x.shape
    tok = lambda w: pl.BlockSpec((None, tm, w), lambda b, i: (b, i, 0))
    full = lambda a: pl.BlockSpec(a.shape, lambda b, i: (0,) * a.ndim)
    sds = lambda w, dt: jax.ShapeDtypeStruct((B, T, w), dt)
    return pl.pallas_call(
        _inproj_kernel,
        grid=(B, T // tm),
        in_specs=[tok(D), full(nw), full(wmain), full(wab), full(convw), full(alog_row), full(dtb_row)],
        out_specs=[tok(GDN_QK), tok(GDN_QK), tok(GDN_QK), tok(GDN_QK), tok(LANES), tok(SWA_Q), tok(2 * SWA_KV)],
        out_shape=[sds(GDN_QK, BF16), sds(GDN_QK, BF16), sds(GDN_QK, BF16), sds(GDN_QK, BF16),
                   sds(LANES, F32), sds(SWA_Q, BF16), sds(2 * SWA_KV, BF16)],
        scratch_shapes=[pltpu.VMEM((3, tm + SUBLANES, GDN_QK), F32)],
        compiler_params=pltpu.CompilerParams(
            dimension_semantics=("arbitrary", "arbitrary"), vmem_limit_bytes=VMEM_LIMIT_BYTES),
        name="inproj",
    )(x, nw, wmain, wab, convw, alog_row, dtb_row)


def _split3(a):
    hi = a.astype(BF16)
    r1 = a - hi.astype(F32)
    mid = r1.astype(BF16)
    lo = (r1 - mid.astype(F32)).astype(BF16)
    return hi, mid, lo


def _lane_col(a, idx):
    lane = lax.broadcasted_iota(jnp.int32, a.shape, 1)
    return jnp.sum(jnp.where(lane == idx, a, 0.0), axis=-1, keepdims=True)


def _gdn_kernel(q_ref, k_ref, v_ref, z_ref, gb_ref, nw_ref, o_ref, s_ref):
    R = GDN_SUPER
    C = GDN_CHUNK
    n_chunks = R // C
    n_super = q_ref.shape[0] // R
    shift = int(math.log2(C))
    heads = range(GDN_HEADS)
    units = [(sc, hd) for sc in range(n_super) for hd in heads]

    @pl.when(pl.program_id(1) == 0)
    def _():
        s_ref[...] = jnp.zeros(s_ref.shape, F32)

    row = lax.broadcasted_iota(jnp.int32, (R, R), 0)
    col = lax.broadcasted_iota(jnp.int32, (R, R), 1)
    same = (row >> shift) == (col >> shift)
    causal = jnp.logical_and(same, row >= col)
    strict = jnp.logical_and(same, row > col)
    eye_packed = jnp.where((lax.broadcasted_iota(jnp.int32, (C, R), 1) & (C - 1))
                           == lax.broadcasted_iota(jnp.int32, (C, R), 0), 1.0, 0.0)
    sums_to = jnp.concatenate([jnp.where(jnp.logical_and(same, row <= col), 1.0, 0.0),
                               jnp.where(same, 1.0, 0.0)], axis=1).astype(BF16)

    def pack(full):
        return sum(full[j * C:(j + 1) * C] for j in range(1, n_chunks)) + full[0:C]

    def block_diag(packed):
        return jnp.where(same, jnp.concatenate([packed] * n_chunks, axis=0), jnp.zeros((R, R), BF16))

    gbs, gcum_rows, cols = [], [], []
    for sc in range(n_super):
        gb = gb_ref[sc * R:(sc + 1) * R, :]
        hi, mid, lo = (p.astype(F32)[0:SUBLANES] for p in _split3(gb.T))
        parts = jnp.concatenate([hi, mid, lo, jnp.zeros_like(hi)], axis=0).astype(BF16)
        sums = _dot(parts, sums_to)
        sums = sums[0:SUBLANES] + sums[SUBLANES:2 * SUBLANES] + sums[2 * SUBLANES:3 * SUBLANES]
        gbs.append(gb)
        gcum_rows.append(sums[:, :R])
        cols.append(jnp.concatenate([sums[:, :R], sums[:, R:], jnp.zeros((LANES - 2 * SUBLANES, R), F32)], axis=0).T)

    rows_of = lambda sc: slice(sc * R, (sc + 1) * R)
    lanes_of = lambda hd: slice(hd * GDN_DK, (hd + 1) * GDN_DK)
    wide = lambda c: jnp.broadcast_to(c, (R, GDN_DK))
    gc = [_lane_col(cols[sc], hd) for sc, hd in units]
    gl = [_lane_col(cols[sc], SUBLANES + hd) for sc, hd in units]
    beta = [_lane_col(gbs[sc], GDN_HEADS + hd) for sc, hd in units]
    n_units = range(len(units))
    decay = [jnp.where(causal, jnp.exp(jnp.where(causal, gc[u] - gcum_rows[sc][hd:hd + 1, :], 0.0)), 0.0)
             for u, (sc, hd) in enumerate(units)]
    qh = [q_ref[rows_of(sc), lanes_of(hd)] for sc, hd in units]
    kh = [k_ref[rows_of(sc), lanes_of(hd)] for sc, hd in units]
    kf = [kh[u].astype(F32) for u in n_units]
    kb = [kf[u] * beta[u] for u in n_units]
    eg = [jnp.exp(wide(gc[u])) for u in n_units]
    rhs = [jnp.concatenate([v_ref[rows_of(sc), lanes_of(hd)].astype(F32) * beta[u], kb[u] * eg[u]],
                           axis=1).astype(BF16) for u, (sc, hd) in enumerate(units)]

    kq = [_dot_nt(jnp.concatenate([kb[u].astype(BF16), qh[u]], axis=0), kh[u]) for u in n_units]
    a = [jnp.where(strict, kq[u][:R] * decay[u], 0.0) for u in n_units]
    qk = [jnp.where(causal, kq[u][R:] * decay[u], 0.0).astype(BF16) for u in n_units]

    a_packed = [pack(a[u]) for u in n_units]
    p = [eye_packed - a_packed[u] for u in n_units]
    xpow = [_dot(a_packed[u].astype(BF16), a[u].astype(BF16)) for u in n_units]
    for _ in range(shift - 2):
        xb = [xpow[u].astype(BF16) for u in n_units]
        y = [_dot(jnp.concatenate([p[u].astype(BF16), xb[u]], axis=0), block_diag(xb[u])) for u in n_units]
        p = [p[u] + y[u][:C] for u in n_units]
        xpow = [y[u][C:] for u in n_units]
    p = [p[u] + _dot(p[u].astype(BF16), block_diag(xpow[u].astype(BF16))) for u in n_units]

    uw = [_dot(block_diag(p[u].astype(BF16)), rhs[u]).astype(BF16) for u in n_units]
    qkuw = [_dot(qk[u], uw[u]) for u in n_units]
    q_eff = [(qh[u].astype(F32) * eg[u] - qkuw[u][:, GDN_DK:]).astype(BF16) for u in n_units]
    kd_t = [(kf[u] * jnp.exp(wide(gl[u] - gc[u]))).T.astype(BF16) for u in n_units]
    g_last = [jnp.exp(wide(gl[u])) for u in n_units]
    row_chunk = lax.broadcasted_iota(jnp.int32, (R, 2 * GDN_DK), 0) >> shift
    zero_uw = jnp.zeros((R, 2 * GDN_DK), BF16)
    nm = [[_dot(kd_t[u], jnp.where(row_chunk == c, uw[u], zero_uw)) for c in range(n_chunks)] for u in n_units]

    state = [s_ref[hd] for hd in heads]
    zero_s = jnp.zeros((GDN_DK, GDN_DK), BF16)
    for sc in range(n_super):
        o_chunks = [[] for _ in heads]
        for c in range(n_chunks):
            rows = slice(c * C, (c + 1) * C)
            for h0 in range(0, GDN_HEADS, 2):
                h1 = h0 + 1
                u0, u1 = sc * GDN_HEADS + h0, sc * GDN_HEADS + h1
                lhs = jnp.concatenate(
                    [jnp.concatenate([q_eff[u0][rows], q_eff[u1][rows]], axis=1),
                     jnp.concatenate([nm[u0][c][:, GDN_DK:], nm[u1][c][:, GDN_DK:]], axis=1).astype(BF16)], axis=0)
                s_pair = jnp.concatenate(
                    [jnp.concatenate([state[h0].astype(BF16), zero_s], axis=1),
                     jnp.concatenate([zero_s, state[h1].astype(BF16)], axis=1)], axis=0)
                r = _dot(lhs, s_pair)
                for i, (hd, u) in enumerate(((h0, u0), (h1, u1))):
                    half = slice(i * GDN_DK, (i + 1) * GDN_DK)
                    o_chunks[hd].append(r[:C, half] + qkuw[u][rows, :GDN_DK])
                    state[hd] = state[hd] * g_last[u][c * C:c * C + 1, :] - r[C:, half] + nm[u][c][:, :GDN_DK]
        for hd in heads:
            o = jnp.concatenate(o_chunks[hd], axis=0)
            o = _rms(o, nw_ref[...]) * _silu(z_ref[rows_of(sc), lanes_of(hd)].astype(F32))
            o_ref[rows_of(sc), lanes_of(hd)] = o.astype(BF16)

    for hd in heads:
        s_ref[hd] = state[hd]


def _gdn(q, k, v, z, gb, nw):
    B, T, _ = q.shape
    rows = GDN_STEP
    tok = lambda w: pl.BlockSpec((None, rows, w), lambda b, i: (b, i, 0))
    return pl.pallas_call(
        _gdn_kernel,
        grid=(B, T // rows),
        in_specs=[tok(GDN_QK), tok(GDN_QK), tok(GDN_QK), tok(GDN_QK), tok(LANES),
                  pl.BlockSpec(nw.shape, lambda b, i: (0, 0))],
        out_specs=tok(GDN_QK),
        out_shape=jax.ShapeDtypeStruct((B, T, GDN_QK), BF16),
        scratch_shapes=[pltpu.VMEM((GDN_HEADS, GDN_DK, GDN_DK), F32)],
        compiler_params=pltpu.CompilerParams(
            dimension_semantics=("arbitrary", "arbitrary"), vmem_limit_bytes=VMEM_LIMIT_BYTES),
        name="gdn",
    )(q, k, v, z, gb, nw)


def _band_tables():
    qi = np.arange(WINDOW, dtype=np.int64)[:, None]
    sj = np.arange(2 * WINDOW, dtype=np.int64)[None, :]
    dist = qi + WINDOW - sj
    in_band = (dist >= 0) & (dist < WINDOW)
    d = np.maximum(dist, 0)
    max_exact = REL_BUCKETS // 2
    ratio = np.log(np.maximum(d, 1).astype(np.float32) / np.float32(max_exact)) / np.float32(
        math.log(REL_MAX_DIST / max_exact))
    large = max_exact + (ratio.astype(np.float32) * np.float32(REL_BUCKETS - max_exact)).astype(np.int32)
    large = np.minimum(large, REL_BUCKETS - 1)
    bucket = np.where(d < max_exact, d, large).astype(np.int32)
    valid = np.stack([in_band & (sj >= WINDOW), in_band]).astype(np.int32)
    return bucket, valid


def _bias_kernel(table_ref, bucket_ref, valid_ref, o_ref):
    bucket = bucket_ref[...]
    for hd in range(SWA_Q_HEADS):
        acc = jnp.zeros(bucket.shape, F32)
        for b in range(REL_BUCKETS):
            acc = jnp.where(bucket == b, table_ref[b, hd], acc)
        for var in range(2):
            o_ref[var, hd] = jnp.where(valid_ref[var] != 0, acc, NEG_INF)


def _rel_bias(table):
    bucket, valid = _band_tables()
    return pl.pallas_call(
        _bias_kernel,
        in_specs=[pl.BlockSpec(memory_space=pltpu.SMEM),
                  pl.BlockSpec(memory_space=pltpu.VMEM), pl.BlockSpec(memory_space=pltpu.VMEM)],
        out_specs=pl.BlockSpec(memory_space=pltpu.VMEM),
        out_shape=jax.ShapeDtypeStruct((2, SWA_Q_HEADS, WINDOW, 2 * WINDOW), F32),
        name="rel_bias",
    )(table.astype(F32), jnp.asarray(bucket), jnp.asarray(valid))


def _swa_kernel(sinks_ref, qs_ref, kv_ref, bias_ref, nw_ref, o_ref):
    W = WINDOW
    T = qs_ref.shape[0]
    group = SWA_Q_HEADS // SWA_KV_HEADS
    scale = SWA_HEAD_DIM ** -0.5
    lo_half = lax.broadcasted_iota(jnp.int32, (W, LANES), 1) < SWA_HEAD_DIM

    def block(n, carry):
        r0 = pl.multiple_of(n * W, W)
        p0 = pl.multiple_of(jnp.maximum(n - 1, 0) * W, W)
        var = jnp.minimum(n, 1)
        q = qs_ref[pl.ds(r0, W), :]
        kv = jnp.concatenate([kv_ref[pl.ds(p0, W), :], kv_ref[pl.ds(r0, W), :]], axis=0).astype(F32)
        k_pair = kv[:, :LANES]
        v_pair = kv[:, LANES:]
        k_opts = (k_pair.astype(BF16), pltpu.roll(k_pair, SWA_HEAD_DIM, axis=1).astype(BF16))
        v_opts = (v_pair.astype(BF16), pltpu.roll(v_pair, SWA_HEAD_DIM, axis=1).astype(BF16))

        pairs = []
        ss = jnp.zeros((W, 1), F32)
        for pr in range(SWA_Q_HEADS // 2):
            halves = []
            for half in range(2):
                hd = 2 * pr + half
                kvh = hd // group
                sel = 0 if half == kvh else 1
                qp = q[:, pr * LANES:(pr + 1) * LANES]
                qm = jnp.where(lo_half if half == 0 else jnp.logical_not(lo_half), qp, jnp.zeros_like(qp))
                s = _dot_nt(qm, k_opts[sel]) * scale + bias_ref[var, hd]
                sink = sinks_ref[hd]
                m = jnp.maximum(jnp.max(s, axis=-1, keepdims=True), sink)
                e = jnp.exp(s - m)
                denom = jnp.sum(e, axis=-1, keepdims=True) + jnp.exp(sink - m)
                halves.append(_dot(e.astype(BF16), v_opts[sel]) / denom)
            o_pair = jnp.where(lo_half, halves[0], halves[1])
            ss = ss + jnp.sum(o_pair * o_pair, axis=-1, keepdims=True)
            pairs.append(o_pair)
        inv = lax.rsqrt(ss * (1.0 / SWA_Q) + NORM_EPS)
        for pr, o_pair in enumerate(pairs):
            lanes = slice(pr * LANES, (pr + 1) * LANES)
            o_ref[pl.ds(r0, W), lanes] = (o_pair * inv * nw_ref[:, lanes]).astype(BF16)
        return carry

    lax.fori_loop(0, T // W, block, 0)


def _swa(sinks, qs, kvs, bias, nw):
    B, T, _ = qs.shape
    seq = lambda w: pl.BlockSpec((None, T, w), lambda b: (b, 0, 0))
    return pl.pallas_call(
        _swa_kernel,
        grid=(B,),
        in_specs=[pl.BlockSpec(memory_space=pltpu.SMEM), seq(SWA_Q), seq(2 * SWA_KV),
                  pl.BlockSpec(bias.shape, lambda b: (0, 0, 0, 0)),
                  pl.BlockSpec(nw.shape, lambda b: (0, 0))],
        out_specs=seq(SWA_Q),
        out_shape=jax.ShapeDtypeStruct((B, T, SWA_Q), BF16),
        compiler_params=pltpu.CompilerParams(
            dimension_semantics=("arbitrary",), vmem_limit_bytes=VMEM_LIMIT_BYTES),
        name="swa",
    )(sinks, qs, kvs, bias, nw)


GELU_K0 = math.sqrt(2.0 / math.pi)
GELU_K1 = 0.044715 * GELU_K0


def _gelu_tanh_x2(x):
    return x * (1.0 + jnp.tanh(x * (GELU_K0 + GELU_K1 * (x * x))))


def _mlp_kernel(x_ref, og_ref, os_ref, wout_ref, pmw_ref, pfw_ref, wgate_ref, wup_ref, convw_ref, convb_ref,
                wdown_ref, postw_ref, o_ref, pad_ref, carry_ref, acc_ref):
    tm = x_ref.shape[0]
    halo = SUBLANES
    n_ff = wgate_ref.shape[0]

    @pl.when(pl.program_id(1) == 0)
    def _():
        carry_ref[...] = jnp.zeros(carry_ref.shape, F32)

    mix = _dot(og_ref[...], wout_ref[0:GDN_QK, :]) + _dot(os_ref[...], wout_ref[GDN_QK:, :])
    x1 = x_ref[...] + _rms(mix, pmw_ref[...])
    h = _rms(x1, pfw_ref[...]).astype(BF16)

    for c in range(n_ff):
        gate = _dot(h, wgate_ref[c])
        up = _dot(h, wup_ref[c])
        pad_ref[0:halo, :] = carry_ref[c]
        pad_ref[halo:halo + tm, :] = gate
        carry_ref[c] = gate[tm - halo:, :]
        cw = convw_ref[c]
        y = gate * cw[FFN_CONV - 1:FFN_CONV] + convb_ref[c]
        for j in range(FFN_CONV - 1):
            r0 = halo - (FFN_CONV - 1) + j
            y = y + pad_ref[r0:r0 + tm, :] * cw[j:j + 1]
        act = (_gelu_tanh_x2(y) * up).astype(BF16)
        contrib = _dot(act, wdown_ref[c])
        if c == 0:
            acc_ref[...] = contrib
        else:
            acc_ref[...] += contrib

    o_ref[...] = x1 + _rms(acc_ref[...], postw_ref[...])


def _mlp(x, og, osw, wout, pmw, pfw, wgate, wup, convw, convb, wdown, postw, tm):
    B, T, D = x.shape
    n_ff = wgate.shape[0]
    tok = lambda w: pl.BlockSpec((None, tm, w), lambda b, i: (b, i, 0))
    full = lambda a: pl.BlockSpec(a.shape, lambda b, i: (0,) * a.ndim, pipeline_mode=pl.Buffered(1))
    return pl.pallas_call(
        _mlp_kernel,
        grid=(B, T // tm),
        in_specs=[tok(D), tok(GDN_QK), tok(SWA_Q), full(wout), full(pmw), full(pfw), full(wgate), full(wup),
                  full(convw), full(convb), full(wdown), full(postw)],
        out_specs=tok(D),
        out_shape=jax.ShapeDtypeStruct((B, T, D), x.dtype),
        scratch_shapes=[pltpu.VMEM((tm + SUBLANES, FF_CHUNK), F32),
                        pltpu.VMEM((n_ff, SUBLANES, FF_CHUNK), F32),
                        pltpu.VMEM((tm, D), F32)],
        compiler_params=pltpu.CompilerParams(
            dimension_semantics=("arbitrary", "arbitrary"), vmem_limit_bytes=VMEM_LIMIT_BYTES),
        name="mlp",
    )(x, og, osw, wout, pmw, pfw, wgate, wup, convw, convb, wdown, postw)


def _token_tile(T):
    for tm in (512, 256, 128):
        if T % tm == 0:
            return tm
    raise ValueError(f"sequence length {T} must be a multiple of 128")


def kernel(x, pre_mix_norm_w, w_in, gdn_conv_w, gdn_a_log, gdn_dt_bias, gdn_norm_w, swa_sinks, rel_bias_table,
           swa_norm_w, w_out, post_mix_norm_w, pre_ffn_norm_w, w_gate, w_up, ffn_conv_w, ffn_conv_b, w_down,
           post_ffn_norm_w):
    B, T, D = x.shape
    depth = w_in.shape[0]
    d_ff = w_gate.shape[-1]
    assert T % GDN_STEP == 0 and T % WINDOW == 0 and d_ff % FF_CHUNK == 0
    tm = _token_tile(T)
    n_ff = d_ff // FF_CHUNK
    gdn_qkv = 3 * GDN_QK
    n_gate = 2 * GDN_HEADS
    ab0 = gdn_qkv + GDN_QK
    row = lambda a: a.reshape(1, -1).astype(F32)

    bias = _rel_bias(rel_bias_table)

    for l in range(depth):
        wl = w_in[l]
        wmain = jnp.concatenate([wl[:, :ab0], wl[:, ab0 + n_gate:]], axis=1).astype(BF16)
        wab = jnp.pad(wl[:, ab0:ab0 + n_gate], ((0, 0), (0, LANES - n_gate))).astype(BF16)
        alog_row = jnp.pad(row(gdn_a_log[l]), ((0, 0), (0, LANES - GDN_HEADS)))
        dtb_row = jnp.pad(row(gdn_dt_bias[l]), ((0, 0), (0, LANES - GDN_HEADS)))

        q, k, v, z, gb, qs, kvs = _inproj(x, row(pre_mix_norm_w[l]), wmain, wab, gdn_conv_w[l].astype(F32),
                                          alog_row, dtb_row, tm)
        o_g = _gdn(q, k, v, z, gb, row(gdn_norm_w[l]))
        o_s = _swa(swa_sinks[l].astype(F32), qs, kvs, bias, row(swa_norm_w[l]))

        chunked = lambda w: jnp.transpose(w.reshape(w.shape[0], n_ff, FF_CHUNK), (1, 0, 2))
        x = _mlp(x, o_g, o_s, w_out[l].astype(BF16), row(post_mix_norm_w[l]), row(pre_ffn_norm_w[l]),
                 chunked(w_gate[l]).astype(BF16), chunked(0.5 * w_up[l]).astype(BF16),
                 chunked(ffn_conv_w[l].astype(F32)), chunked(row(ffn_conv_b[l])),
                 w_down[l].reshape(n_ff, FF_CHUNK, D).astype(BF16), row(post_ffn_norm_w[l]), tm)
    return x
```

```python
import functools
import math

import numpy as np
import jax
import jax.numpy as jnp
from jax import lax
from jax.experimental import pallas as pl
from jax.experimental.pallas import tpu as pltpu

F32 = jnp.float32
BF16 = jnp.bfloat16

GDN_HEADS = 4
GDN_DK = 128
GDN_CONV = 4
GDN_CHUNK = 64
SWA_Q_HEADS = 8
SWA_KV_HEADS = 2
SWA_HEAD_DIM = 64
WINDOW = 128
REL_BUCKETS = 32
REL_MAX_DIST = 128
FFN_CONV = 3
NORM_EPS = 1e-6
NEG_INF = -1e30
LOG2_E = math.log2(math.e)

GDN_QK = GDN_HEADS * GDN_DK
SWA_Q = SWA_Q_HEADS * SWA_HEAD_DIM
SWA_KV = SWA_KV_HEADS * SWA_HEAD_DIM

LANES = 128
SUBLANES = 8
VMEM_LIMIT_BYTES = 56 * 1024 * 1024

GDN_SUPER = 4 * GDN_CHUNK
GDN_STEP = 2 * GDN_SUPER
FF_CHUNK = 256
SWA_UNROLL = 4


def _rms(x, w):
    return x * lax.rsqrt(jnp.mean(x * x, axis=-1, keepdims=True) + NORM_EPS) * w


def _sigmoid(x):
    return 1.0 / (1.0 + jnp.exp(-x))


def _silu(x):
    return x * _sigmoid(x)


def _dot(a, b):
    return jnp.dot(a, b, preferred_element_type=F32)


def _dot_nt(a, b):
    return lax.dot_general(a, b, (((1,), (1,)), ((), ())), preferred_element_type=F32)


def _inproj_kernel(x_ref, nw_ref, wmain_ref, wab_ref, convw_ref, alog_ref, dtb_ref,
                   q_ref, k_ref, v_ref, z_ref, gb_ref, qs_ref, kvs_ref, pad_ref):
    tm = x_ref.shape[0]
    halo = SUBLANES

    @pl.when(pl.program_id(1) == 0)
    def _():
        pad_ref[:, 0:halo, :] = jnp.zeros((3, halo, GDN_QK), F32)

    h = _rms(x_ref[...], nw_ref[...]).astype(BF16)

    ab = _dot(h, wab_ref[...])
    lane = lax.broadcasted_iota(jnp.int32, ab.shape, 1)
    pre = ab + dtb_ref[...]
    softplus = jnp.maximum(pre, 0.0) + jnp.log(1.0 + jnp.exp(-jnp.abs(pre)))
    g = -jnp.exp(alog_ref[...]) * softplus
    gb_ref[...] = jnp.where(lane < GDN_HEADS, g, _sigmoid(ab))

    outs = (q_ref, k_ref, v_ref)
    for grp in range(3):
        c0 = grp * GDN_QK
        raw = _dot(h, wmain_ref[:, c0:c0 + GDN_QK])
        pad_ref[grp, halo:halo + tm, :] = raw
        cw = convw_ref[:, c0:c0 + GDN_QK]
        y = raw * cw[GDN_CONV - 1:GDN_CONV]
        for j in range(GDN_CONV - 1):
            r0 = halo - (GDN_CONV - 1) + j
            y = y + pad_ref[grp, r0:r0 + tm, :] * cw[j:j + 1]
        pad_ref[grp, 0:halo, :] = pad_ref[grp, tm:tm + halo, :]
        y = _silu(y)
        if grp < 2:
            scale = GDN_DK ** -0.5 if grp == 0 else 1.0
            for hd in range(GDN_HEADS):
                yh = y[:, hd * GDN_DK:(hd + 1) * GDN_DK]
                inv = lax.rsqrt(jnp.sum(yh * yh, axis=-1, keepdims=True) + NORM_EPS) * scale
                outs[grp][:, hd * GDN_DK:(hd + 1) * GDN_DK] = (yh * inv).astype(BF16)
        else:
            v_ref[...] = y.astype(BF16)

    c0 = 3 * GDN_QK
    z_ref[...] = _dot(h, wmain_ref[:, c0:c0 + GDN_QK]).astype(BF16)
    c0 += GDN_QK
    qs_ref[...] = _dot(h, wmain_ref[:, c0:c0 + SWA_Q]).astype(BF16)
    c0 += SWA_Q
    kvs_ref[...] = _dot(h, wmain_ref[:, c0:c0 + 2 * SWA_KV]).astype(BF16)


def _inproj(x, nw, wmain, wab, convw, alog_row, dtb_row, tm):
    B, T, D = x.shape
    tok = lambda w: pl.BlockSpec((None, tm, w), lambda b, i: (b, i, 0))
    full = lambda a: pl.BlockSpec(a.shape, lambda b, i: (0,) * a.ndim)
    sds = lambda w, dt: jax.ShapeDtypeStruct((B, T, w), dt)
    return pl.pallas_call(
        _inproj_kernel,
        grid=(B, T // tm),
        in_specs=[tok(D), full(nw), full(wmain), full(wab), full(convw), full(alog_row), full(dtb_row)],
        out_specs=[tok(GDN_QK), tok(GDN_QK), tok(GDN_QK), tok(GDN_QK), tok(LANES), tok(SWA_Q), tok(2 * SWA_KV)],
        out_shape=[sds(GDN_QK, BF16), sds(GDN_QK, BF16), sds(GDN_QK, BF16), sds(GDN_QK, BF16),
                   sds(LANES, F32), sds(SWA_Q, BF16), sds(2 * SWA_KV, BF16)],
        scratch_shapes=[pltpu.VMEM((3, tm + SUBLANES, GDN_QK), F32)],
        compiler_params=pltpu.CompilerParams(
            dimension_semantics=("arbitrary", "arbitrary"), vmem_limit_bytes=VMEM_LIMIT_BYTES),
        name="inproj",
    )(x, nw, wmain, wab, convw, alog_row, dtb_row)


def _split3(a):
    hi = a.astype(BF16)
    r1 = a - hi.astype(F32)
    mid = r1.astype(BF16)
    lo = (r1 - mid.astype(F32)).astype(BF16)
    return hi, mid, lo


def _lane_col(a, idx):
    lane = lax.broadcasted_iota(jnp.int32, a.shape, 1)
    return jnp.sum(jnp.where(lane == idx, a, 0.0), axis=-1, keepdims=True)


def _gdn_kernel(q_ref, k_ref, v_ref, z_ref, gb_ref, nw_ref, o_ref, s_ref):
    R = GDN_SUPER
    C = GDN_CHUNK
    n_chunks = R // C
    n_super = q_ref.shape[0] // R
    shift = int(math.log2(C))
    heads = range(GDN_HEADS)
    units = [(sc, hd) for sc in range(n_super) for hd in heads]

    @pl.when(pl.program_id(1) == 0)
    def _():
        s_ref[...] = jnp.zeros(s_ref.shape, F32)

    row = lax.broadcasted_iota(jnp.int32, (R, R), 0)
    col = lax.broadcasted_iota(jnp.int32, (R, R), 1)
    same = (row >> shift) == (col >> shift)
    causal = jnp.logical_and(same, row >= col)
    strict = jnp.logical_and(same, row > col)
    eye_packed = jnp.where((lax.broadcasted_iota(jnp.int32, (C, R), 1) & (C - 1))
                           == lax.broadcasted_iota(jnp.int32, (C, R), 0), 1.0, 0.0)
    sums_to = jnp.concatenate([jnp.where(jnp.logical_and(same, row <= col), 1.0, 0.0),
                               jnp.where(same, 1.0, 0.0)], axis=1).astype(BF16)

    def pack(full):
        return sum(full[j * C:(j + 1) * C] for j in range(1, n_chunks)) + full[0:C]

    def block_diag(packed):
        return jnp.where(same, jnp.concatenate([packed] * n_chunks, axis=0), jnp.zeros((R, R), BF16))

    gbs, gcum_rows, cols = [], [], []
    for sc in range(n_super):
        gb = gb_ref[sc * R:(sc + 1) * R, :]
        hi, mid, lo = (p.astype(F32)[0:SUBLANES] for p in _split3(gb.T))
        parts = jnp.concatenate([hi, mid, lo, jnp.zeros_like(hi)], axis=0).astype(BF16)
        sums = _dot(parts, sums_to)
        sums = sums[0:SUBLANES] + sums[SUBLANES:2 * SUBLANES] + sums[2 * SUBLANES:3 * SUBLANES]
        gbs.append(gb)
        gcum_rows.append(sums[:, :R])
        cols.append(jnp.concatenate([sums[:, :R], sums[:, R:], jnp.zeros((LANES - 2 * SUBLANES, R), F32)], axis=0).T)

    rows_of = lambda sc: slice(sc * R, (sc + 1) * R)
    lanes_of = lambda hd: slice(hd * GDN_DK, (hd + 1) * GDN_DK)
    wide = lambda c: jnp.broadcast_to(c, (R, GDN_DK))
    gc = [_lane_col(cols[sc], hd) for sc, hd in units]
    gl = [_lane_col(cols[sc], SUBLANES + hd) for sc, hd in units]
    beta = [_lane_col(gbs[sc], GDN_HEADS + hd) for sc, hd in units]
    n_units = range(len(units))
    decay = [jnp.where(causal, jnp.exp(jnp.where(causal, gc[u] - gcum_rows[sc][hd:hd + 1, :], 0.0)), 0.0)
             for u, (sc, hd) in enumerate(units)]
    qh = [q_ref[rows_of(sc), lanes_of(hd)] for sc, hd in units]
    kh = [k_ref[rows_of(sc), lanes_of(hd)] for sc, hd in units]
    kf = [kh[u].astype(F32) for u in n_units]
    kb = [kf[u] * beta[u] for u in n_units]
    eg = [jnp.exp(wide(gc[u])) for u in n_units]
    rhs = [jnp.concatenate([v_ref[rows_of(sc), lanes_of(hd)].astype(F32) * beta[u], kb[u] * eg[u]],
                           axis=1).astype(BF16) for u, (sc, hd) in enumerate(units)]

    kq = [_dot_nt(jnp.concatenate([kb[u].astype(BF16), qh[u]], axis=0), kh[u]) for u in n_units]
    a = [jnp.where(strict, kq[u][:R] * decay[u], 0.0) for u in n_units]
    qk = [jnp.where(causal, kq[u][R:] * decay[u], 0.0).astype(BF16) for u in n_units]

    a_packed = [pack(a[u]) for u in n_units]
    p = [eye_packed - a_packed[u] for u in n_units]
    xpow = [_dot(a_packed[u].astype(BF16), a[u].astype(BF16)) for u in n_units]
    for _ in range(shift - 2):
        xb = [xpow[u].astype(BF16) for u in n_units]
        y = [_dot(jnp.concatenate([p[u].astype(BF16), xb[u]], axis=0), block_diag(xb[u])) for u in n_units]
        p = [p[u] + y[u][:C] for u in n_units]
        xpow = [y[u][C:] for u in n_units]
    p = [p[u] + _dot(p[u].astype(BF16), block_diag(xpow[u].astype(BF16))) for u in n_units]

    uw = [_dot(block_diag(p[u].astype(BF16)), rhs[u]).astype(BF16) for u in n_units]
    qkuw = [_dot(qk[u], uw[u]) for u in n_units]
    q_eff = [(qh[u].astype(F32) * eg[u] - qkuw[u][:, GDN_DK:]).astype(BF16) for u in n_units]
    kd_t = [(kf[u] * jnp.exp(wide(gl[u] - gc[u]))).T.astype(BF16) for u in n_units]
    g_last = [jnp.exp(wide(gl[u])) for u in n_units]
    row_chunk = lax.broadcasted_iota(jnp.int32, (R, 2 * GDN_DK), 0) >> shift
    zero_uw = jnp.zeros((R, 2 * GDN_DK), BF16)
    nm = [[_dot(kd_t[u], jnp.where(row_chunk == c, uw[u], zero_uw)) for c in range(n_chunks)] for u in n_units]

    state = [s_ref[hd] for hd in heads]
    zero_s = jnp.zeros((GDN_DK, GDN_DK), BF16)
    for sc in range(n_super):
        o_chunks = [[] for _ in heads]
        for c in range(n_chunks):
            rows = slice(c * C, (c + 1) * C)
            for h0 in range(0, GDN_HEADS, 2):
                h1 = h0 + 1
                u0, u1 = sc * GDN_HEADS + h0, sc * GDN_HEADS + h1
                lhs = jnp.concatenate(
                    [jnp.concatenate([q_eff[u0][rows], q_eff[u1][rows]], axis=1),
                     jnp.concatenate([nm[u0][c][:, GDN_DK:], nm[u1][c][:, GDN_DK:]], axis=1).astype(BF16)], axis=0)
                s_pair = jnp.concatenate(
                    [jnp.concatenate([state[h0].astype(BF16), zero_s], axis=1),
                     jnp.concatenate([zero_s, state[h1].astype(BF16)], axis=1)], axis=0)
                r = _dot(lhs, s_pair)
                for i, (hd, u) in enumerate(((h0, u0), (h1, u1))):
                    half = slice(i * GDN_DK, (i + 1) * GDN_DK)
                    o_chunks[hd].append(r[:C, half] + qkuw[u][rows, :GDN_DK])
                    state[hd] = state[hd] * g_last[u][c * C:c * C + 1, :] - r[C:, half] + nm[u][c][:, :GDN_DK]
        for hd in heads:
            o = jnp.concatenate(o_chunks[hd], axis=0)
            o = _rms(o, nw_ref[...]) * _silu(z_ref[rows_of(sc), lanes_of(hd)].astype(F32))
            o_ref[rows_of(sc), lanes_of(hd)] = o.astype(BF16)

    for hd in heads:
        s_ref[hd] = state[hd]


def _gdn(q, k, v, z, gb, nw):
    B, T, _ = q.shape
    rows = GDN_STEP
    tok = lambda w: pl.BlockSpec((None, rows, w), lambda b, i: (b, i, 0))
    return pl.pallas_call(
        _gdn_kernel,
        grid=(B, T // rows),
        in_specs=[tok(GDN_QK), tok(GDN_QK), tok(GDN_QK), tok(GDN_QK), tok(LANES),
                  pl.BlockSpec(nw.shape, lambda b, i: (0, 0))],
        out_specs=tok(GDN_QK),
        out_shape=jax.ShapeDtypeStruct((B, T, GDN_QK), BF16),
        scratch_shapes=[pltpu.VMEM((GDN_HEADS, GDN_DK, GDN_DK), F32)],
        compiler_params=pltpu.CompilerParams(
            dimension_semantics=("arbitrary", "arbitrary"), vmem_limit_bytes=VMEM_LIMIT_BYTES),
        name="gdn",
    )(q, k, v, z, gb, nw)


def _band_tables():
    qi = np.arange(WINDOW, dtype=np.int64)[:, None]
    sj = np.arange(2 * WINDOW, dtype=np.int64)[None, :]
    dist = qi + WINDOW - sj
    in_band = (dist >= 0) & (dist < WINDOW)
    d = np.maximum(dist, 0)
    max_exact = REL_BUCKETS // 2
    ratio = np.log(np.maximum(d, 1).astype(np.float32) / np.float32(max_exact)) / np.float32(
        math.log(REL_MAX_DIST / max_exact))
    large = max_exact + (ratio.astype(np.float32) * np.float32(REL_BUCKETS - max_exact)).astype(np.int32)
    large = np.minimum(large, REL_BUCKETS - 1)
    bucket = np.where(d < max_exact, d, large).astype(np.int32)
    valid = np.stack([in_band & (sj >= WINDOW), in_band]).astype(np.int32)
    return bucket, valid


def _bias_kernel(table_ref, bucket_ref, valid_ref, o_ref):
    bucket = bucket_ref[...]
    for hd in range(SWA_Q_HEADS):
        acc = jnp.zeros(bucket.shape, F32)
        for b in range(REL_BUCKETS):
            acc = jnp.where(bucket == b, table_ref[b, hd] * LOG2_E, acc)
        for var in range(2):
            o_ref[var, hd] = jnp.where(valid_ref[var] != 0, acc, NEG_INF)


def _rel_bias(table):
    bucket, valid = _band_tables()
    return pl.pallas_call(
        _bias_kernel,
        in_specs=[pl.BlockSpec(memory_space=pltpu.SMEM),
                  pl.BlockSpec(memory_space=pltpu.VMEM), pl.BlockSpec(memory_space=pltpu.VMEM)],
        out_specs=pl.BlockSpec(memory_space=pltpu.VMEM),
        out_shape=jax.ShapeDtypeStruct((2, SWA_Q_HEADS, WINDOW, 2 * WINDOW), F32),
        name="rel_bias",
    )(table.astype(F32), jnp.asarray(bucket), jnp.asarray(valid))


def _swa_kernel(sinks_ref, qs_ref, kv_ref, bias_ref, nw_ref, o_ref):
    W = WINDOW
    T = qs_ref.shape[0]
    group = SWA_Q_HEADS // SWA_KV_HEADS
    lo_half = lax.broadcasted_iota(jnp.int32, (W, LANES), 1) < SWA_HEAD_DIM

    def block(n, carry):
        r0 = pl.multiple_of(n * W, W)
        p0 = pl.multiple_of(jnp.maximum(n - 1, 0) * W, W)
        var = jnp.minimum(n, 1)
        q = qs_ref[pl.ds(r0, W), :]
        kv = jnp.concatenate([kv_ref[pl.ds(p0, W), :], kv_ref[pl.ds(r0, W), :]], axis=0).astype(F32)
        k_pair = kv[:, :LANES]
        v_pair = kv[:, LANES:]
        k_opts = (k_pair.astype(BF16), pltpu.roll(k_pair, SWA_HEAD_DIM, axis=1).astype(BF16))
        v_opts = (v_pair.astype(BF16), pltpu.roll(v_pair, SWA_HEAD_DIM, axis=1).astype(BF16))

        pairs = []
        ss = jnp.zeros((W, 1), F32)
        for pr in range(SWA_Q_HEADS // 2):
            halves = []
            for half in range(2):
                hd = 2 * pr + half
                kvh = hd // group
                sel = 0 if half == kvh else 1
                qp = q[:, pr * LANES:(pr + 1) * LANES]
                qm = jnp.where(lo_half if half == 0 else jnp.logical_not(lo_half), qp, jnp.zeros_like(qp))
                s = _dot_nt(qm, k_opts[sel]) + bias_ref[var, hd]
                sink = sinks_ref[hd] * LOG2_E
                m = jnp.maximum(jnp.max(s, axis=-1, keepdims=True), sink)
                e = jnp.exp2(s - m)
                denom = jnp.sum(e, axis=-1, keepdims=True) + jnp.exp2(sink - m)
                halves.append(_dot(e.astype(BF16), v_opts[sel]) / denom)
            o_pair = jnp.where(lo_half, halves[0], halves[1])
            ss = ss + jnp.sum(o_pair * o_pair, axis=-1, keepdims=True)
            pairs.append(o_pair)
        inv = lax.rsqrt(ss * (1.0 / SWA_Q) + NORM_EPS)
        for pr, o_pair in enumerate(pairs):
            lanes = slice(pr * LANES, (pr + 1) * LANES)
            o_ref[pl.ds(r0, W), lanes] = (o_pair * inv * nw_ref[:, lanes]).astype(BF16)
        return carry

    lax.fori_loop(0, T // W, block, 0, unroll=SWA_UNROLL)


def _swa(sinks, qs, kvs, bias, nw):
    B, T, _ = qs.shape
    seq = lambda w: pl.BlockSpec((None, T, w), lambda b: (b, 0, 0))
    return pl.pallas_call(
        _swa_kernel,
        grid=(B,),
        in_specs=[pl.BlockSpec(memory_space=pltpu.SMEM), seq(SWA_Q), seq(2 * SWA_KV),
                  pl.BlockSpec(bias.shape, lambda b: (0, 0, 0, 0)),
                  pl.BlockSpec(nw.shape, lambda b: (0, 0))],
        out_specs=seq(SWA_Q),
        out_shape=jax.ShapeDtypeStruct((B, T, SWA_Q), BF16),
        compiler_params=pltpu.CompilerParams(
            dimension_semantics=("arbitrary",), vmem_limit_bytes=VMEM_LIMIT_BYTES),
        name="swa",
    )(sinks, qs, kvs, bias, nw)


GELU_K0 = math.sqrt(2.0 / math.pi)
GELU_K1 = 0.044715 * GELU_K0


def _gelu_tanh_x2(x):
    return x * (1.0 + jnp.tanh(x * (GELU_K0 + GELU_K1 * (x * x))))


def _mlp_kernel(x_ref, og_ref, os_ref, wout_ref, pmw_ref, pfw_ref, wgate_ref, wup_ref, convw_ref, convb_ref,
                wdown_ref, postw_ref, o_ref, pad_ref, carry_ref, acc_ref):
    tm = x_ref.shape[0]
    halo = SUBLANES
    n_ff = wgate_ref.shape[0]

    @pl.when(pl.program_id(1) == 0)
    def _():
        carry_ref[...] = jnp.zeros(carry_ref.shape, F32)

    mix = _dot(og_ref[...], wout_ref[0:GDN_QK, :]) + _dot(os_ref[...], wout_ref[GDN_QK:, :])
    x1 = x_ref[...] + _rms(mix, pmw_ref[...])
    h = _rms(x1, pfw_ref[...]).astype(BF16)

    for c in range(n_ff):
        gate = _dot(h, wgate_ref[c])
        up = _dot(h, wup_ref[c])
        pad_ref[0:halo, :] = carry_ref[c]
        pad_ref[halo:halo + tm, :] = gate
        carry_ref[c] = gate[tm - halo:, :]
        cw = convw_ref[c]
        y = gate * cw[FFN_CONV - 1:FFN_CONV] + convb_ref[c]
        for j in range(FFN_CONV - 1):
            r0 = halo - (FFN_CONV - 1) + j
            y = y + pad_ref[r0:r0 + tm, :] * cw[j:j + 1]
        act = (_gelu_tanh_x2(y) * up).astype(BF16)
        contrib = _dot(act, wdown_ref[c])
        if c == 0:
            acc_ref[...] = contrib
        else:
            acc_ref[...] += contrib

    o_ref[...] = x1 + _rms(acc_ref[...], postw_ref[...])


def _mlp(x, og, osw, wout, pmw, pfw, wgate, wup, convw, convb, wdown, postw, tm):
    B, T, D = x.shape
    n_ff = wgate.shape[0]
    tok = lambda w: pl.BlockSpec((None, tm, w), lambda b, i: (b, i, 0))
    full = lambda a: pl.BlockSpec(a.shape, lambda b, i: (0,) * a.ndim, pipeline_mode=pl.Buffered(1))
    return pl.pallas_call(
        _mlp_kernel,
        grid=(B, T // tm),
        in_specs=[tok(D), tok(GDN_QK), tok(SWA_Q), full(wout), full(pmw), full(pfw), full(wgate), full(wup),
                  full(convw), full(convb), full(wdown), full(postw)],
        out_specs=tok(D),
        out_shape=jax.ShapeDtypeStruct((B, T, D), x.dtype),
        scratch_shapes=[pltpu.VMEM((tm + SUBLANES, FF_CHUNK), F32),
                        pltpu.VMEM((n_ff, SUBLANES, FF_CHUNK), F32),
                        pltpu.VMEM((tm, D), F32)],
        compiler_params=pltpu.CompilerParams(
            dimension_semantics=("arbitrary", "arbitrary"), vmem_limit_bytes=VMEM_LIMIT_BYTES),
        name="mlp",
    )(x, og, osw, wout, pmw, pfw, wgate, wup, convw, convb, wdown, postw)


def _token_tile(T):
    for tm in (512, 256, 128):
        if T % tm == 0:
            return tm
    raise ValueError(f"sequence length {T} must be a multiple of 128")


def kernel(x, pre_mix_norm_w, w_in, gdn_conv_w, gdn_a_log, gdn_dt_bias, gdn_norm_w, swa_sinks, rel_bias_table,
           swa_norm_w, w_out, post_mix_norm_w, pre_ffn_norm_w, w_gate, w_up, ffn_conv_w, ffn_conv_b, w_down,
           post_ffn_norm_w):
    B, T, D = x.shape
    depth = w_in.shape[0]
    d_ff = w_gate.shape[-1]
    assert T % GDN_STEP == 0 and T % WINDOW == 0 and d_ff % FF_CHUNK == 0
    tm = _token_tile(T)
    n_ff = d_ff // FF_CHUNK
    gdn_qkv = 3 * GDN_QK
    n_gate = 2 * GDN_HEADS
    ab0 = gdn_qkv + GDN_QK
    row = lambda a: a.reshape(1, -1).astype(F32)

    bias = _rel_bias(rel_bias_table)

    for l in range(depth):
        wl = w_in[l]
        qs0 = ab0 + n_gate
        wmain = jnp.concatenate([wl[:, :ab0], wl[:, qs0:qs0 + SWA_Q] * (SWA_HEAD_DIM ** -0.5 * LOG2_E),
                                 wl[:, qs0 + SWA_Q:]], axis=1).astype(BF16)
        wab = jnp.pad(wl[:, ab0:ab0 + n_gate], ((0, 0), (0, LANES - n_gate))).astype(BF16)
        alog_row = jnp.pad(row(gdn_a_log[l]), ((0, 0), (0, LANES - GDN_HEADS)))
        dtb_row = jnp.pad(row(gdn_dt_bias[l]), ((0, 0), (0, LANES - GDN_HEADS)))

        q, k, v, z, gb, qs, kvs = _inproj(x, row(pre_mix_norm_w[l]), wmain, wab, gdn_conv_w[l].astype(F32),
                                          alog_row, dtb_row, tm)
        o_g = _gdn(q, k, v, z, gb, row(gdn_norm_w[l]))
        o_s = _swa(swa_sinks[l].astype(F32), qs, kvs, bias, row(swa_norm_w[l]))

        chunked = lambda w: jnp.transpose(w.reshape(w.shape[0], n_ff, FF_CHUNK), (1, 0, 2))
        x = _mlp(x, o_g, o_s, w_out[l].astype(BF16), row(post_mix_norm_w[l]), row(pre_ffn_norm_w[l]),
                 chunked(w_gate[l]).astype(BF16), chunked(0.5 * w_up[l]).astype(BF16),
                 chunked(ffn_conv_w[l].astype(F32)), chunked(row(ffn_conv_b[l])),
                 w_down[l].reshape(n_ff, FF_CHUNK, D).astype(BF16), row(post_ffn_norm_w[l]), tm)
    return x
```

```python
import functools
import math

import numpy as np
import jax
import jax.numpy as jnp
from jax import lax
from jax.experimental import pallas as pl
from jax.experimental.pallas import tpu as pltpu

F32 = jnp.float32
BF16 = jnp.bfloat16

GDN_HEADS = 4
GDN_DK = 128
GDN_CONV = 4
GDN_CHUNK = 64
SWA_Q_HEADS = 8
SWA_KV_HEADS = 2
SWA_HEAD_DIM = 64
WINDOW = 128
REL_BUCKETS = 32
REL_MAX_DIST = 128
FFN_CONV = 3
NORM_EPS = 1e-6
NEG_INF = -1e30
LOG2_E = math.log2(math.e)

GDN_QK = GDN_HEADS * GDN_DK
SWA_Q = SWA_Q_HEADS * SWA_HEAD_DIM
SWA_KV = SWA_KV_HEADS * SWA_HEAD_DIM

LANES = 128
SUBLANES = 8
VMEM_LIMIT_BYTES = 56 * 1024 * 1024

GDN_SUPER = 4 * GDN_CHUNK
GDN_STEP = 2 * GDN_SUPER
FF_CHUNK = 256
SWA_UNROLL = 4


def _rms(x, w):
    return x * lax.rsqrt(jnp.mean(x * x, axis=-1, keepdims=True) + NORM_EPS) * w


def _sigmoid(x):
    return 1.0 / (1.0 + jnp.exp(-x))


def _silu(x):
    return x * _sigmoid(x)


def _dot(a, b):
    return jnp.dot(a, b, preferred_element_type=F32)


def _dot_nt(a, b):
    return lax.dot_general(a, b, (((1,), (1,)), ((), ())), preferred_element_type=F32)


def _inproj_kernel(x_ref, nw_ref, wmain_ref, wab_ref, convw_ref, alog_ref, dtb_ref,
                   q_ref, k_ref, v_ref, z_ref, gb_ref, qs_ref, kvs_ref, pad_ref):
    tm = x_ref.shape[0]
    halo = SUBLANES

    @pl.when(pl.program_id(1) == 0)
    def _():
        pad_ref[:, 0:halo, :] = jnp.zeros((3, halo, GDN_QK), F32)

    h = _rms(x_ref[...], nw_ref[...]).astype(BF16)

    ab = _dot(h, wab_ref[...])
    lane = lax.broadcasted_iota(jnp.int32, ab.shape, 1)
    pre = ab + dtb_ref[...]
    softplus = jnp.maximum(pre, 0.0) + jnp.log(1.0 + jnp.exp(-jnp.abs(pre)))
    g = -jnp.exp(alog_ref[...]) * softplus
    gb_ref[...] = jnp.where(lane < GDN_HEADS, g, _sigmoid(ab))

    outs = (q_ref, k_ref, v_ref)
    for grp in range(3):
        c0 = grp * GDN_QK
        raw = _dot(h, wmain_ref[:, c0:c0 + GDN_QK])
        pad_ref[grp, halo:halo + tm, :] = raw
        cw = convw_ref[:, c0:c0 + GDN_QK]
        y = raw * cw[GDN_CONV - 1:GDN_CONV]
        for j in range(GDN_CONV - 1):
            r0 = halo - (GDN_CONV - 1) + j
            y = y + pad_ref[grp, r0:r0 + tm, :] * cw[j:j + 1]
        pad_ref[grp, 0:halo, :] = pad_ref[grp, tm:tm + halo, :]
        y = _silu(y)
        if grp < 2:
            scale = GDN_DK ** -0.5 if grp == 0 else 1.0
            for hd in range(GDN_HEADS):
                yh = y[:, hd * GDN_DK:(hd + 1) * GDN_DK]
                inv = lax.rsqrt(jnp.sum(yh * yh, axis=-1, keepdims=True) + NORM_EPS) * scale
                outs[grp][:, hd * GDN_DK:(hd + 1) * GDN_DK] = (yh * inv).astype(BF16)
        else:
            v_ref[...] = y.astype(BF16)

    c0 = 3 * GDN_QK
    z_ref[...] = _dot(h, wmain_ref[:, c0:c0 + GDN_QK]).astype(BF16)
    c0 += GDN_QK
    qs_ref[...] = _dot(h, wmain_ref[:, c0:c0 + SWA_Q]).astype(BF16)
    c0 += SWA_Q
    kvs_ref[...] = _dot(h, wmain_ref[:, c0:c0 + 2 * SWA_KV]).astype(BF16)


def _inproj(x, nw, wmain, wab, convw, alog_row, dtb_row, tm):
    B, T, D = x.shape
    tok = lambda w: pl.BlockSpec((None, tm, w), lambda b, i: (b, i, 0))
    full = lambda a: pl.BlockSpec(a.shape, lambda b, i: (0,) * a.ndim)
    sds = lambda w, dt: jax.ShapeDtypeStruct((B, T, w), dt)
    return pl.pallas_call(
        _inproj_kernel,
        grid=(B, T // tm),
        in_specs=[tok(D), full(nw), full(wmain), full(wab), full(convw), full(alog_row), full(dtb_row)],
        out_specs=[tok(GDN_QK), tok(GDN_QK), tok(GDN_QK), tok(GDN_QK), tok(LANES), tok(SWA_Q), tok(2 * SWA_KV)],
        out_shape=[sds(GDN_QK, BF16), sds(GDN_QK, BF16), sds(GDN_QK, BF16), sds(GDN_QK, BF16),
                   sds(LANES, F32), sds(SWA_Q, BF16), sds(2 * SWA_KV, BF16)],
        scratch_shapes=[pltpu.VMEM((3, tm + SUBLANES, GDN_QK), F32)],
        compiler_params=pltpu.CompilerParams(
            dimension_semantics=("arbitrary", "arbitrary"), vmem_limit_bytes=VMEM_LIMIT_BYTES),
        name="inproj",
    )(x, nw, wmain, wab, convw, alog_row, dtb_row)


def _split3(a):
    hi = a.astype(BF16)
    r1 = a - hi.astype(F32)
    mid = r1.astype(BF16)
    lo = (r1 - mid.astype(F32)).astype(BF16)
    return hi, mid, lo


def _lane_col(a, idx):
    lane = lax.broadcasted_iota(jnp.int32, a.shape, 1)
    return jnp.sum(jnp.where(lane == idx, a, 0.0), axis=-1, keepdims=True)


def _gdn_kernel(q_ref, k_ref, v_ref, z_ref, gb_ref, nw_ref, o_ref, s_ref):
    R = GDN_SUPER
    C = GDN_CHUNK
    n_chunks = R // C
    n_super = q_ref.shape[0] // R
    shift = int(math.log2(C))
    heads = range(GDN_HEADS)
    units = [(sc, hd) for sc in range(n_super) for hd in heads]

    @pl.when(pl.program_id(1) == 0)
    def _():
        s_ref[...] = jnp.zeros(s_ref.shape, F32)

    row = lax.broadcasted_iota(jnp.int32, (R, R), 0)
    col = lax.broadcasted_iota(jnp.int32, (R, R), 1)
    same = (row >> shift) == (col >> shift)
    causal = jnp.logical_and(same, row >= col)
    strict = jnp.logical_and(same, row > col)
    eye_packed = jnp.where((lax.broadcasted_iota(jnp.int32, (C, R), 1) & (C - 1))
                           == lax.broadcasted_iota(jnp.int32, (C, R), 0), 1.0, 0.0)
    sums_to = jnp.concatenate([jnp.where(jnp.logical_and(same, row <= col), 1.0, 0.0),
                               jnp.where(same, 1.0, 0.0)], axis=1).astype(BF16)

    def pack(full):
        return sum(full[j * C:(j + 1) * C] for j in range(1, n_chunks)) + full[0:C]

    def block_diag(packed):
        return jnp.where(same, jnp.concatenate([packed] * n_chunks, axis=0), jnp.zeros((R, R), BF16))

    gbs, gcum_rows, cols = [], [], []
    for sc in range(n_super):
        gb = gb_ref[sc * R:(sc + 1) * R, :]
        hi, mid, lo = (p.astype(F32)[0:SUBLANES] for p in _split3(gb.T))
        parts = jnp.concatenate([hi, mid, lo, jnp.zeros_like(hi)], axis=0).astype(BF16)
        sums = _dot(parts, sums_to)
        sums = sums[0:SUBLANES] + sums[SUBLANES:2 * SUBLANES] + sums[2 * SUBLANES:3 * SUBLANES]
        gbs.append(gb)
        gcum_rows.append(sums[:, :R])
        cols.append(jnp.concatenate([sums[:, :R], sums[:, R:], jnp.zeros((LANES - 2 * SUBLANES, R), F32)], axis=0).T)

    rows_of = lambda sc: slice(sc * R, (sc + 1) * R)
    lanes_of = lambda hd: slice(hd * GDN_DK, (hd + 1) * GDN_DK)
    wide = lambda c: jnp.broadcast_to(c, (R, GDN_DK))
    gc = [_lane_col(cols[sc], hd) for sc, hd in units]
    gl = [_lane_col(cols[sc], SUBLANES + hd) for sc, hd in units]
    beta = [_lane_col(gbs[sc], GDN_HEADS + hd) for sc, hd in units]
    n_units = range(len(units))
    decay = [jnp.where(causal, jnp.exp(jnp.where(causal, gc[u] - gcum_rows[sc][hd:hd + 1, :], 0.0)), 0.0)
             for u, (sc, hd) in enumerate(units)]
    qh = [q_ref[rows_of(sc), lanes_of(hd)] for sc, hd in units]
    kh = [k_ref[rows_of(sc), lanes_of(hd)] for sc, hd in units]
    kf = [kh[u].astype(F32) for u in n_units]
    kb = [kf[u] * beta[u] for u in n_units]
    eg = [jnp.exp(wide(gc[u])) for u in n_units]
    rhs = [jnp.concatenate([v_ref[rows_of(sc), lanes_of(hd)].astype(F32) * beta[u], kb[u] * eg[u]],
                           axis=1).astype(BF16) for u, (sc, hd) in enumerate(units)]

    kq = [_dot_nt(jnp.concatenate([kb[u].astype(BF16), qh[u]], axis=0), kh[u]) for u in n_units]
    a = [jnp.where(strict, kq[u][:R] * decay[u], 0.0) for u in n_units]
    qk = [jnp.where(causal, kq[u][R:] * decay[u], 0.0).astype(BF16) for u in n_units]

    a_packed = [pack(a[u]) for u in n_units]
    p = [eye_packed - a_packed[u] for u in n_units]
    xpow = [_dot(a_packed[u].astype(BF16), a[u].astype(BF16)) for u in n_units]
    for _ in range(shift - 2):
        xb = [xpow[u].astype(BF16) for u in n_units]
        y = [_dot(jnp.concatenate([p[u].astype(BF16), xb[u]], axis=0), block_diag(xb[u])) for u in n_units]
        p = [p[u] + y[u][:C] for u in n_units]
        xpow = [y[u][C:] for u in n_units]
    p = [p[u] + _dot(p[u].astype(BF16), block_diag(xpow[u].astype(BF16))) for u in n_units]

    uw = [_dot(block_diag(p[u].astype(BF16)), rhs[u]).astype(BF16) for u in n_units]
    qkuw = [_dot(qk[u], uw[u]) for u in n_units]
    q_eff = [(qh[u].astype(F32) * eg[u] - qkuw[u][:, GDN_DK:]).astype(BF16) for u in n_units]
    kd_t = [(kf[u] * jnp.exp(wide(gl[u] - gc[u]))).T.astype(BF16) for u in n_units]
    g_last = [jnp.exp(wide(gl[u])) for u in n_units]
    row_chunk = lax.broadcasted_iota(jnp.int32, (R, 2 * GDN_DK), 0) >> shift
    zero_uw = jnp.zeros((R, 2 * GDN_DK), BF16)
    nm = [[_dot(kd_t[u], jnp.where(row_chunk == c, uw[u], zero_uw)) for c in range(n_chunks)] for u in n_units]

    state = [s_ref[hd] for hd in heads]
    zero_s = jnp.zeros((GDN_DK, GDN_DK), BF16)
    for sc in range(n_super):
        o_chunks = [[] for _ in heads]
        for c in range(n_chunks):
            rows = slice(c * C, (c + 1) * C)
            for h0 in range(0, GDN_HEADS, 2):
                h1 = h0 + 1
                u0, u1 = sc * GDN_HEADS + h0, sc * GDN_HEADS + h1
                lhs = jnp.concatenate(
                    [jnp.concatenate([q_eff[u0][rows], q_eff[u1][rows]], axis=1),
                     jnp.concatenate([nm[u0][c][:, GDN_DK:], nm[u1][c][:, GDN_DK:]], axis=1).astype(BF16)], axis=0)
                s_pair = jnp.concatenate(
                    [jnp.concatenate([state[h0].astype(BF16), zero_s], axis=1),
                     jnp.concatenate([zero_s, state[h1].astype(BF16)], axis=1)], axis=0)
                r = _dot(lhs, s_pair)
                for i, (hd, u) in enumerate(((h0, u0), (h1, u1))):
                    half = slice(i * GDN_DK, (i + 1) * GDN_DK)
                    o_chunks[hd].append(r[:C, half] + qkuw[u][rows, :GDN_DK])
                    state[hd] = state[hd] * g_last[u][c * C:c * C + 1, :] - r[C:, half] + nm[u][c][:, :GDN_DK]
        for hd in heads:
            o = jnp.concatenate(o_chunks[hd], axis=0)
            o = _rms(o, nw_ref[...]) * _silu(z_ref[rows_of(sc), lanes_of(hd)].astype(F32))
            o_ref[rows_of(sc), lanes_of(hd)] = o.astype(BF16)

    for hd in heads:
        s_ref[hd] = state[hd]


def _gdn(q, k, v, z, gb, nw):
    B, T, _ = q.shape
    rows = GDN_STEP
    tok = lambda w: pl.BlockSpec((None, rows, w), lambda b, i: (b, i, 0))
    return pl.pallas_call(
        _gdn_kernel,
        grid=(B, T // rows),
        in_specs=[tok(GDN_QK), tok(GDN_QK), tok(GDN_QK), tok(GDN_QK), tok(LANES),
                  pl.BlockSpec(nw.shape, lambda b, i: (0, 0))],
        out_specs=tok(GDN_QK),
        out_shape=jax.ShapeDtypeStruct((B, T, GDN_QK), BF16),
        scratch_shapes=[pltpu.VMEM((GDN_HEADS, GDN_DK, GDN_DK), F32)],
        compiler_params=pltpu.CompilerParams(
            dimension_semantics=("arbitrary", "arbitrary"), vmem_limit_bytes=VMEM_LIMIT_BYTES),
        name="gdn",
    )(q, k, v, z, gb, nw)


def _band_tables():
    qi = np.arange(WINDOW, dtype=np.int64)[:, None]
    sj = np.arange(2 * WINDOW, dtype=np.int64)[None, :]
    dist = qi + WINDOW - sj
    in_band = (dist >= 0) & (dist < WINDOW)
    d = np.maximum(dist, 0)
    max_exact = REL_BUCKETS // 2
    ratio = np.log(np.maximum(d, 1).astype(np.float32) / np.float32(max_exact)) / np.float32(
        math.log(REL_MAX_DIST / max_exact))
    large = max_exact + (ratio.astype(np.float32) * np.float32(REL_BUCKETS - max_exact)).astype(np.int32)
    large = np.minimum(large, REL_BUCKETS - 1)
    bucket = np.where(d < max_exact, d, large).astype(np.int32)
    valid = np.stack([in_band & (sj >= WINDOW), in_band]).astype(np.int32)
    return bucket, valid


def _bias_kernel(table_ref, bucket_ref, valid_ref, o_ref):
    bucket = bucket_ref[...]
    for hd in range(SWA_Q_HEADS):
        acc = jnp.zeros(bucket.shape, F32)
        for b in range(REL_BUCKETS):
            acc = jnp.where(bucket == b, table_ref[b, hd] * LOG2_E, acc)
        for var in range(2):
            o_ref[var, hd] = jnp.where(valid_ref[var] != 0, acc, NEG_INF)


def _rel_bias(table):
    bucket, valid = _band_tables()
    return pl.pallas_call(
        _bias_kernel,
        in_specs=[pl.BlockSpec(memory_space=pltpu.SMEM),
                  pl.BlockSpec(memory_space=pltpu.VMEM), pl.BlockSpec(memory_space=pltpu.VMEM)],
        out_specs=pl.BlockSpec(memory_space=pltpu.VMEM),
        out_shape=jax.ShapeDtypeStruct((2, SWA_Q_HEADS, WINDOW, 2 * WINDOW), F32),
        name="rel_bias",
    )(table.astype(F32), jnp.asarray(bucket), jnp.asarray(valid))


def _swa_kernel(sinks_ref, qs_ref, kv_ref, bias_ref, nw_ref, o_ref):
    W = WINDOW
    T = qs_ref.shape[0]
    group = SWA_Q_HEADS // SWA_KV_HEADS
    lo_half = lax.broadcasted_iota(jnp.int32, (W, LANES), 1) < SWA_HEAD_DIM

    def block(n, carry):
        r0 = pl.multiple_of(n * W, W)
        p0 = pl.multiple_of(jnp.maximum(n - 1, 0) * W, W)
        var = jnp.minimum(n, 1)
        q = qs_ref[pl.ds(r0, W), :]
        kv = jnp.concatenate([kv_ref[pl.ds(p0, W), :], kv_ref[pl.ds(r0, W), :]], axis=0).astype(F32)
        k_pair = kv[:, :LANES]
        v_pair = kv[:, LANES:]
        k_opts = (k_pair.astype(BF16), pltpu.roll(k_pair, SWA_HEAD_DIM, axis=1).astype(BF16))
        v_opts = (v_pair.astype(BF16), pltpu.roll(v_pair, SWA_HEAD_DIM, axis=1).astype(BF16))

        pairs = []
        ss = jnp.zeros((W, 1), F32)
        for pr in range(SWA_Q_HEADS // 2):
            halves = []
            for half in range(2):
                hd = 2 * pr + half
                kvh = hd // group
                sel = 0 if half == kvh else 1
                qp = q[:, pr * LANES:(pr + 1) * LANES]
                qm = jnp.where(lo_half if half == 0 else jnp.logical_not(lo_half), qp, jnp.zeros_like(qp))
                s = _dot_nt(qm, k_opts[sel]) + bias_ref[var, hd]
                sink = sinks_ref[hd] * LOG2_E
                m = jnp.maximum(jnp.max(s, axis=-1, keepdims=True), sink)
                e = jnp.exp2(s - m)
                denom = jnp.sum(e, axis=-1, keepdims=True) + jnp.exp2(sink - m)
                halves.append(_dot(e.astype(BF16), v_opts[sel]) / denom)
            o_pair = jnp.where(lo_half, halves[0], halves[1])
            ss = ss + jnp.sum(o_pair * o_pair, axis=-1, keepdims=True)
            pairs.append(o_pair)
        inv = lax.rsqrt(ss * (1.0 / SWA_Q) + NORM_EPS)
        for pr, o_pair in enumerate(pairs):
            lanes = slice(pr * LANES, (pr + 1) * LANES)
            o_ref[pl.ds(r0, W), lanes] = (o_pair * inv * nw_ref[:, lanes]).astype(BF16)
        return carry

    lax.fori_loop(0, T // W, block, 0, unroll=SWA_UNROLL)


def _swa(sinks, qs, kvs, bias, nw):
    B, T, _ = qs.shape
    seq = lambda w: pl.BlockSpec((None, T, w), lambda b: (b, 0, 0))
    return pl.pallas_call(
        _swa_kernel,
        grid=(B,),
        in_specs=[pl.BlockSpec(memory_space=pltpu.SMEM), seq(SWA_Q), seq(2 * SWA_KV),
                  pl.BlockSpec(bias.shape, lambda b: (0, 0, 0, 0)),
                  pl.BlockSpec(nw.shape, lambda b: (0, 0))],
        out_specs=seq(SWA_Q),
        out_shape=jax.ShapeDtypeStruct((B, T, SWA_Q), BF16),
        compiler_params=pltpu.CompilerParams(
            dimension_semantics=("arbitrary",), vmem_limit_bytes=VMEM_LIMIT_BYTES),
        name="swa",
    )(sinks, qs, kvs, bias, nw)


GELU_K0 = math.sqrt(2.0 / math.pi)
GELU_K1 = 0.044715 * GELU_K0


def _gelu_tanh_x2(x):
    return x * (1.0 + jnp.tanh(x * (GELU_K0 + GELU_K1 * (x * x))))


def _mlp_kernel(x_ref, og_ref, os_ref, wout_ref, pmw_ref, pfw_ref, wgate_ref, wup_ref, convw_ref, convb_ref,
                wdown_ref, postw_ref, o_ref, pad_ref, carry_ref, act_ref):
    tm = x_ref.shape[0]
    halo = SUBLANES
    n_ff = wgate_ref.shape[0]

    @pl.when(pl.program_id(1) == 0)
    def _():
        carry_ref[...] = jnp.zeros(carry_ref.shape, F32)

    mix = _dot(og_ref[...], wout_ref[0:GDN_QK, :]) + _dot(os_ref[...], wout_ref[GDN_QK:, :])
    x1 = x_ref[...] + _rms(mix, pmw_ref[...])
    h = _rms(x1, pfw_ref[...]).astype(BF16)

    for c in range(n_ff):
        gate = _dot(h, wgate_ref[c])
        up = _dot(h, wup_ref[c])
        pad_ref[0:halo, :] = carry_ref[c]
        pad_ref[halo:halo + tm, :] = gate
        carry_ref[c] = gate[tm - halo:, :]
        cw = convw_ref[c]
        y = gate * cw[FFN_CONV - 1:FFN_CONV] + convb_ref[c]
        for j in range(FFN_CONV - 1):
            r0 = halo - (FFN_CONV - 1) + j
            y = y + pad_ref[r0:r0 + tm, :] * cw[j:j + 1]
        act_ref[:, c * FF_CHUNK:(c + 1) * FF_CHUNK] = (_gelu_tanh_x2(y) * up).astype(BF16)

    y = _dot(act_ref[...], wdown_ref[...])
    o_ref[...] = x1 + _rms(y, postw_ref[...])


def _mlp(x, og, osw, wout, pmw, pfw, wgate, wup, convw, convb, wdown, postw, tm):
    B, T, D = x.shape
    n_ff = wgate.shape[0]
    tok = lambda w: pl.BlockSpec((None, tm, w), lambda b, i: (b, i, 0))
    full = lambda a: pl.BlockSpec(a.shape, lambda b, i: (0,) * a.ndim, pipeline_mode=pl.Buffered(1))
    return pl.pallas_call(
        _mlp_kernel,
        grid=(B, T // tm),
        in_specs=[tok(D), tok(GDN_QK), tok(SWA_Q), full(wout), full(pmw), full(pfw), full(wgate), full(wup),
                  full(convw), full(convb), full(wdown), full(postw)],
        out_specs=tok(D),
        out_shape=jax.ShapeDtypeStruct((B, T, D), x.dtype),
        scratch_shapes=[pltpu.VMEM((tm + SUBLANES, FF_CHUNK), F32),
                        pltpu.VMEM((n_ff, SUBLANES, FF_CHUNK), F32),
                        pltpu.VMEM((tm, n_ff * FF_CHUNK), BF16)],
        compiler_params=pltpu.CompilerParams(
            dimension_semantics=("arbitrary", "arbitrary"), vmem_limit_bytes=VMEM_LIMIT_BYTES),
        name="mlp",
    )(x, og, osw, wout, pmw, pfw, wgate, wup, convw, convb, wdown, postw)


def _token_tile(T):
    for tm in (512, 256, 128):
        if T % tm == 0:
            return tm
    raise ValueError(f"sequence length {T} must be a multiple of 128")


def kernel(x, pre_mix_norm_w, w_in, gdn_conv_w, gdn_a_log, gdn_dt_bias, gdn_norm_w, swa_sinks, rel_bias_table,
           swa_norm_w, w_out, post_mix_norm_w, pre_ffn_norm_w, w_gate, w_up, ffn_conv_w, ffn_conv_b, w_down,
           post_ffn_norm_w):
    B, T, D = x.shape
    depth = w_in.shape[0]
    d_ff = w_gate.shape[-1]
    assert T % GDN_STEP == 0 and T % WINDOW == 0 and d_ff % FF_CHUNK == 0
    tm = _token_tile(T)
    n_ff = d_ff // FF_CHUNK
    gdn_qkv = 3 * GDN_QK
    n_gate = 2 * GDN_HEADS
    ab0 = gdn_qkv + GDN_QK
    row = lambda a: a.reshape(1, -1).astype(F32)

    bias = _rel_bias(rel_bias_table)

    for l in range(depth):
        wl = w_in[l]
        qs0 = ab0 + n_gate
        wmain = jnp.concatenate([wl[:, :ab0], wl[:, qs0:qs0 + SWA_Q] * (SWA_HEAD_DIM ** -0.5 * LOG2_E),
                                 wl[:, qs0 + SWA_Q:]], axis=1).astype(BF16)
        wab = jnp.pad(wl[:, ab0:ab0 + n_gate], ((0, 0), (0, LANES - n_gate))).astype(BF16)
        alog_row = jnp.pad(row(gdn_a_log[l]), ((0, 0), (0, LANES - GDN_HEADS)))
        dtb_row = jnp.pad(row(gdn_dt_bias[l]), ((0, 0), (0, LANES - GDN_HEADS)))

        q, k, v, z, gb, qs, kvs = _inproj(x, row(pre_mix_norm_w[l]), wmain, wab, gdn_conv_w[l].astype(F32),
                                          alog_row, dtb_row, tm)
        o_g = _gdn(q, k, v, z, gb, row(gdn_norm_w[l]))
        o_s = _swa(swa_sinks[l].astype(F32), qs, kvs, bias, row(swa_norm_w[l]))

        chunked = lambda w: jnp.transpose(w.reshape(w.shape[0], n_ff, FF_CHUNK), (1, 0, 2))
        x = _mlp(x, o_g, o_s, w_out[l].astype(BF16), row(post_mix_norm_w[l]), row(pre_ffn_norm_w[l]),
                 chunked(w_gate[l]).astype(BF16), chunked(0.5 * w_up[l]).astype(BF16),
                 chunked(ffn_conv_w[l].astype(F32)), chunked(row(ffn_conv_b[l])),
                 w_down[l].astype(BF16), row(post_ffn_norm_w[l]), tm)
    return x
```

```python
import functools
import math

import numpy as np
import jax
import jax.numpy as jnp
from jax import lax
from jax.experimental import pallas as pl
from jax.experimental.pallas import tpu as pltpu

F32 = jnp.float32
BF16 = jnp.bfloat16

GDN_HEADS = 4
GDN_DK = 128
GDN_CONV = 4
GDN_CHUNK = 64
SWA_Q_HEADS = 8
SWA_KV_HEADS = 2
SWA_HEAD_DIM = 64
WINDOW = 128
REL_BUCKETS = 32
REL_MAX_DIST = 128
FFN_CONV = 3
NORM_EPS = 1e-6
NEG_INF = -1e30
LOG2_E = math.log2(math.e)

GDN_QK = GDN_HEADS * GDN_DK
SWA_Q = SWA_Q_HEADS * SWA_HEAD_DIM
SWA_KV = SWA_KV_HEADS * SWA_HEAD_DIM

LANES = 128
SUBLANES = 8
VMEM_LIMIT_BYTES = 56 * 1024 * 1024

GDN_SUPER = 4 * GDN_CHUNK
GDN_STEP = 2 * GDN_SUPER
FF_CHUNK = 256
INPROJ_TILE = 512
MLP_TILE = 1024
SWA_UNROLL = 4


def _rms_unit(x):
    return x * lax.rsqrt(jnp.mean(x * x, axis=-1, keepdims=True) + NORM_EPS)


def _rms(x, w):
    return _rms_unit(x) * w


def _sigmoid(x):
    return 1.0 / (1.0 + jnp.exp2(x * (-LOG2_E)))


def _silu(x):
    return x * _sigmoid(x)


def _dot(a, b):
    return jnp.dot(a, b, preferred_element_type=F32)


def _dot_nt(a, b):
    return lax.dot_general(a, b, (((1,), (1,)), ((), ())), preferred_element_type=F32)


def _inproj_kernel(x_ref, wmain_ref, wab_ref, convw_ref, alog_ref, dtb_ref,
                   q_ref, k_ref, v_ref, z_ref, gb_ref, qs_ref, kvs_ref, pad_ref):
    tm = x_ref.shape[0]
    halo = SUBLANES

    @pl.when(pl.program_id(1) == 0)
    def _():
        pad_ref[:, 0:halo, :] = jnp.zeros((3, halo, GDN_QK), F32)

    h = _rms_unit(x_ref[...]).astype(BF16)

    ab = _dot(h, wab_ref[...])
    lane = lax.broadcasted_iota(jnp.int32, ab.shape, 1)
    pre = ab + dtb_ref[...]
    softplus = jnp.maximum(pre, 0.0) + jnp.log(1.0 + jnp.exp(-jnp.abs(pre)))
    g = -jnp.exp(alog_ref[...]) * softplus
    gb_ref[...] = jnp.where(lane < GDN_HEADS, g, _sigmoid(ab))

    outs = (q_ref, k_ref, v_ref)
    for grp in range(3):
        c0 = grp * GDN_QK
        raw = _dot(h, wmain_ref[:, c0:c0 + GDN_QK])
        pad_ref[grp, halo:halo + tm, :] = raw
        cw = convw_ref[:, c0:c0 + GDN_QK]
        y = raw * cw[GDN_CONV - 1:GDN_CONV]
        for j in range(GDN_CONV - 1):
            r0 = halo - (GDN_CONV - 1) + j
            y = y + pad_ref[grp, r0:r0 + tm, :] * cw[j:j + 1]
        pad_ref[grp, 0:halo, :] = pad_ref[grp, tm:tm + halo, :]
        y = _silu(y)
        if grp < 2:
            scale = GDN_DK ** -0.5 if grp == 0 else 1.0
            for hd in range(GDN_HEADS):
                yh = y[:, hd * GDN_DK:(hd + 1) * GDN_DK]
                inv = lax.rsqrt(jnp.sum(yh * yh, axis=-1, keepdims=True) + NORM_EPS) * scale
                outs[grp][:, hd * GDN_DK:(hd + 1) * GDN_DK] = (yh * inv).astype(BF16)
        else:
            v_ref[...] = y.astype(BF16)

    c0 = 3 * GDN_QK
    z_ref[...] = _dot(h, wmain_ref[:, c0:c0 + GDN_QK]).astype(BF16)
    c0 += GDN_QK
    qs_ref[...] = _dot(h, wmain_ref[:, c0:c0 + SWA_Q]).astype(BF16)
    c0 += SWA_Q
    kvs_ref[...] = _dot(h, wmain_ref[:, c0:c0 + 2 * SWA_KV]).astype(BF16)


def _inproj(x, wmain, wab, convw, alog_row, dtb_row, tm):
    B, T, D = x.shape
    tok = lambda w: pl.BlockSpec((None, tm, w), lambda b, i: (b, i, 0))
    full = lambda a: pl.BlockSpec(a.shape, lambda b, i: (0,) * a.ndim)
    sds = lambda w, dt: jax.ShapeDtypeStruct((B, T, w), dt)
    return pl.pallas_call(
        _inproj_kernel,
        grid=(B, T // tm),
        in_specs=[tok(D), full(wmain), full(wab), full(convw), full(alog_row), full(dtb_row)],
        out_specs=[tok(GDN_QK), tok(GDN_QK), tok(GDN_QK), tok(GDN_QK), tok(LANES), tok(SWA_Q), tok(2 * SWA_KV)],
        out_shape=[sds(GDN_QK, BF16), sds(GDN_QK, BF16), sds(GDN_QK, BF16), sds(GDN_QK, BF16),
                   sds(LANES, F32), sds(SWA_Q, BF16), sds(2 * SWA_KV, BF16)],
        scratch_shapes=[pltpu.VMEM((3, tm + SUBLANES, GDN_QK), F32)],
        compiler_params=pltpu.CompilerParams(
            dimension_semantics=("arbitrary", "arbitrary"), vmem_limit_bytes=VMEM_LIMIT_BYTES),
        name="inproj",
    )(x, wmain, wab, convw, alog_row, dtb_row)


def _split3(a):
    hi = a.astype(BF16)
    r1 = a - hi.astype(F32)
    mid = r1.astype(BF16)
    lo = (r1 - mid.astype(F32)).astype(BF16)
    return hi, mid, lo


def _lane_col(a, idx):
    lane = lax.broadcasted_iota(jnp.int32, a.shape, 1)
    return jnp.sum(jnp.where(lane == idx, a, 0.0), axis=-1, keepdims=True)


def _gdn_kernel(q_ref, k_ref, v_ref, z_ref, gb_ref, nw_ref, o_ref, s_ref):
    R = GDN_SUPER
    C = GDN_CHUNK
    n_chunks = R // C
    n_super = q_ref.shape[0] // R
    shift = int(math.log2(C))
    heads = range(GDN_HEADS)
    units = [(sc, hd) for sc in range(n_super) for hd in heads]

    @pl.when(pl.program_id(1) == 0)
    def _():
        s_ref[...] = jnp.zeros(s_ref.shape, F32)

    row = lax.broadcasted_iota(jnp.int32, (R, R), 0)
    col = lax.broadcasted_iota(jnp.int32, (R, R), 1)
    same = (row >> shift) == (col >> shift)
    causal = jnp.logical_and(same, row >= col)
    strict = jnp.logical_and(same, row > col)
    eye_packed = jnp.where((lax.broadcasted_iota(jnp.int32, (C, R), 1) & (C - 1))
                           == lax.broadcasted_iota(jnp.int32, (C, R), 0), 1.0, 0.0)
    sums_to = jnp.concatenate([jnp.where(jnp.logical_and(same, row <= col), 1.0, 0.0),
                               jnp.where(same, 1.0, 0.0)], axis=1).astype(BF16)

    def pack(full):
        return sum(full[j * C:(j + 1) * C] for j in range(1, n_chunks)) + full[0:C]

    def block_diag(packed):
        return jnp.where(same, jnp.concatenate([packed] * n_chunks, axis=0), jnp.zeros((R, R), BF16))

    gbs, gcum_rows, cols = [], [], []
    for sc in range(n_super):
        gb = gb_ref[sc * R:(sc + 1) * R, :]
        hi, mid, lo = (p.astype(F32)[0:SUBLANES] for p in _split3(gb.T))
        parts = jnp.concatenate([hi, mid, lo, jnp.zeros_like(hi)], axis=0).astype(BF16)
        sums = _dot(parts, sums_to)
        sums = sums[0:SUBLANES] + sums[SUBLANES:2 * SUBLANES] + sums[2 * SUBLANES:3 * SUBLANES]
        gbs.append(gb)
        gcum_rows.append(sums[:, :R])
        cols.append(jnp.concatenate([sums[:, :R], sums[:, R:], jnp.zeros((LANES - 2 * SUBLANES, R), F32)], axis=0).T)

    rows_of = lambda sc: slice(sc * R, (sc + 1) * R)
    lanes_of = lambda hd: slice(hd * GDN_DK, (hd + 1) * GDN_DK)
    wide = lambda c: jnp.broadcast_to(c, (R, GDN_DK))
    gc = [_lane_col(cols[sc], hd) for sc, hd in units]
    gl = [_lane_col(cols[sc], SUBLANES + hd) for sc, hd in units]
    beta = [_lane_col(gbs[sc], GDN_HEADS + hd) for sc, hd in units]
    n_units = range(len(units))
    decay = [jnp.where(causal, jnp.exp(jnp.where(causal, gc[u] - gcum_rows[sc][hd:hd + 1, :], 0.0)), 0.0)
             for u, (sc, hd) in enumerate(units)]
    qh = [q_ref[rows_of(sc), lanes_of(hd)] for sc, hd in units]
    kh = [k_ref[rows_of(sc), lanes_of(hd)] for sc, hd in units]
    kf = [kh[u].astype(F32) for u in n_units]
    kb = [kf[u] * beta[u] for u in n_units]
    eg = [jnp.exp(wide(gc[u])) for u in n_units]
    rhs = [jnp.concatenate([v_ref[rows_of(sc), lanes_of(hd)].astype(F32) * beta[u], kb[u] * eg[u]],
                           axis=1).astype(BF16) for u, (sc, hd) in enumerate(units)]

    kq = [_dot_nt(jnp.concatenate([kb[u].astype(BF16), qh[u]], axis=0), kh[u]) for u in n_units]
    a = [jnp.where(strict, kq[u][:R] * decay[u], 0.0) for u in n_units]
    qk = [jnp.where(causal, kq[u][R:] * decay[u], 0.0).astype(BF16) for u in n_units]

    a_packed = [pack(a[u]) for u in n_units]
    p = [eye_packed - a_packed[u] for u in n_units]
    xpow = [_dot(a_packed[u].astype(BF16), a[u].astype(BF16)) for u in n_units]
    for _ in range(shift - 2):
        xb = [xpow[u].astype(BF16) for u in n_units]
        y = [_dot(jnp.concatenate([p[u].astype(BF16), xb[u]], axis=0), block_diag(xb[u])) for u in n_units]
        p = [p[u] + y[u][:C] for u in n_units]
        xpow = [y[u][C:] for u in n_units]
    p = [p[u] + _dot(p[u].astype(BF16), block_diag(xpow[u].astype(BF16))) for u in n_units]

    uw = [_dot(block_diag(p[u].astype(BF16)), rhs[u]).astype(BF16) for u in n_units]
    qkuw = [_dot(qk[u], uw[u]) for u in n_units]
    q_eff = [(qh[u].astype(F32) * eg[u] - qkuw[u][:, GDN_DK:]).astype(BF16) for u in n_units]
    kd_t = [(kf[u] * jnp.exp(wide(gl[u] - gc[u]))).T.astype(BF16) for u in n_units]
    g_last = [jnp.exp(wide(gl[u])) for u in n_units]
    row_chunk = lax.broadcasted_iota(jnp.int32, (R, 2 * GDN_DK), 0) >> shift
    zero_uw = jnp.zeros((R, 2 * GDN_DK), BF16)
    nm = [[_dot(kd_t[u], jnp.where(row_chunk == c, uw[u], zero_uw)) for c in range(n_chunks)] for u in n_units]

    state = [s_ref[hd] for hd in heads]
    zero_s = jnp.zeros((GDN_DK, GDN_DK), BF16)
    for sc in range(n_super):
        o_chunks = [[] for _ in heads]
        for c in range(n_chunks):
            rows = slice(c * C, (c + 1) * C)
            for h0 in range(0, GDN_HEADS, 2):
                h1 = h0 + 1
                u0, u1 = sc * GDN_HEADS + h0, sc * GDN_HEADS + h1
                lhs = jnp.concatenate(
                    [jnp.concatenate([q_eff[u0][rows], q_eff[u1][rows]], axis=1),
                     jnp.concatenate([nm[u0][c][:, GDN_DK:], nm[u1][c][:, GDN_DK:]], axis=1).astype(BF16)], axis=0)
                s_pair = jnp.concatenate(
                    [jnp.concatenate([state[h0].astype(BF16), zero_s], axis=1),
                     jnp.concatenate([zero_s, state[h1].astype(BF16)], axis=1)], axis=0)
                r = _dot(lhs, s_pair)
                for i, (hd, u) in enumerate(((h0, u0), (h1, u1))):
                    half = slice(i * GDN_DK, (i + 1) * GDN_DK)
                    o_chunks[hd].append(r[:C, half] + qkuw[u][rows, :GDN_DK])
                    state[hd] = state[hd] * g_last[u][c * C:c * C + 1, :] - r[C:, half] + nm[u][c][:, :GDN_DK]
        for hd in heads:
            o = jnp.concatenate(o_chunks[hd], axis=0)
            o = _rms(o, nw_ref[...]) * _silu(z_ref[rows_of(sc), lanes_of(hd)].astype(F32))
            o_ref[rows_of(sc), lanes_of(hd)] = o.astype(BF16)

    for hd in heads:
        s_ref[hd] = state[hd]


def _gdn(q, k, v, z, gb, nw):
    B, T, _ = q.shape
    rows = GDN_STEP
    tok = lambda w: pl.BlockSpec((None, rows, w), lambda b, i: (b, i, 0))
    return pl.pallas_call(
        _gdn_kernel,
        grid=(B, T // rows),
        in_specs=[tok(GDN_QK), tok(GDN_QK), tok(GDN_QK), tok(GDN_QK), tok(LANES),
                  pl.BlockSpec(nw.shape, lambda b, i: (0, 0))],
        out_specs=tok(GDN_QK),
        out_shape=jax.ShapeDtypeStruct((B, T, GDN_QK), BF16),
        scratch_shapes=[pltpu.VMEM((GDN_HEADS, GDN_DK, GDN_DK), F32)],
        compiler_params=pltpu.CompilerParams(
            dimension_semantics=("arbitrary", "arbitrary"), vmem_limit_bytes=VMEM_LIMIT_BYTES),
        name="gdn",
    )(q, k, v, z, gb, nw)


def _band_tables():
    qi = np.arange(WINDOW, dtype=np.int64)[:, None]
    sj = np.arange(2 * WINDOW, dtype=np.int64)[None, :]
    dist = qi + WINDOW - sj
    in_band = (dist >= 0) & (dist < WINDOW)
    d = np.maximum(dist, 0)
    max_exact = REL_BUCKETS // 2
    ratio = np.log(np.maximum(d, 1).astype(np.float32) / np.float32(max_exact)) / np.float32(
        math.log(REL_MAX_DIST / max_exact))
    large = max_exact + (ratio.astype(np.float32) * np.float32(REL_BUCKETS - max_exact)).astype(np.int32)
    large = np.minimum(large, REL_BUCKETS - 1)
    bucket = np.where(d < max_exact, d, large).astype(np.int32)
    valid = np.stack([in_band & (sj >= WINDOW), in_band]).astype(np.int32)
    return bucket, valid


def _bias_kernel(table_ref, bucket_ref, valid_ref, o_ref):
    bucket = bucket_ref[...]
    for hd in range(SWA_Q_HEADS):
        acc = jnp.zeros(bucket.shape, F32)
        for b in range(REL_BUCKETS):
            acc = jnp.where(bucket == b, table_ref[b, hd] * LOG2_E, acc)
        for var in range(2):
            o_ref[var, hd] = jnp.where(valid_ref[var] != 0, acc, NEG_INF)


def _rel_bias(table):
    bucket, valid = _band_tables()
    return pl.pallas_call(
        _bias_kernel,
        in_specs=[pl.BlockSpec(memory_space=pltpu.SMEM),
                  pl.BlockSpec(memory_space=pltpu.VMEM), pl.BlockSpec(memory_space=pltpu.VMEM)],
        out_specs=pl.BlockSpec(memory_space=pltpu.VMEM),
        out_shape=jax.ShapeDtypeStruct((2, SWA_Q_HEADS, WINDOW, 2 * WINDOW), F32),
        name="rel_bias",
    )(table.astype(F32), jnp.asarray(bucket), jnp.asarray(valid))


def _swa_kernel(sinks_ref, qs_ref, kv_ref, bias_ref, nw_ref, o_ref):
    W = WINDOW
    T = qs_ref.shape[0]
    group = SWA_Q_HEADS // SWA_KV_HEADS
    lo_half = lax.broadcasted_iota(jnp.int32, (W, LANES), 1) < SWA_HEAD_DIM

    def block(n, carry):
        r0 = pl.multiple_of(n * W, W)
        p0 = pl.multiple_of(jnp.maximum(n - 1, 0) * W, W)
        var = jnp.minimum(n, 1)
        q = qs_ref[pl.ds(r0, W), :]
        kv = jnp.concatenate([kv_ref[pl.ds(p0, W), :], kv_ref[pl.ds(r0, W), :]], axis=0).astype(F32)
        k_pair = kv[:, :LANES]
        v_pair = kv[:, LANES:]
        k_opts = (k_pair.astype(BF16), pltpu.roll(k_pair, SWA_HEAD_DIM, axis=1).astype(BF16))
        v_opts = (v_pair.astype(BF16), pltpu.roll(v_pair, SWA_HEAD_DIM, axis=1).astype(BF16))

        pairs = []
        ss = jnp.zeros((W, 1), F32)
        for pr in range(SWA_Q_HEADS // 2):
            halves = []
            for half in range(2):
                hd = 2 * pr + half
                kvh = hd // group
                sel = 0 if half == kvh else 1
                qp = q[:, pr * LANES:(pr + 1) * LANES]
                qm = jnp.where(lo_half if half == 0 else jnp.logical_not(lo_half), qp, jnp.zeros_like(qp))
                s = _dot_nt(qm, k_opts[sel]) + bias_ref[var, hd]
                sink = sinks_ref[hd] * LOG2_E
                m = jnp.maximum(jnp.max(s, axis=-1, keepdims=True), sink)
                e = jnp.exp2(s - m)
                denom = jnp.sum(e, axis=-1, keepdims=True) + jnp.exp2(sink - m)
                halves.append(_dot(e.astype(BF16), v_opts[sel]) / denom)
            o_pair = jnp.where(lo_half, halves[0], halves[1])
            ss = ss + jnp.sum(o_pair * o_pair, axis=-1, keepdims=True)
            pairs.append(o_pair)
        inv = lax.rsqrt(ss * (1.0 / SWA_Q) + NORM_EPS)
        for pr, o_pair in enumerate(pairs):
            lanes = slice(pr * LANES, (pr + 1) * LANES)
            o_ref[pl.ds(r0, W), lanes] = (o_pair * inv * nw_ref[:, lanes]).astype(BF16)
        return carry

    lax.fori_loop(0, T // W, block, 0, unroll=SWA_UNROLL)


def _swa(sinks, qs, kvs, bias, nw):
    B, T, _ = qs.shape
    seq = lambda w: pl.BlockSpec((None, T, w), lambda b: (b, 0, 0))
    return pl.pallas_call(
        _swa_kernel,
        grid=(B,),
        in_specs=[pl.BlockSpec(memory_space=pltpu.SMEM), seq(SWA_Q), seq(2 * SWA_KV),
                  pl.BlockSpec(bias.shape, lambda b: (0, 0, 0, 0)),
                  pl.BlockSpec(nw.shape, lambda b: (0, 0))],
        out_specs=seq(SWA_Q),
        out_shape=jax.ShapeDtypeStruct((B, T, SWA_Q), BF16),
        compiler_params=pltpu.CompilerParams(
            dimension_semantics=("arbitrary",), vmem_limit_bytes=VMEM_LIMIT_BYTES),
        name="swa",
    )(sinks, qs, kvs, bias, nw)


GELU_K0 = math.sqrt(2.0 / math.pi)
GELU_K1 = 0.044715 * GELU_K0


def _gelu_tanh_x2(x):
    return x * (1.0 + jnp.tanh(x * (GELU_K0 + GELU_K1 * (x * x))))


def _mlp_kernel(x_ref, og_ref, os_ref, wout_ref, pmw_ref, wgate_ref, wup_ref, convw_ref, convb_ref,
                wdown_ref, postw_ref, o_ref, pad_ref, carry_ref, act_ref):
    tm = x_ref.shape[0]
    halo = SUBLANES
    n_ff = wgate_ref.shape[0]

    @pl.when(pl.program_id(1) == 0)
    def _():
        carry_ref[...] = jnp.zeros(carry_ref.shape, F32)

    mix = _dot(og_ref[...], wout_ref[0:GDN_QK, :]) + _dot(os_ref[...], wout_ref[GDN_QK:, :])
    x1 = x_ref[...] + _rms(mix, pmw_ref[...])
    h = _rms_unit(x1).astype(BF16)

    for c in range(n_ff):
        gate = _dot(h, wgate_ref[c])
        up = _dot(h, wup_ref[c])
        pad_ref[0:halo, :] = carry_ref[c]
        pad_ref[halo:halo + tm, :] = gate
        carry_ref[c] = gate[tm - halo:, :]
        cw = convw_ref[c]
        y = gate * cw[FFN_CONV - 1:FFN_CONV] + convb_ref[c]
        for j in range(FFN_CONV - 1):
            r0 = halo - (FFN_CONV - 1) + j
            y = y + pad_ref[r0:r0 + tm, :] * cw[j:j + 1]
        act_ref[:, c * FF_CHUNK:(c + 1) * FF_CHUNK] = (_gelu_tanh_x2(y) * up).astype(BF16)

    y = _dot(act_ref[...], wdown_ref[...])
    o_ref[...] = x1 + _rms(y, postw_ref[...])


def _mlp(x, og, osw, wout, pmw, wgate, wup, convw, convb, wdown, postw, tm):
    B, T, D = x.shape
    n_ff = wgate.shape[0]
    tok = lambda w: pl.BlockSpec((None, tm, w), lambda b, i: (b, i, 0))
    full = lambda a: pl.BlockSpec(a.shape, lambda b, i: (0,) * a.ndim, pipeline_mode=pl.Buffered(1))
    return pl.pallas_call(
        _mlp_kernel,
        grid=(B, T // tm),
        in_specs=[tok(D), tok(GDN_QK), tok(SWA_Q), full(wout), full(pmw), full(wgate), full(wup),
                  full(convw), full(convb), full(wdown), full(postw)],
        out_specs=tok(D),
        out_shape=jax.ShapeDtypeStruct((B, T, D), x.dtype),
        scratch_shapes=[pltpu.VMEM((tm + SUBLANES, FF_CHUNK), F32),
                        pltpu.VMEM((n_ff, SUBLANES, FF_CHUNK), F32),
                        pltpu.VMEM((tm, n_ff * FF_CHUNK), BF16)],
        compiler_params=pltpu.CompilerParams(
            dimension_semantics=("arbitrary", "arbitrary"), vmem_limit_bytes=VMEM_LIMIT_BYTES),
        name="mlp",
    )(x, og, osw, wout, pmw, wgate, wup, convw, convb, wdown, postw)


def _token_tile(T, largest):
    for tm in (1024, 512, 256, 128):
        if tm <= largest and T % tm == 0:
            return tm
    raise ValueError(f"sequence length {T} must be a multiple of 128")


def kernel(x, pre_mix_norm_w, w_in, gdn_conv_w, gdn_a_log, gdn_dt_bias, gdn_norm_w, swa_sinks, rel_bias_table,
           swa_norm_w, w_out, post_mix_norm_w, pre_ffn_norm_w, w_gate, w_up, ffn_conv_w, ffn_conv_b, w_down,
           post_ffn_norm_w):
    B, T, D = x.shape
    depth = w_in.shape[0]
    d_ff = w_gate.shape[-1]
    assert T % GDN_STEP == 0 and T % WINDOW == 0 and d_ff % FF_CHUNK == 0
    tm_in = _token_tile(T, INPROJ_TILE)
    tm_mlp = _token_tile(T, MLP_TILE)
    n_ff = d_ff // FF_CHUNK
    gdn_qkv = 3 * GDN_QK
    n_gate = 2 * GDN_HEADS
    ab0 = gdn_qkv + GDN_QK
    row = lambda a: a.reshape(1, -1).astype(F32)

    bias = _rel_bias(rel_bias_table)

    for l in range(depth):
        wl = w_in[l] * pre_mix_norm_w[l].astype(F32)[:, None]
        qs0 = ab0 + n_gate
        wmain = jnp.concatenate([wl[:, :ab0], wl[:, qs0:qs0 + SWA_Q] * (SWA_HEAD_DIM ** -0.5 * LOG2_E),
                                 wl[:, qs0 + SWA_Q:]], axis=1).astype(BF16)
        wab = jnp.pad(wl[:, ab0:ab0 + n_gate], ((0, 0), (0, LANES - n_gate))).astype(BF16)
        alog_row = jnp.pad(row(gdn_a_log[l]), ((0, 0), (0, LANES - GDN_HEADS)))
        dtb_row = jnp.pad(row(gdn_dt_bias[l]), ((0, 0), (0, LANES - GDN_HEADS)))

        q, k, v, z, gb, qs, kvs = _inproj(x, wmain, wab, gdn_conv_w[l].astype(F32), alog_row, dtb_row, tm_in)
        o_g = _gdn(q, k, v, z, gb, row(gdn_norm_w[l]))
        o_s = _swa(swa_sinks[l].astype(F32), qs, kvs, bias, row(swa_norm_w[l]))

        chunked = lambda w: jnp.transpose(w.reshape(w.shape[0], n_ff, FF_CHUNK), (1, 0, 2))
        ffn_gain = pre_ffn_norm_w[l].astype(F32)[:, None]
        x = _mlp(x, o_g, o_s, w_out[l].astype(BF16), row(post_mix_norm_w[l]),
                 chunked(w_gate[l] * ffn_gain).astype(BF16), chunked(w_up[l] * (0.5 * ffn_gain)).astype(BF16),
                 chunked(ffn_conv_w[l].astype(F32)), chunked(row(ffn_conv_b[l])),
                 w_down[l].astype(BF16), row(post_ffn_norm_w[l]), tm_mlp)
    return x
```

```python
import functools
import math

import numpy as np
import jax
import jax.numpy as jnp
from jax import lax
from jax.experimental import pallas as pl
from jax.experimental.pallas import tpu as pltpu

F32 = jnp.float32
BF16 = jnp.bfloat16

GDN_HEADS = 4
GDN_DK = 128
GDN_CONV = 4
GDN_CHUNK = 64
SWA_Q_HEADS = 8
SWA_KV_HEADS = 2
SWA_HEAD_DIM = 64
WINDOW = 128
REL_BUCKETS = 32
REL_MAX_DIST = 128
FFN_CONV = 3
NORM_EPS = 1e-6
NEG_INF = -1e30
LOG2_E = math.log2(math.e)

GDN_QK = GDN_HEADS * GDN_DK
SWA_Q = SWA_Q_HEADS * SWA_HEAD_DIM
SWA_KV = SWA_KV_HEADS * SWA_HEAD_DIM

LANES = 128
SUBLANES = 8
VMEM_LIMIT_BYTES = 56 * 1024 * 1024

GDN_SUPER = 4 * GDN_CHUNK
GDN_STEP = 4 * GDN_SUPER
FF_CHUNK = 256
INPROJ_TILE = 1024
MLP_TILE = 1024
SWA_UNROLL = 4


def _rms(x, w):
    return x * lax.rsqrt(jnp.mean(x * x, axis=-1, keepdims=True) + NORM_EPS) * w


def _sigmoid(x):
    return 1.0 / (1.0 + jnp.exp2(x * (-LOG2_E)))


def _silu(x):
    return x * _sigmoid(x)


def _dot(a, b):
    return jnp.dot(a, b, preferred_element_type=F32)


def _dot_nt(a, b):
    return lax.dot_general(a, b, (((1,), (1,)), ((), ())), preferred_element_type=F32)


def _inproj_kernel(x_ref, nw_ref, wmain_ref, wab_ref, convw_ref, alog_ref, dtb_ref,
                   q_ref, k_ref, v_ref, z_ref, gb_ref, qs_ref, kvs_ref, pad_ref):
    tm = x_ref.shape[0]
    halo = SUBLANES

    @pl.when(pl.program_id(1) == 0)
    def _():
        pad_ref[:, 0:halo, :] = jnp.zeros((3, halo, GDN_QK), F32)

    h = _rms(x_ref[...], nw_ref[...]).astype(BF16)

    ab = _dot(h, wab_ref[...])
    lane = lax.broadcasted_iota(jnp.int32, ab.shape, 1)
    pre = ab + dtb_ref[...]
    softplus = jnp.maximum(pre, 0.0) + jnp.log(1.0 + jnp.exp(-jnp.abs(pre)))
    g = -jnp.exp(alog_ref[...]) * softplus
    gb_ref[...] = jnp.where(lane < GDN_HEADS, g, _sigmoid(ab))

    outs = (q_ref, k_ref, v_ref)
    for grp in range(3):
        c0 = grp * GDN_QK
        raw = _dot(h, wmain_ref[:, c0:c0 + GDN_QK])
        pad_ref[grp, halo:halo + tm, :] = raw
        cw = convw_ref[:, c0:c0 + GDN_QK]
        y = raw * cw[GDN_CONV - 1:GDN_CONV]
        for j in range(GDN_CONV - 1):
            r0 = halo - (GDN_CONV - 1) + j
            y = y + pad_ref[grp, r0:r0 + tm, :] * cw[j:j + 1]
        pad_ref[grp, 0:halo, :] = pad_ref[grp, tm:tm + halo, :]
        y = _silu(y)
        if grp < 2:
            scale = GDN_DK ** -0.5 if grp == 0 else 1.0
            for hd in range(GDN_HEADS):
                yh = y[:, hd * GDN_DK:(hd + 1) * GDN_DK]
                inv = lax.rsqrt(jnp.sum(yh * yh, axis=-1, keepdims=True) + NORM_EPS) * scale
                outs[grp][:, hd * GDN_DK:(hd + 1) * GDN_DK] = (yh * inv).astype(BF16)
        else:
            v_ref[...] = y.astype(BF16)

    c0 = 3 * GDN_QK
    z_ref[...] = _dot(h, wmain_ref[:, c0:c0 + GDN_QK]).astype(BF16)
    c0 += GDN_QK
    qs_ref[...] = _dot(h, wmain_ref[:, c0:c0 + SWA_Q]).astype(BF16)
    c0 += SWA_Q
    kvs_ref[...] = _dot(h, wmain_ref[:, c0:c0 + 2 * SWA_KV]).astype(BF16)


def _inproj(x, nw, wmain, wab, convw, alog_row, dtb_row, tm):
    B, T, D = x.shape
    tok = lambda w: pl.BlockSpec((None, tm, w), lambda b, i: (b, i, 0))
    full = lambda a: pl.BlockSpec(a.shape, lambda b, i: (0,) * a.ndim)
    sds = lambda w, dt: jax.ShapeDtypeStruct((B, T, w), dt)
    return pl.pallas_call(
        _inproj_kernel,
        grid=(B, T // tm),
        in_specs=[tok(D), full(nw), full(wmain), full(wab), full(convw), full(alog_row), full(dtb_row)],
        out_specs=[tok(GDN_QK), tok(GDN_QK), tok(GDN_QK), tok(GDN_QK), tok(LANES), tok(SWA_Q), tok(2 * SWA_KV)],
        out_shape=[sds(GDN_QK, BF16), sds(GDN_QK, BF16), sds(GDN_QK, BF16), sds(GDN_QK, BF16),
                   sds(LANES, F32), sds(SWA_Q, BF16), sds(2 * SWA_KV, BF16)],
        scratch_shapes=[pltpu.VMEM((3, tm + SUBLANES, GDN_QK), F32)],
        compiler_params=pltpu.CompilerParams(
            dimension_semantics=("arbitrary", "arbitrary"), vmem_limit_bytes=VMEM_LIMIT_BYTES),
        name="inproj",
    )(x, nw, wmain, wab, convw, alog_row, dtb_row)


def _split3(a):
    hi = a.astype(BF16)
    r1 = a - hi.astype(F32)
    mid = r1.astype(BF16)
    lo = (r1 - mid.astype(F32)).astype(BF16)
    return hi, mid, lo


def _lane_col(a, idx):
    lane = lax.broadcasted_iota(jnp.int32, a.shape, 1)
    return jnp.sum(jnp.where(lane == idx, a, 0.0), axis=-1, keepdims=True)


def _gdn_kernel(q_ref, k_ref, v_ref, z_ref, gb_ref, nw_ref, o_ref, s_ref):
    R = GDN_SUPER
    C = GDN_CHUNK
    n_chunks = R // C
    n_super = q_ref.shape[0] // R
    shift = int(math.log2(C))
    heads = range(GDN_HEADS)
    units = [(sc, hd) for sc in range(n_super) for hd in heads]

    @pl.when(pl.program_id(1) == 0)
    def _():
        s_ref[...] = jnp.zeros(s_ref.shape, F32)

    row = lax.broadcasted_iota(jnp.int32, (R, R), 0)
    col = lax.broadcasted_iota(jnp.int32, (R, R), 1)
    same = (row >> shift) == (col >> shift)
    causal = jnp.logical_and(same, row >= col)
    strict = jnp.logical_and(same, row > col)
    eye_packed = jnp.where((lax.broadcasted_iota(jnp.int32, (C, R), 1) & (C - 1))
                           == lax.broadcasted_iota(jnp.int32, (C, R), 0), 1.0, 0.0)
    sums_to = jnp.concatenate([jnp.where(jnp.logical_and(same, row <= col), 1.0, 0.0),
                               jnp.where(same, 1.0, 0.0)], axis=1).astype(BF16)

    def pack(full):
        return sum(full[j * C:(j + 1) * C] for j in range(1, n_chunks)) + full[0:C]

    def block_diag(packed):
        return jnp.where(same, jnp.concatenate([packed] * n_chunks, axis=0), jnp.zeros((R, R), BF16))

    gbs, gcum_rows, cols = [], [], []
    for sc in range(n_super):
        gb = gb_ref[sc * R:(sc + 1) * R, :]
        hi, mid, lo = (p.astype(F32)[0:SUBLANES] for p in _split3(gb.T))
        parts = jnp.concatenate([hi, mid, lo, jnp.zeros_like(hi)], axis=0).astype(BF16)
        sums = _dot(parts, sums_to)
        sums = sums[0:SUBLANES] + sums[SUBLANES:2 * SUBLANES] + sums[2 * SUBLANES:3 * SUBLANES]
        gbs.append(gb)
        gcum_rows.append(sums[:, :R])
        cols.append(jnp.concatenate([sums[:, :R], sums[:, R:], jnp.zeros((LANES - 2 * SUBLANES, R), F32)], axis=0).T)

    rows_of = lambda sc: slice(sc * R, (sc + 1) * R)
    lanes_of = lambda hd: slice(hd * GDN_DK, (hd + 1) * GDN_DK)
    wide = lambda c: jnp.broadcast_to(c, (R, GDN_DK))
    gc = [_lane_col(cols[sc], hd) for sc, hd in units]
    gl = [_lane_col(cols[sc], SUBLANES + hd) for sc, hd in units]
    beta = [_lane_col(gbs[sc], GDN_HEADS + hd) for sc, hd in units]
    n_units = range(len(units))
    decay = [jnp.exp(gc[u] - gcum_rows[sc][hd:hd + 1, :]) for u, (sc, hd) in enumerate(units)]
    qh = [q_ref[rows_of(sc), lanes_of(hd)] for sc, hd in units]
    kh = [k_ref[rows_of(sc), lanes_of(hd)] for sc, hd in units]
    kf = [kh[u].astype(F32) for u in n_units]
    kb = [kf[u] * beta[u] for u in n_units]
    eg = [jnp.exp(wide(gc[u])) for u in n_units]
    rhs = [jnp.concatenate([v_ref[rows_of(sc), lanes_of(hd)].astype(F32) * beta[u], kb[u] * eg[u]],
                           axis=1).astype(BF16) for u, (sc, hd) in enumerate(units)]

    kq = [_dot_nt(jnp.concatenate([kb[u].astype(BF16), qh[u]], axis=0), kh[u]) for u in n_units]
    a = [jnp.where(strict, kq[u][:R] * decay[u], 0.0) for u in n_units]
    qk = [jnp.where(causal, kq[u][R:] * decay[u], 0.0).astype(BF16) for u in n_units]

    a_packed = [pack(a[u]) for u in n_units]
    p = [eye_packed - a_packed[u] for u in n_units]
    xpow = [_dot(a_packed[u].astype(BF16), a[u].astype(BF16)) for u in n_units]
    for _ in range(shift - 2):
        xb = [xpow[u].astype(BF16) for u in n_units]
        y = [_dot(jnp.concatenate([p[u].astype(BF16), xb[u]], axis=0), block_diag(xb[u])) for u in n_units]
        p = [p[u] + y[u][:C] for u in n_units]
        xpow = [y[u][C:] for u in n_units]
    p = [p[u] + _dot(p[u].astype(BF16), block_diag(xpow[u].astype(BF16))) for u in n_units]

    uw = [_dot(block_diag(p[u].astype(BF16)), rhs[u]).astype(BF16) for u in n_units]
    qkuw = [_dot(qk[u], uw[u]) for u in n_units]
    q_eff = [(qh[u].astype(F32) * eg[u] - qkuw[u][:, GDN_DK:]).astype(BF16) for u in n_units]
    kd_t = [(kf[u] * jnp.exp(wide(gl[u] - gc[u]))).T.astype(BF16) for u in n_units]
    g_last = [jnp.exp(wide(gl[u])) for u in n_units]
    row_chunk = lax.broadcasted_iota(jnp.int32, (R, 2 * GDN_DK), 0) >> shift
    zero_uw = jnp.zeros((R, 2 * GDN_DK), BF16)
    nm = [[_dot(kd_t[u], jnp.where(row_chunk == c, uw[u], zero_uw)) for c in range(n_chunks)] for u in n_units]

    state = [s_ref[hd] for hd in heads]
    zero_s = jnp.zeros((GDN_DK, GDN_DK), BF16)
    for sc in range(n_super):
        o_chunks = [[] for _ in heads]
        for c in range(n_chunks):
            rows = slice(c * C, (c + 1) * C)
            for h0 in range(0, GDN_HEADS, 2):
                h1 = h0 + 1
                u0, u1 = sc * GDN_HEADS + h0, sc * GDN_HEADS + h1
                lhs = jnp.concatenate(
                    [jnp.concatenate([q_eff[u0][rows], q_eff[u1][rows]], axis=1),
                     jnp.concatenate([nm[u0][c][:, GDN_DK:], nm[u1][c][:, GDN_DK:]], axis=1).astype(BF16)], axis=0)
                s_pair = jnp.concatenate(
                    [jnp.concatenate([state[h0].astype(BF16), zero_s], axis=1),
                     jnp.concatenate([zero_s, state[h1].astype(BF16)], axis=1)], axis=0)
                r = _dot(lhs, s_pair)
                for i, (hd, u) in enumerate(((h0, u0), (h1, u1))):
                    half = slice(i * GDN_DK, (i + 1) * GDN_DK)
                    o_chunks[hd].append(r[:C, half] + qkuw[u][rows, :GDN_DK])
                    state[hd] = state[hd] * g_last[u][c * C:c * C + 1, :] - r[C:, half] + nm[u][c][:, :GDN_DK]
        for hd in heads:
            o = jnp.concatenate(o_chunks[hd], axis=0)
            o = _rms(o, nw_ref[...]) * _silu(z_ref[rows_of(sc), lanes_of(hd)].astype(F32))
            o_ref[rows_of(sc), lanes_of(hd)] = o.astype(BF16)

    for hd in heads:
        s_ref[hd] = state[hd]


def _gdn(q, k, v, z, gb, nw):
    B, T, _ = q.shape
    rows = GDN_STEP
    tok = lambda w: pl.BlockSpec((None, rows, w), lambda b, i: (b, i, 0))
    return pl.pallas_call(
        _gdn_kernel,
        grid=(B, T // rows),
        in_specs=[tok(GDN_QK), tok(GDN_QK), tok(GDN_QK), tok(GDN_QK), tok(LANES),
                  pl.BlockSpec(nw.shape, lambda b, i: (0, 0))],
        out_specs=tok(GDN_QK),
        out_shape=jax.ShapeDtypeStruct((B, T, GDN_QK), BF16),
        scratch_shapes=[pltpu.VMEM((GDN_HEADS, GDN_DK, GDN_DK), F32)],
        compiler_params=pltpu.CompilerParams(
            dimension_semantics=("arbitrary", "arbitrary"), vmem_limit_bytes=VMEM_LIMIT_BYTES),
        name="gdn",
    )(q, k, v, z, gb, nw)


def _band_tables():
    qi = np.arange(WINDOW, dtype=np.int64)[:, None]
    sj = np.arange(2 * WINDOW, dtype=np.int64)[None, :]
    dist = qi + WINDOW - sj
    in_band = (dist >= 0) & (dist < WINDOW)
    d = np.maximum(dist, 0)
    max_exact = REL_BUCKETS // 2
    ratio = np.log(np.maximum(d, 1).astype(np.float32) / np.float32(max_exact)) / np.float32(
        math.log(REL_MAX_DIST / max_exact))
    large = max_exact + (ratio.astype(np.float32) * np.float32(REL_BUCKETS - max_exact)).astype(np.int32)
    large = np.minimum(large, REL_BUCKETS - 1)
    bucket = np.where(d < max_exact, d, large).astype(np.int32)
    valid = np.stack([in_band & (sj >= WINDOW), in_band]).astype(np.int32)
    return bucket, valid


def _bias_kernel(table_ref, bucket_ref, valid_ref, o_ref):
    bucket = bucket_ref[...]
    for hd in range(SWA_Q_HEADS):
        acc = jnp.zeros(bucket.shape, F32)
        for b in range(REL_BUCKETS):
            acc = jnp.where(bucket == b, table_ref[b, hd] * LOG2_E, acc)
        for var in range(2):
            o_ref[var, hd] = jnp.where(valid_ref[var] != 0, acc, NEG_INF)


def _rel_bias(table):
    bucket, valid = _band_tables()
    return pl.pallas_call(
        _bias_kernel,
        in_specs=[pl.BlockSpec(memory_space=pltpu.SMEM),
                  pl.BlockSpec(memory_space=pltpu.VMEM), pl.BlockSpec(memory_space=pltpu.VMEM)],
        out_specs=pl.BlockSpec(memory_space=pltpu.VMEM),
        out_shape=jax.ShapeDtypeStruct((2, SWA_Q_HEADS, WINDOW, 2 * WINDOW), F32),
        name="rel_bias",
    )(table.astype(F32), jnp.asarray(bucket), jnp.asarray(valid))


def _swa_kernel(sinks_ref, qs_ref, kv_ref, bias_ref, nw_ref, o_ref):
    W = WINDOW
    T = qs_ref.shape[0]
    group = SWA_Q_HEADS // SWA_KV_HEADS
    lo_half = lax.broadcasted_iota(jnp.int32, (W, LANES), 1) < SWA_HEAD_DIM

    def block(n, carry):
        r0 = pl.multiple_of(n * W, W)
        p0 = pl.multiple_of(jnp.maximum(n - 1, 0) * W, W)
        var = jnp.minimum(n, 1)
        q = qs_ref[pl.ds(r0, W), :]
        kv = jnp.concatenate([kv_ref[pl.ds(p0, W), :], kv_ref[pl.ds(r0, W), :]], axis=0).astype(F32)
        k_pair = kv[:, :LANES]
        v_pair = kv[:, LANES:]
        k_opts = (k_pair.astype(BF16), pltpu.roll(k_pair, SWA_HEAD_DIM, axis=1).astype(BF16))
        v_opts = (v_pair.astype(BF16), pltpu.roll(v_pair, SWA_HEAD_DIM, axis=1).astype(BF16))

        pairs = []
        ss = jnp.zeros((W, 1), F32)
        for pr in range(SWA_Q_HEADS // 2):
            halves = []
            for half in range(2):
                hd = 2 * pr + half
                kvh = hd // group
                sel = 0 if half == kvh else 1
                qp = q[:, pr * LANES:(pr + 1) * LANES]
                qm = jnp.where(lo_half if half == 0 else jnp.logical_not(lo_half), qp, jnp.zeros_like(qp))
                s = _dot_nt(qm, k_opts[sel]) + bias_ref[var, hd]
                sink = sinks_ref[hd] * LOG2_E
                m = jnp.maximum(jnp.max(s, axis=-1, keepdims=True), sink)
                e = jnp.exp2(s - m)
                denom = jnp.sum(e, axis=-1, keepdims=True) + jnp.exp2(sink - m)
                halves.append(_dot(e.astype(BF16), v_opts[sel]) / denom)
            o_pair = jnp.where(lo_half, halves[0], halves[1])
            ss = ss + jnp.sum(o_pair * o_pair, axis=-1, keepdims=True)
            pairs.append(o_pair)
        inv = lax.rsqrt(ss * (1.0 / SWA_Q) + NORM_EPS)
        for pr, o_pair in enumerate(pairs):
            lanes = slice(pr * LANES, (pr + 1) * LANES)
            o_ref[pl.ds(r0, W), lanes] = (o_pair * inv * nw_ref[:, lanes]).astype(BF16)
        return carry

    lax.fori_loop(0, T // W, block, 0, unroll=SWA_UNROLL)


def _swa(sinks, qs, kvs, bias, nw):
    B, T, _ = qs.shape
    seq = lambda w: pl.BlockSpec((None, T, w), lambda b: (b, 0, 0))
    return pl.pallas_call(
        _swa_kernel,
        grid=(B,),
        in_specs=[pl.BlockSpec(memory_space=pltpu.SMEM), seq(SWA_Q), seq(2 * SWA_KV),
                  pl.BlockSpec(bias.shape, lambda b: (0, 0, 0, 0)),
                  pl.BlockSpec(nw.shape, lambda b: (0, 0))],
        out_specs=seq(SWA_Q),
        out_shape=jax.ShapeDtypeStruct((B, T, SWA_Q), BF16),
        compiler_params=pltpu.CompilerParams(
            dimension_semantics=("arbitrary",), vmem_limit_bytes=VMEM_LIMIT_BYTES),
        name="swa",
    )(sinks, qs, kvs, bias, nw)


GELU_K0 = math.sqrt(2.0 / math.pi)
GELU_K1 = 0.044715 * GELU_K0


def _gelu_tanh_x2(x):
    return x * (1.0 + jnp.tanh(x * (GELU_K0 + GELU_K1 * (x * x))))


def _mlp_kernel(x_ref, og_ref, os_ref, wout_ref, pmw_ref, pfw_ref, wgate_ref, wup_ref, convw_ref, convb_ref,
                wdown_ref, postw_ref, o_ref, pad_ref, carry_ref, act_ref):
    tm = x_ref.shape[0]
    halo = SUBLANES
    n_ff = wgate_ref.shape[0]

    @pl.when(pl.program_id(1) == 0)
    def _():
        carry_ref[...] = jnp.zeros(carry_ref.shape, F32)

    mix = _dot(og_ref[...], wout_ref[0:GDN_QK, :]) + _dot(os_ref[...], wout_ref[GDN_QK:, :])
    x1 = x_ref[...] + _rms(mix, pmw_ref[...])
    h = _rms(x1, pfw_ref[...]).astype(BF16)

    for c in range(n_ff):
        gate = _dot(h, wgate_ref[c])
        up = _dot(h, wup_ref[c])
        pad_ref[0:halo, :] = carry_ref[c]
        pad_ref[halo:halo + tm, :] = gate
        carry_ref[c] = gate[tm - halo:, :]
        cw = convw_ref[c]
        y = gate * cw[FFN_CONV - 1:FFN_CONV] + convb_ref[c]
        for j in range(FFN_CONV - 1):
            r0 = halo - (FFN_CONV - 1) + j
            y = y + pad_ref[r0:r0 + tm, :] * cw[j:j + 1]
        act_ref[:, c * FF_CHUNK:(c + 1) * FF_CHUNK] = (_gelu_tanh_x2(y) * up).astype(BF16)

    y = _dot(act_ref[...], wdown_ref[...])
    o_ref[...] = x1 + _rms(y, postw_ref[...])


def _mlp(x, og, osw, wout, pmw, pfw, wgate, wup, convw, convb, wdown, postw, tm):
    B, T, D = x.shape
    n_ff = wgate.shape[0]
    tok = lambda w: pl.BlockSpec((None, tm, w), lambda b, i: (b, i, 0))
    full = lambda a: pl.BlockSpec(a.shape, lambda b, i: (0,) * a.ndim, pipeline_mode=pl.Buffered(1))
    return pl.pallas_call(
        _mlp_kernel,
        grid=(B, T // tm),
        in_specs=[tok(D), tok(GDN_QK), tok(SWA_Q), full(wout), full(pmw), full(pfw), full(wgate), full(wup),
                  full(convw), full(convb), full(wdown), full(postw)],
        out_specs=tok(D),
        out_shape=jax.ShapeDtypeStruct((B, T, D), x.dtype),
        scratch_shapes=[pltpu.VMEM((tm + SUBLANES, FF_CHUNK), F32),
                        pltpu.VMEM((n_ff, SUBLANES, FF_CHUNK), F32),
                        pltpu.VMEM((tm, n_ff * FF_CHUNK), BF16)],
        compiler_params=pltpu.CompilerParams(
            dimension_semantics=("arbitrary", "arbitrary"), vmem_limit_bytes=VMEM_LIMIT_BYTES),
        name="mlp",
    )(x, og, osw, wout, pmw, pfw, wgate, wup, convw, convb, wdown, postw)


def _token_tile(T, largest):
    for tm in (1024, 512, 256, 128):
        if tm <= largest and T % tm == 0:
            return tm
    raise ValueError(f"sequence length {T} must be a multiple of 128")


def kernel(x, pre_mix_norm_w, w_in, gdn_conv_w, gdn_a_log, gdn_dt_bias, gdn_norm_w, swa_sinks, rel_bias_table,
           swa_norm_w, w_out, post_mix_norm_w, pre_ffn_norm_w, w_gate, w_up, ffn_conv_w, ffn_conv_b, w_down,
           post_ffn_norm_w):
    B, T, D = x.shape
    depth = w_in.shape[0]
    d_ff = w_gate.shape[-1]
    assert T % GDN_STEP == 0 and T % WINDOW == 0 and d_ff % FF_CHUNK == 0
    tm_in = _token_tile(T, INPROJ_TILE)
    tm_mlp = _token_tile(T, MLP_TILE)
    n_ff = d_ff // FF_CHUNK
    gdn_qkv = 3 * GDN_QK
    n_gate = 2 * GDN_HEADS
    ab0 = gdn_qkv + GDN_QK
    row = lambda a: a.reshape(1, -1).astype(F32)

    bias = _rel_bias(rel_bias_table)

    for l in range(depth):
        wl = w_in[l]
        qs0 = ab0 + n_gate
        wmain = jnp.concatenate([wl[:, :ab0], wl[:, qs0:qs0 + SWA_Q] * (SWA_HEAD_DIM ** -0.5 * LOG2_E),
                                 wl[:, qs0 + SWA_Q:]], axis=1).astype(BF16)
        wab = jnp.pad(wl[:, ab0:ab0 + n_gate], ((0, 0), (0, LANES - n_gate))).astype(BF16)
        alog_row = jnp.pad(row(gdn_a_log[l]), ((0, 0), (0, LANES - GDN_HEADS)))
        dtb_row = jnp.pad(row(gdn_dt_bias[l]), ((0, 0), (0, LANES - GDN_HEADS)))

        q, k, v, z, gb, qs, kvs = _inproj(x, row(pre_mix_norm_w[l]), wmain, wab, gdn_conv_w[l].astype(F32),
                                          alog_row, dtb_row, tm_in)
        o_g = _gdn(q, k, v, z, gb, row(gdn_norm_w[l]))
        o_s = _swa(swa_sinks[l].astype(F32), qs, kvs, bias, row(swa_norm_w[l]))

        chunked = lambda w: jnp.transpose(w.reshape(w.shape[0], n_ff, FF_CHUNK), (1, 0, 2))
        x = _mlp(x, o_g, o_s, w_out[l].astype(BF16), row(post_mix_norm_w[l]), row(pre_ffn_norm_w[l]),
                 chunked(w_gate[l]).astype(BF16), chunked(0.5 * w_up[l]).astype(BF16),
                 chunked(ffn_conv_w[l].astype(F32)), chunked(row(ffn_conv_b[l])),
                 w_down[l].astype(BF16), row(post_ffn_norm_w[l]), tm_mlp)
    return x
```

```python
import functools
import math

import numpy as np
import jax
import jax.numpy as jnp
from jax import lax
from jax.experimental import pallas as pl
from jax.experimental.pallas import tpu as pltpu

F32 = jnp.float32
BF16 = jnp.bfloat16

GDN_HEADS = 4
GDN_DK = 128
GDN_CONV = 4
GDN_CHUNK = 64
SWA_Q_HEADS = 8
SWA_KV_HEADS = 2
SWA_HEAD_DIM = 64
WINDOW = 128
REL_BUCKETS = 32
REL_MAX_DIST = 128
FFN_CONV = 3
NORM_EPS = 1e-6
NEG_INF = -1e30
LOG2_E = math.log2(math.e)

GDN_QK = GDN_HEADS * GDN_DK
SWA_Q = SWA_Q_HEADS * SWA_HEAD_DIM
SWA_KV = SWA_KV_HEADS * SWA_HEAD_DIM

LANES = 128
SUBLANES = 8
VMEM_LIMIT_BYTES = 56 * 1024 * 1024

GDN_SUPER = 4 * GDN_CHUNK
GDN_STEP = 4 * GDN_SUPER
FF_CHUNK = 256
INPROJ_TILE = 1024
MLP_TILE = 1024
SWA_UNROLL = 4


def _rms(x, w):
    return x * lax.rsqrt(jnp.mean(x * x, axis=-1, keepdims=True) + NORM_EPS) * w


def _sigmoid(x):
    return 1.0 / (1.0 + jnp.exp2(x * (-LOG2_E)))


def _silu(x):
    return x * _sigmoid(x)


def _dot(a, b):
    return jnp.dot(a, b, preferred_element_type=F32)


def _dot_nt(a, b):
    return lax.dot_general(a, b, (((1,), (1,)), ((), ())), preferred_element_type=F32)


def _inproj_kernel(x_ref, nw_ref, wmain_ref, wab_ref, convw_ref, alog_ref, dtb_ref,
                   q_ref, k_ref, v_ref, z_ref, gb_ref, qs_ref, kvs_ref, pad_ref):
    tm = x_ref.shape[0]
    halo = SUBLANES

    @pl.when(pl.program_id(1) == 0)
    def _():
        pad_ref[:, 0:halo, :] = jnp.zeros((3 * GDN_HEADS, halo, GDN_DK), F32)

    h = _rms(x_ref[...], nw_ref[...]).astype(BF16)

    ab = _dot(h, wab_ref[...])
    lane = lax.broadcasted_iota(jnp.int32, ab.shape, 1)
    pre = ab + dtb_ref[...]
    softplus = jnp.maximum(pre, 0.0) + jnp.log(1.0 + jnp.exp(-jnp.abs(pre)))
    g = -jnp.exp(alog_ref[...]) * softplus
    gb_ref[...] = jnp.where(lane < GDN_HEADS, g, _sigmoid(ab))

    outs = (q_ref, k_ref, v_ref)
    for grp in range(3):
        c0 = grp * GDN_QK
        raw = _dot(h, wmain_ref[:, c0:c0 + GDN_QK])
        for hd in range(GDN_HEADS):
            lanes = slice(hd * GDN_DK, (hd + 1) * GDN_DK)
            slab = grp * GDN_HEADS + hd
            pad_ref[slab, halo:halo + tm, :] = raw[:, lanes]
            cw = convw_ref[:, c0 + hd * GDN_DK:c0 + (hd + 1) * GDN_DK]
            y = raw[:, lanes] * cw[GDN_CONV - 1:GDN_CONV]
            for j in range(GDN_CONV - 1):
                r0 = halo - (GDN_CONV - 1) + j
                y = y + pad_ref[slab, r0:r0 + tm, :] * cw[j:j + 1]
            pad_ref[slab, 0:halo, :] = pad_ref[slab, tm:tm + halo, :]
            y = _silu(y)
            if grp < 2:
                scale = GDN_DK ** -0.5 if grp == 0 else 1.0
                y = y * (lax.rsqrt(jnp.sum(y * y, axis=-1, keepdims=True) + NORM_EPS) * scale)
            outs[grp][:, lanes] = y.astype(BF16)

    c0 = 3 * GDN_QK
    z_ref[...] = _dot(h, wmain_ref[:, c0:c0 + GDN_QK]).astype(BF16)
    c0 += GDN_QK
    qs_ref[...] = _dot(h, wmain_ref[:, c0:c0 + SWA_Q]).astype(BF16)
    c0 += SWA_Q
    kvs_ref[...] = _dot(h, wmain_ref[:, c0:c0 + 2 * SWA_KV]).astype(BF16)


def _inproj(x, nw, wmain, wab, convw, alog_row, dtb_row, tm):
    B, T, D = x.shape
    tok = lambda w: pl.BlockSpec((None, tm, w), lambda b, i: (b, i, 0))
    full = lambda a: pl.BlockSpec(a.shape, lambda b, i: (0,) * a.ndim)
    sds = lambda w, dt: jax.ShapeDtypeStruct((B, T, w), dt)
    return pl.pallas_call(
        _inproj_kernel,
        grid=(B, T // tm),
        in_specs=[tok(D), full(nw), full(wmain), full(wab), full(convw), full(alog_row), full(dtb_row)],
        out_specs=[tok(GDN_QK), tok(GDN_QK), tok(GDN_QK), tok(GDN_QK), tok(LANES), tok(SWA_Q), tok(2 * SWA_KV)],
        out_shape=[sds(GDN_QK, BF16), sds(GDN_QK, BF16), sds(GDN_QK, BF16), sds(GDN_QK, BF16),
                   sds(LANES, F32), sds(SWA_Q, BF16), sds(2 * SWA_KV, BF16)],
        scratch_shapes=[pltpu.VMEM((3 * GDN_HEADS, tm + SUBLANES, GDN_DK), F32)],
        compiler_params=pltpu.CompilerParams(
            dimension_semantics=("arbitrary", "arbitrary"), vmem_limit_bytes=VMEM_LIMIT_BYTES),
        name="inproj",
    )(x, nw, wmain, wab, convw, alog_row, dtb_row)


def _split3(a):
    hi = a.astype(BF16)
    r1 = a - hi.astype(F32)
    mid = r1.astype(BF16)
    lo = (r1 - mid.astype(F32)).astype(BF16)
    return hi, mid, lo


def _lane_col(a, idx):
    lane = lax.broadcasted_iota(jnp.int32, a.shape, 1)
    return jnp.sum(jnp.where(lane == idx, a, 0.0), axis=-1, keepdims=True)


def _gdn_kernel(q_ref, k_ref, v_ref, z_ref, gb_ref, nw_ref, o_ref, s_ref):
    R = GDN_SUPER
    C = GDN_CHUNK
    n_chunks = R // C
    n_super = q_ref.shape[0] // R
    shift = int(math.log2(C))
    heads = range(GDN_HEADS)
    units = [(sc, hd) for sc in range(n_super) for hd in heads]

    @pl.when(pl.program_id(1) == 0)
    def _():
        s_ref[...] = jnp.zeros(s_ref.shape, F32)

    row = lax.broadcasted_iota(jnp.int32, (R, R), 0)
    col = lax.broadcasted_iota(jnp.int32, (R, R), 1)
    same = (row >> shift) == (col >> shift)
    causal = jnp.logical_and(same, row >= col)
    strict = jnp.logical_and(same, row > col)
    eye_packed = jnp.where((lax.broadcasted_iota(jnp.int32, (C, R), 1) & (C - 1))
                           == lax.broadcasted_iota(jnp.int32, (C, R), 0), 1.0, 0.0)
    sums_to = jnp.concatenate([jnp.where(jnp.logical_and(same, row <= col), 1.0, 0.0),
                               jnp.where(same, 1.0, 0.0)], axis=1).astype(BF16)

    def pack(full):
        return sum(full[j * C:(j + 1) * C] for j in range(1, n_chunks)) + full[0:C]

    def block_diag(packed):
        return jnp.where(same, jnp.concatenate([packed] * n_chunks, axis=0), jnp.zeros((R, R), BF16))

    gbs, gcum_rows, cols = [], [], []
    for sc in range(n_super):
        gb = gb_ref[sc * R:(sc + 1) * R, :]
        hi, mid, lo = (p.astype(F32)[0:SUBLANES] for p in _split3(gb.T))
        parts = jnp.concatenate([hi, mid, lo, jnp.zeros_like(hi)], axis=0).astype(BF16)
        sums = _dot(parts, sums_to)
        sums = sums[0:SUBLANES] + sums[SUBLANES:2 * SUBLANES] + sums[2 * SUBLANES:3 * SUBLANES]
        gbs.append(gb)
        gcum_rows.append(sums[:, :R])
        cols.append(jnp.concatenate([sums[:, :R], sums[:, R:], jnp.zeros((LANES - 2 * SUBLANES, R), F32)], axis=0).T)

    rows_of = lambda sc: slice(sc * R, (sc + 1) * R)
    lanes_of = lambda hd: slice(hd * GDN_DK, (hd + 1) * GDN_DK)
    wide = lambda c: jnp.broadcast_to(c, (R, GDN_DK))
    gc = [_lane_col(cols[sc], hd) for sc, hd in units]
    gl = [_lane_col(cols[sc], SUBLANES + hd) for sc, hd in units]
    beta = [_lane_col(gbs[sc], GDN_HEADS + hd) for sc, hd in units]
    n_units = range(len(units))
    decay = [jnp.exp(gc[u] - gcum_rows[sc][hd:hd + 1, :]) for u, (sc, hd) in enumerate(units)]
    qh = [q_ref[rows_of(sc), lanes_of(hd)] for sc, hd in units]
    kh = [k_ref[rows_of(sc), lanes_of(hd)] for sc, hd in units]
    kf = [kh[u].astype(F32) for u in n_units]
    kb = [kf[u] * beta[u] for u in n_units]
    eg = [jnp.exp(wide(gc[u])) for u in n_units]
    rhs = [jnp.concatenate([v_ref[rows_of(sc), lanes_of(hd)].astype(F32) * beta[u], kb[u] * eg[u]],
                           axis=1).astype(BF16) for u, (sc, hd) in enumerate(units)]

    kq = [_dot_nt(jnp.concatenate([kb[u].astype(BF16), qh[u]], axis=0), kh[u]) for u in n_units]
    a = [jnp.where(strict, kq[u][:R] * decay[u], 0.0) for u in n_units]
    qk = [jnp.where(causal, kq[u][R:] * decay[u], 0.0).astype(BF16) for u in n_units]

    a_packed = [pack(a[u]) for u in n_units]
    p = [eye_packed - a_packed[u] for u in n_units]
    xpow = [_dot(a_packed[u].astype(BF16), a[u].astype(BF16)) for u in n_units]
    for _ in range(shift - 2):
        xb = [xpow[u].astype(BF16) for u in n_units]
        y = [_dot(jnp.concatenate([p[u].astype(BF16), xb[u]], axis=0), block_diag(xb[u])) for u in n_units]
        p = [p[u] + y[u][:C] for u in n_units]
        xpow = [y[u][C:] for u in n_units]
    p = [p[u] + _dot(p[u].astype(BF16), block_diag(xpow[u].astype(BF16))) for u in n_units]

    uw = [_dot(block_diag(p[u].astype(BF16)), rhs[u]).astype(BF16) for u in n_units]
    qkuw = [_dot(qk[u], uw[u]) for u in n_units]
    q_eff = [(qh[u].astype(F32) * eg[u] - qkuw[u][:, GDN_DK:]).astype(BF16) for u in n_units]
    kd_t = [(kf[u] * jnp.exp(wide(gl[u] - gc[u]))).T.astype(BF16) for u in n_units]
    g_last = [jnp.exp(wide(gl[u])) for u in n_units]
    row_chunk = lax.broadcasted_iota(jnp.int32, (R, 2 * GDN_DK), 0) >> shift
    zero_uw = jnp.zeros((R, 2 * GDN_DK), BF16)
    nm = [[_dot(kd_t[u], jnp.where(row_chunk == c, uw[u], zero_uw)) for c in range(n_chunks)] for u in n_units]

    state = [s_ref[hd] for hd in heads]
    zero_s = jnp.zeros((GDN_DK, GDN_DK), BF16)
    for sc in range(n_super):
        o_chunks = [[] for _ in heads]
        for c in range(n_chunks):
            rows = slice(c * C, (c + 1) * C)
            for h0 in range(0, GDN_HEADS, 2):
                h1 = h0 + 1
                u0, u1 = sc * GDN_HEADS + h0, sc * GDN_HEADS + h1
                lhs = jnp.concatenate(
                    [jnp.concatenate([q_eff[u0][rows], q_eff[u1][rows]], axis=1),
                     jnp.concatenate([nm[u0][c][:, GDN_DK:], nm[u1][c][:, GDN_DK:]], axis=1).astype(BF16)], axis=0)
                s_pair = jnp.concatenate(
                    [jnp.concatenate([state[h0].astype(BF16), zero_s], axis=1),
                     jnp.concatenate([zero_s, state[h1].astype(BF16)], axis=1)], axis=0)
                r = _dot(lhs, s_pair)
                for i, (hd, u) in enumerate(((h0, u0), (h1, u1))):
                    half = slice(i * GDN_DK, (i + 1) * GDN_DK)
                    o_chunks[hd].append(r[:C, half] + qkuw[u][rows, :GDN_DK])
                    state[hd] = state[hd] * g_last[u][c * C:c * C + 1, :] - r[C:, half] + nm[u][c][:, :GDN_DK]
        for hd in heads:
            o = jnp.concatenate(o_chunks[hd], axis=0)
            o = _rms(o, nw_ref[...]) * _silu(z_ref[rows_of(sc), lanes_of(hd)].astype(F32))
            o_ref[rows_of(sc), lanes_of(hd)] = o.astype(BF16)

    for hd in heads:
        s_ref[hd] = state[hd]


def _gdn(q, k, v, z, gb, nw):
    B, T, _ = q.shape
    rows = GDN_STEP
    tok = lambda w: pl.BlockSpec((None, rows, w), lambda b, i: (b, i, 0))
    return pl.pallas_call(
        _gdn_kernel,
        grid=(B, T // rows),
        in_specs=[tok(GDN_QK), tok(GDN_QK), tok(GDN_QK), tok(GDN_QK), tok(LANES),
                  pl.BlockSpec(nw.shape, lambda b, i: (0, 0))],
        out_specs=tok(GDN_QK),
        out_shape=jax.ShapeDtypeStruct((B, T, GDN_QK), BF16),
        scratch_shapes=[pltpu.VMEM((GDN_HEADS, GDN_DK, GDN_DK), F32)],
        compiler_params=pltpu.CompilerParams(
            dimension_semantics=("arbitrary", "arbitrary"), vmem_limit_bytes=VMEM_LIMIT_BYTES),
        name="gdn",
    )(q, k, v, z, gb, nw)


def _band_tables():
    qi = np.arange(WINDOW, dtype=np.int64)[:, None]
    sj = np.arange(2 * WINDOW, dtype=np.int64)[None, :]
    dist = qi + WINDOW - sj
    in_band = (dist >= 0) & (dist < WINDOW)
    d = np.maximum(dist, 0)
    max_exact = REL_BUCKETS // 2
    ratio = np.log(np.maximum(d, 1).astype(np.float32) / np.float32(max_exact)) / np.float32(
        math.log(REL_MAX_DIST / max_exact))
    large = max_exact + (ratio.astype(np.float32) * np.float32(REL_BUCKETS - max_exact)).astype(np.int32)
    large = np.minimum(large, REL_BUCKETS - 1)
    bucket = np.where(d < max_exact, d, large).astype(np.int32)
    valid = np.stack([in_band & (sj >= WINDOW), in_band]).astype(np.int32)
    return bucket, valid


def _bias_kernel(table_ref, bucket_ref, valid_ref, o_ref):
    bucket = bucket_ref[...]
    for hd in range(SWA_Q_HEADS):
        acc = jnp.zeros(bucket.shape, F32)
        for b in range(REL_BUCKETS):
            acc = jnp.where(bucket == b, table_ref[b, hd] * LOG2_E, acc)
        for var in range(2):
            o_ref[var, hd] = jnp.where(valid_ref[var] != 0, acc, NEG_INF)


def _rel_bias(table):
    bucket, valid = _band_tables()
    return pl.pallas_call(
        _bias_kernel,
        in_specs=[pl.BlockSpec(memory_space=pltpu.SMEM),
                  pl.BlockSpec(memory_space=pltpu.VMEM), pl.BlockSpec(memory_space=pltpu.VMEM)],
        out_specs=pl.BlockSpec(memory_space=pltpu.VMEM),
        out_shape=jax.ShapeDtypeStruct((2, SWA_Q_HEADS, WINDOW, 2 * WINDOW), F32),
        name="rel_bias",
    )(table.astype(F32), jnp.asarray(bucket), jnp.asarray(valid))


def _swa_kernel(sinks_ref, qs_ref, kv_ref, bias_ref, nw_ref, o_ref):
    W = WINDOW
    T = qs_ref.shape[0]
    group = SWA_Q_HEADS // SWA_KV_HEADS
    lo_half = lax.broadcasted_iota(jnp.int32, (W, LANES), 1) < SWA_HEAD_DIM

    def block(n, carry):
        r0 = pl.multiple_of(n * W, W)
        p0 = pl.multiple_of(jnp.maximum(n - 1, 0) * W, W)
        var = jnp.minimum(n, 1)
        q = qs_ref[pl.ds(r0, W), :]
        kv = jnp.concatenate([kv_ref[pl.ds(p0, W), :], kv_ref[pl.ds(r0, W), :]], axis=0).astype(F32)
        k_pair = kv[:, :LANES]
        v_pair = kv[:, LANES:]
        k_opts = (k_pair.astype(BF16), pltpu.roll(k_pair, SWA_HEAD_DIM, axis=1).astype(BF16))
        v_opts = (v_pair.astype(BF16), pltpu.roll(v_pair, SWA_HEAD_DIM, axis=1).astype(BF16))

        pairs = []
        ss = jnp.zeros((W, 1), F32)
        for pr in range(SWA_Q_HEADS // 2):
            halves = []
            for half in range(2):
                hd = 2 * pr + half
                kvh = hd // group
                sel = 0 if half == kvh else 1
                qp = q[:, pr * LANES:(pr + 1) * LANES]
                qm = jnp.where(lo_half if half == 0 else jnp.logical_not(lo_half), qp, jnp.zeros_like(qp))
                s = _dot_nt(qm, k_opts[sel]) + bias_ref[var, hd]
                sink = sinks_ref[hd] * LOG2_E
                m = jnp.maximum(jnp.max(s, axis=-1, keepdims=True), sink)
                e = jnp.exp2(s - m)
                denom = jnp.sum(e, axis=-1, keepdims=True) + jnp.exp2(sink - m)
                halves.append(_dot(e.astype(BF16), v_opts[sel]) / denom)
            o_pair = jnp.where(lo_half, halves[0], halves[1])
            ss = ss + jnp.sum(o_pair * o_pair, axis=-1, keepdims=True)
            pairs.append(o_pair)
        inv = lax.rsqrt(ss * (1.0 / SWA_Q) + NORM_EPS)
        for pr, o_pair in enumerate(pairs):
            lanes = slice(pr * LANES, (pr + 1) * LANES)
            o_ref[pl.ds(r0, W), lanes] = (o_pair * inv * nw_ref[:, lanes]).astype(BF16)
        return carry

    lax.fori_loop(0, T // W, block, 0, unroll=SWA_UNROLL)


def _swa(sinks, qs, kvs, bias, nw):
    B, T, _ = qs.shape
    seq = lambda w: pl.BlockSpec((None, T, w), lambda b: (b, 0, 0))
    return pl.pallas_call(
        _swa_kernel,
        grid=(B,),
        in_specs=[pl.BlockSpec(memory_space=pltpu.SMEM), seq(SWA_Q), seq(2 * SWA_KV),
                  pl.BlockSpec(bias.shape, lambda b: (0, 0, 0, 0)),
                  pl.BlockSpec(nw.shape, lambda b: (0, 0))],
        out_specs=seq(SWA_Q),
        out_shape=jax.ShapeDtypeStruct((B, T, SWA_Q), BF16),
        compiler_params=pltpu.CompilerParams(
            dimension_semantics=("arbitrary",), vmem_limit_bytes=VMEM_LIMIT_BYTES),
        name="swa",
    )(sinks, qs, kvs, bias, nw)


GELU_K0 = math.sqrt(2.0 / math.pi)
GELU_K1 = 0.044715 * GELU_K0


def _gelu_tanh_x2(x):
    return x * (1.0 + jnp.tanh(x * (GELU_K0 + GELU_K1 * (x * x))))


def _mlp_kernel(x_ref, og_ref, os_ref, wout_ref, pmw_ref, pfw_ref, wgate_ref, wup_ref, convw_ref, convb_ref,
                wdown_ref, postw_ref, o_ref, pad_ref, carry_ref, act_ref):
    tm = x_ref.shape[0]
    halo = SUBLANES
    n_ff = wgate_ref.shape[1] // FF_CHUNK

    @pl.when(pl.program_id(1) == 0)
    def _():
        carry_ref[...] = jnp.zeros(carry_ref.shape, F32)

    mix = _dot(og_ref[...], wout_ref[0:GDN_QK, :]) + _dot(os_ref[...], wout_ref[GDN_QK:, :])
    x1 = x_ref[...] + _rms(mix, pmw_ref[...])
    h = _rms(x1, pfw_ref[...]).astype(BF16)

    for c in range(n_ff):
        cols = slice(c * FF_CHUNK, (c + 1) * FF_CHUNK)
        gate = _dot(h, wgate_ref[:, cols])
        up = _dot(h, wup_ref[:, cols])
        cw = convw_ref[:, cols]
        bias = convb_ref[:, cols]
        for half in range(FF_CHUNK // LANES):
            lanes = slice(half * LANES, (half + 1) * LANES)
            pad_ref[half, 0:halo, :] = carry_ref[c, :, lanes]
            pad_ref[half, halo:halo + tm, :] = gate[:, lanes]
            y = gate[:, lanes] * cw[FFN_CONV - 1:FFN_CONV, lanes] + bias[:, lanes]
            for j in range(FFN_CONV - 1):
                r0 = halo - (FFN_CONV - 1) + j
                y = y + pad_ref[half, r0:r0 + tm, :] * cw[j:j + 1, lanes]
            act_ref[:, c * FF_CHUNK + half * LANES:c * FF_CHUNK + (half + 1) * LANES] = (
                _gelu_tanh_x2(y) * up[:, lanes]).astype(BF16)
        carry_ref[c] = gate[tm - halo:, :]

    y = _dot(act_ref[...], wdown_ref[...])
    o_ref[...] = x1 + _rms(y, postw_ref[...])


def _mlp(x, og, osw, wout, pmw, pfw, wgate, wup, convw, convb, wdown, postw, tm):
    B, T, D = x.shape
    n_ff = wgate.shape[1] // FF_CHUNK
    tok = lambda w: pl.BlockSpec((None, tm, w), lambda b, i: (b, i, 0))
    full = lambda a: pl.BlockSpec(a.shape, lambda b, i: (0,) * a.ndim, pipeline_mode=pl.Buffered(1))
    return pl.pallas_call(
        _mlp_kernel,
        grid=(B, T // tm),
        in_specs=[tok(D), tok(GDN_QK), tok(SWA_Q), full(wout), full(pmw), full(pfw), full(wgate), full(wup),
                  full(convw), full(convb), full(wdown), full(postw)],
        out_specs=tok(D),
        out_shape=jax.ShapeDtypeStruct((B, T, D), x.dtype),
        scratch_shapes=[pltpu.VMEM((FF_CHUNK // LANES, tm + SUBLANES, LANES), F32),
                        pltpu.VMEM((n_ff, SUBLANES, FF_CHUNK), F32),
                        pltpu.VMEM((tm, n_ff * FF_CHUNK), BF16)],
        compiler_params=pltpu.CompilerParams(
            dimension_semantics=("arbitrary", "arbitrary"), vmem_limit_bytes=VMEM_LIMIT_BYTES),
        name="mlp",
    )(x, og, osw, wout, pmw, pfw, wgate, wup, convw, convb, wdown, postw)


def _token_tile(T, largest):
    for tm in (1024, 512, 256, 128):
        if tm <= largest and T % tm == 0:
            return tm
    raise ValueError(f"sequence length {T} must be a multiple of 128")


def kernel(x, pre_mix_norm_w, w_in, gdn_conv_w, gdn_a_log, gdn_dt_bias, gdn_norm_w, swa_sinks, rel_bias_table,
           swa_norm_w, w_out, post_mix_norm_w, pre_ffn_norm_w, w_gate, w_up, ffn_conv_w, ffn_conv_b, w_down,
           post_ffn_norm_w):
    B, T, D = x.shape
    depth = w_in.shape[0]
    d_ff = w_gate.shape[-1]
    assert T % GDN_STEP == 0 and T % WINDOW == 0 and d_ff % FF_CHUNK == 0
    tm_in = _token_tile(T, INPROJ_TILE)
    tm_mlp = _token_tile(T, MLP_TILE)
    gdn_qkv = 3 * GDN_QK
    n_gate = 2 * GDN_HEADS
    ab0 = gdn_qkv + GDN_QK
    row = lambda a: a.reshape(1, -1).astype(F32)

    bias = _rel_bias(rel_bias_table)

    for l in range(depth):
        wl = w_in[l]
        qs0 = ab0 + n_gate
        wmain = jnp.concatenate([wl[:, :ab0], wl[:, qs0:qs0 + SWA_Q] * (SWA_HEAD_DIM ** -0.5 * LOG2_E),
                                 wl[:, qs0 + SWA_Q:]], axis=1).astype(BF16)
        wab = jnp.pad(wl[:, ab0:ab0 + n_gate], ((0, 0), (0, LANES - n_gate))).astype(BF16)
        alog_row = jnp.pad(row(gdn_a_log[l]), ((0, 0), (0, LANES - GDN_HEADS)))
        dtb_row = jnp.pad(row(gdn_dt_bias[l]), ((0, 0), (0, LANES - GDN_HEADS)))

        q, k, v, z, gb, qs, kvs = _inproj(x, row(pre_mix_norm_w[l]), wmain, wab, gdn_conv_w[l].astype(F32),
                                          alog_row, dtb_row, tm_in)
        o_g = _gdn(q, k, v, z, gb, row(gdn_norm_w[l]))
        o_s = _swa(swa_sinks[l].astype(F32), qs, kvs, bias, row(swa_norm_w[l]))

        x = _mlp(x, o_g, o_s, w_out[l].astype(BF16), row(post_mix_norm_w[l]), row(pre_ffn_norm_w[l]),
                 w_gate[l].astype(BF16), (0.5 * w_up[l]).astype(BF16), ffn_conv_w[l].astype(F32), row(ffn_conv_b[l]),
                 w_down[l].astype(BF16), row(post_ffn_norm_w[l]), tm_mlp)
    return x
```

```python
import functools
import math

import numpy as np
import jax
import jax.numpy as jnp
from jax import lax
from jax.experimental import pallas as pl
from jax.experimental.pallas import tpu as pltpu

F32 = jnp.float32
BF16 = jnp.bfloat16

GDN_HEADS = 4
GDN_DK = 128
GDN_CONV = 4
GDN_CHUNK = 64
SWA_Q_HEADS = 8
SWA_KV_HEADS = 2
SWA_HEAD_DIM = 64
WINDOW = 128
REL_BUCKETS = 32
REL_MAX_DIST = 128
FFN_CONV = 3
NORM_EPS = 1e-6
NEG_INF = -1e30
LOG2_E = math.log2(math.e)

GDN_QK = GDN_HEADS * GDN_DK
SWA_Q = SWA_Q_HEADS * SWA_HEAD_DIM
SWA_KV = SWA_KV_HEADS * SWA_HEAD_DIM

LANES = 128
SUBLANES = 8
VMEM_LIMIT_BYTES = 56 * 1024 * 1024

GDN_SUPER = 4 * GDN_CHUNK
GDN_STEP = 4 * GDN_SUPER
FF_CHUNK = 256
INPROJ_TILE = 1024
MLP_TILE = 1024
SWA_UNROLL = 4


def _rms(x, w):
    return x * lax.rsqrt(jnp.mean(x * x, axis=-1, keepdims=True) + NORM_EPS) * w


def _sigmoid(x):
    return 1.0 / (1.0 + jnp.exp2(x * (-LOG2_E)))


def _silu(x):
    return x * _sigmoid(x)


def _dot(a, b):
    return jnp.dot(a, b, preferred_element_type=F32)


def _dot_nt(a, b):
    return lax.dot_general(a, b, (((1,), (1,)), ((), ())), preferred_element_type=F32)


def _inproj_kernel(x_ref, nw_ref, wmain_ref, wab_ref, convw_ref, alog_ref, dtb_ref,
                   q_ref, k_ref, v_ref, z_ref, gb_ref, qs_ref, kvs_ref, pad_ref):
    tm = x_ref.shape[0]
    halo = SUBLANES

    @pl.when(pl.program_id(1) == 0)
    def _():
        pad_ref[:, 0:halo, :] = jnp.zeros((3 * GDN_HEADS, halo, GDN_DK), F32)

    h = _rms(x_ref[...], nw_ref[...]).astype(BF16)

    ab = _dot(h, wab_ref[...])
    lane = lax.broadcasted_iota(jnp.int32, ab.shape, 1)
    pre = ab + dtb_ref[...]
    softplus = jnp.maximum(pre, 0.0) + jnp.log(1.0 + jnp.exp(-jnp.abs(pre)))
    g = -jnp.exp(alog_ref[...]) * softplus
    gb_ref[...] = jnp.where(lane < GDN_HEADS, g, _sigmoid(ab))

    outs = (q_ref, k_ref, v_ref)
    for grp in range(3):
        c0 = grp * GDN_QK
        raw = _dot(h, wmain_ref[:, c0:c0 + GDN_QK])
        for hd in range(GDN_HEADS):
            lanes = slice(hd * GDN_DK, (hd + 1) * GDN_DK)
            slab = grp * GDN_HEADS + hd
            pad_ref[slab, halo:halo + tm, :] = raw[:, lanes]
            cw = convw_ref[:, c0 + hd * GDN_DK:c0 + (hd + 1) * GDN_DK]
            y = raw[:, lanes] * cw[GDN_CONV - 1:GDN_CONV]
            for j in range(GDN_CONV - 1):
                r0 = halo - (GDN_CONV - 1) + j
                y = y + pad_ref[slab, r0:r0 + tm, :] * cw[j:j + 1]
            pad_ref[slab, 0:halo, :] = pad_ref[slab, tm:tm + halo, :]
            y = _silu(y)
            if grp < 2:
                scale = GDN_DK ** -0.5 if grp == 0 else 1.0
                y = y * (lax.rsqrt(jnp.sum(y * y, axis=-1, keepdims=True) + NORM_EPS) * scale)
            outs[grp][:, lanes] = y.astype(BF16)

    c0 = 3 * GDN_QK
    z_ref[...] = _dot(h, wmain_ref[:, c0:c0 + GDN_QK]).astype(BF16)
    c0 += GDN_QK
    qs_ref[...] = _dot(h, wmain_ref[:, c0:c0 + SWA_Q]).astype(BF16)
    c0 += SWA_Q
    kvs_ref[...] = _dot(h, wmain_ref[:, c0:c0 + 2 * SWA_KV]).astype(BF16)


def _inproj(x, nw, wmain, wab, convw, alog_row, dtb_row, tm):
    B, T, D = x.shape
    tok = lambda w: pl.BlockSpec((None, tm, w), lambda b, i: (b, i, 0))
    full = lambda a: pl.BlockSpec(a.shape, lambda b, i: (0,) * a.ndim)
    sds = lambda w, dt: jax.ShapeDtypeStruct((B, T, w), dt)
    return pl.pallas_call(
        _inproj_kernel,
        grid=(B, T // tm),
        in_specs=[tok(D), full(nw), full(wmain), full(wab), full(convw), full(alog_row), full(dtb_row)],
        out_specs=[tok(GDN_QK), tok(GDN_QK), tok(GDN_QK), tok(GDN_QK), tok(LANES), tok(SWA_Q), tok(2 * SWA_KV)],
        out_shape=[sds(GDN_QK, BF16), sds(GDN_QK, BF16), sds(GDN_QK, BF16), sds(GDN_QK, BF16),
                   sds(LANES, F32), sds(SWA_Q, BF16), sds(2 * SWA_KV, BF16)],
        scratch_shapes=[pltpu.VMEM((3 * GDN_HEADS, tm + SUBLANES, GDN_DK), F32)],
        compiler_params=pltpu.CompilerParams(
            dimension_semantics=("arbitrary", "arbitrary"), vmem_limit_bytes=VMEM_LIMIT_BYTES),
        name="inproj",
    )(x, nw, wmain, wab, convw, alog_row, dtb_row)


def _split3(a):
    hi = a.astype(BF16)
    r1 = a - hi.astype(F32)
    mid = r1.astype(BF16)
    lo = (r1 - mid.astype(F32)).astype(BF16)
    return hi, mid, lo


def _lane_col(a, idx):
    lane = lax.broadcasted_iota(jnp.int32, a.shape, 1)
    return jnp.sum(jnp.where(lane == idx, a, 0.0), axis=-1, keepdims=True)


def _gdn_kernel(q_ref, k_ref, v_ref, z_ref, gb_ref, nw_ref, o_ref, s_ref):
    R = GDN_SUPER
    C = GDN_CHUNK
    n_chunks = R // C
    n_super = q_ref.shape[0] // R
    shift = int(math.log2(C))
    heads = range(GDN_HEADS)
    units = [(sc, hd) for sc in range(n_super) for hd in heads]

    @pl.when(pl.program_id(1) == 0)
    def _():
        s_ref[...] = jnp.zeros(s_ref.shape, F32)

    row = lax.broadcasted_iota(jnp.int32, (R, R), 0)
    col = lax.broadcasted_iota(jnp.int32, (R, R), 1)
    same = (row >> shift) == (col >> shift)
    causal = jnp.logical_and(same, row >= col)
    strict = jnp.logical_and(same, row > col)
    eye_packed = jnp.where((lax.broadcasted_iota(jnp.int32, (C, R), 1) & (C - 1))
                           == lax.broadcasted_iota(jnp.int32, (C, R), 0), 1.0, 0.0)
    sums_to = jnp.concatenate([jnp.where(jnp.logical_and(same, row <= col), 1.0, 0.0),
                               jnp.where(same, 1.0, 0.0)], axis=1).astype(BF16)

    def pack(full):
        return sum(full[j * C:(j + 1) * C] for j in range(1, n_chunks)) + full[0:C]

    def block_diag(packed):
        return jnp.where(same, jnp.concatenate([packed] * n_chunks, axis=0), jnp.zeros((R, R), BF16))

    gbs, gcum_rows, cols = [], [], []
    for sc in range(n_super):
        gb = gb_ref[sc * R:(sc + 1) * R, :]
        hi, mid, lo = (p.astype(F32)[0:SUBLANES] for p in _split3(gb.T))
        parts = jnp.concatenate([hi, mid, lo, jnp.zeros_like(hi)], axis=0).astype(BF16)
        sums = _dot(parts, sums_to)
        sums = sums[0:SUBLANES] + sums[SUBLANES:2 * SUBLANES] + sums[2 * SUBLANES:3 * SUBLANES]
        gbs.append(gb)
        gcum_rows.append(sums[:, :R])
        cols.append(jnp.concatenate([sums[:, :R], sums[:, R:], jnp.zeros((LANES - 2 * SUBLANES, R), F32)], axis=0).T)

    rows_of = lambda sc: slice(sc * R, (sc + 1) * R)
    lanes_of = lambda hd: slice(hd * GDN_DK, (hd + 1) * GDN_DK)
    wide = lambda c: jnp.broadcast_to(c, (R, GDN_DK))
    gc = [_lane_col(cols[sc], hd) for sc, hd in units]
    gl = [_lane_col(cols[sc], SUBLANES + hd) for sc, hd in units]
    beta = [_lane_col(gbs[sc], GDN_HEADS + hd) for sc, hd in units]
    n_units = range(len(units))
    decay = [jnp.exp(gc[u] - gcum_rows[sc][hd:hd + 1, :]) for u, (sc, hd) in enumerate(units)]
    qh = [q_ref[rows_of(sc), lanes_of(hd)] for sc, hd in units]
    kh = [k_ref[rows_of(sc), lanes_of(hd)] for sc, hd in units]
    kf = [kh[u].astype(F32) for u in n_units]
    kb = [kf[u] * beta[u] for u in n_units]
    eg = [jnp.exp(wide(gc[u])) for u in n_units]
    rhs = [jnp.concatenate([v_ref[rows_of(sc), lanes_of(hd)].astype(F32) * beta[u], kb[u] * eg[u]],
                           axis=1).astype(BF16) for u, (sc, hd) in enumerate(units)]

    kq = [_dot_nt(jnp.concatenate([kb[u].astype(BF16), qh[u]], axis=0), kh[u]) for u in n_units]
    a = [jnp.where(strict, kq[u][:R] * decay[u], 0.0) for u in n_units]
    qk = [jnp.where(causal, kq[u][R:] * decay[u], 0.0).astype(BF16) for u in n_units]

    a_packed = [pack(a[u]) for u in n_units]
    p = [eye_packed - a_packed[u] for u in n_units]
    xpow = [_dot(a_packed[u].astype(BF16), a[u].astype(BF16)) for u in n_units]
    for _ in range(shift - 2):
        xb = [xpow[u].astype(BF16) for u in n_units]
        y = [_dot(jnp.concatenate([p[u].astype(BF16), xb[u]], axis=0), block_diag(xb[u])) for u in n_units]
        p = [p[u] + y[u][:C] for u in n_units]
        xpow = [y[u][C:] for u in n_units]
    p = [p[u] + _dot(p[u].astype(BF16), block_diag(xpow[u].astype(BF16))) for u in n_units]

    uw = [_dot(block_diag(p[u].astype(BF16)), rhs[u]).astype(BF16) for u in n_units]
    qkuw = [_dot(qk[u], uw[u]) for u in n_units]
    q_eff = [(qh[u].astype(F32) * eg[u] - qkuw[u][:, GDN_DK:]).astype(BF16) for u in n_units]
    kd_t = [(kf[u] * jnp.exp(wide(gl[u] - gc[u]))).T.astype(BF16) for u in n_units]
    g_last = [jnp.exp(wide(gl[u])) for u in n_units]
    row_chunk = lax.broadcasted_iota(jnp.int32, (R, 2 * GDN_DK), 0) >> shift
    zero_uw = jnp.zeros((R, 2 * GDN_DK), BF16)
    nm = [[_dot(kd_t[u], jnp.where(row_chunk == c, uw[u], zero_uw)) for c in range(n_chunks)] for u in n_units]

    state = [s_ref[hd] for hd in heads]
    zero_s = jnp.zeros((GDN_DK, GDN_DK), BF16)
    for sc in range(n_super):
        o_chunks = [[] for _ in heads]
        for c in range(n_chunks):
            rows = slice(c * C, (c + 1) * C)
            for h0 in range(0, GDN_HEADS, 2):
                h1 = h0 + 1
                u0, u1 = sc * GDN_HEADS + h0, sc * GDN_HEADS + h1
                lhs = jnp.concatenate(
                    [jnp.concatenate([q_eff[u0][rows], q_eff[u1][rows]], axis=1),
                     jnp.concatenate([nm[u0][c][:, GDN_DK:], nm[u1][c][:, GDN_DK:]], axis=1).astype(BF16)], axis=0)
                s_pair = jnp.concatenate(
                    [jnp.concatenate([state[h0].astype(BF16), zero_s], axis=1),
                     jnp.concatenate([zero_s, state[h1].astype(BF16)], axis=1)], axis=0)
                r = _dot(lhs, s_pair)
                for i, (hd, u) in enumerate(((h0, u0), (h1, u1))):
                    half = slice(i * GDN_DK, (i + 1) * GDN_DK)
                    o_chunks[hd].append(r[:C, half] + qkuw[u][rows, :GDN_DK])
                    state[hd] = state[hd] * g_last[u][c * C:c * C + 1, :] - r[C:, half] + nm[u][c][:, :GDN_DK]
        for hd in heads:
            o = jnp.concatenate(o_chunks[hd], axis=0)
            o = _rms(o, nw_ref[...]) * _silu(z_ref[rows_of(sc), lanes_of(hd)].astype(F32))
            o_ref[rows_of(sc), lanes_of(hd)] = o.astype(BF16)

    for hd in heads:
        s_ref[hd] = state[hd]


def _gdn(q, k, v, z, gb, nw):
    B, T, _ = q.shape
    rows = GDN_STEP
    tok = lambda w: pl.BlockSpec((None, rows, w), lambda b, i: (b, i, 0))
    return pl.pallas_call(
        _gdn_kernel,
        grid=(B, T // rows),
        in_specs=[tok(GDN_QK), tok(GDN_QK), tok(GDN_QK), tok(GDN_QK), tok(LANES),
                  pl.BlockSpec(nw.shape, lambda b, i: (0, 0))],
        out_specs=tok(GDN_QK),
        out_shape=jax.ShapeDtypeStruct((B, T, GDN_QK), BF16),
        scratch_shapes=[pltpu.VMEM((GDN_HEADS, GDN_DK, GDN_DK), F32)],
        compiler_params=pltpu.CompilerParams(
            dimension_semantics=("arbitrary", "arbitrary"), vmem_limit_bytes=VMEM_LIMIT_BYTES),
        name="gdn",
    )(q, k, v, z, gb, nw)


def _band_tables():
    qi = np.arange(WINDOW, dtype=np.int64)[:, None]
    sj = np.arange(2 * WINDOW, dtype=np.int64)[None, :]
    dist = qi + WINDOW - sj
    in_band = (dist >= 0) & (dist < WINDOW)
    d = np.maximum(dist, 0)
    max_exact = REL_BUCKETS // 2
    ratio = np.log(np.maximum(d, 1).astype(np.float32) / np.float32(max_exact)) / np.float32(
        math.log(REL_MAX_DIST / max_exact))
    large = max_exact + (ratio.astype(np.float32) * np.float32(REL_BUCKETS - max_exact)).astype(np.int32)
    large = np.minimum(large, REL_BUCKETS - 1)
    bucket = np.where(d < max_exact, d, large).astype(np.int32)
    valid = np.stack([in_band & (sj >= WINDOW), in_band]).astype(np.int32)
    return bucket, valid


def _bias_kernel(table_ref, bucket_ref, valid_ref, o_ref):
    bucket = bucket_ref[...]
    for hd in range(SWA_Q_HEADS):
        acc = jnp.zeros(bucket.shape, F32)
        for b in range(REL_BUCKETS):
            acc = jnp.where(bucket == b, table_ref[b, hd] * LOG2_E, acc)
        for var in range(2):
            o_ref[var, hd] = jnp.where(valid_ref[var] != 0, acc, NEG_INF)


def _rel_bias(table):
    bucket, valid = _band_tables()
    return pl.pallas_call(
        _bias_kernel,
        in_specs=[pl.BlockSpec(memory_space=pltpu.SMEM),
                  pl.BlockSpec(memory_space=pltpu.VMEM), pl.BlockSpec(memory_space=pltpu.VMEM)],
        out_specs=pl.BlockSpec(memory_space=pltpu.VMEM),
        out_shape=jax.ShapeDtypeStruct((2, SWA_Q_HEADS, WINDOW, 2 * WINDOW), F32),
        name="rel_bias",
    )(table.astype(F32), jnp.asarray(bucket), jnp.asarray(valid))


def _swa_kernel(sinks_ref, qs_ref, kv_ref, bias_ref, nw_ref, o_ref):
    W = WINDOW
    T = qs_ref.shape[0]
    group = SWA_Q_HEADS // SWA_KV_HEADS
    lo_half = lax.broadcasted_iota(jnp.int32, (W, LANES), 1) < SWA_HEAD_DIM

    def block(n, carry):
        r0 = pl.multiple_of(n * W, W)
        p0 = pl.multiple_of(jnp.maximum(n - 1, 0) * W, W)
        var = jnp.minimum(n, 1)
        q = qs_ref[pl.ds(r0, W), :]
        kv = jnp.concatenate([kv_ref[pl.ds(p0, W), :], kv_ref[pl.ds(r0, W), :]], axis=0).astype(F32)
        k_pair = kv[:, :LANES]
        v_pair = kv[:, LANES:]
        k_opts = (k_pair.astype(BF16), pltpu.roll(k_pair, SWA_HEAD_DIM, axis=1).astype(BF16))
        v_opts = (v_pair.astype(BF16), pltpu.roll(v_pair, SWA_HEAD_DIM, axis=1).astype(BF16))

        pairs = []
        ss = jnp.zeros((W, 1), F32)
        for pr in range(SWA_Q_HEADS // 2):
            halves = []
            for half in range(2):
                hd = 2 * pr + half
                kvh = hd // group
                sel = 0 if half == kvh else 1
                qp = q[:, pr * LANES:(pr + 1) * LANES]
                qm = jnp.where(lo_half if half == 0 else jnp.logical_not(lo_half), qp, jnp.zeros_like(qp))
                s = _dot_nt(qm, k_opts[sel]) + bias_ref[var, hd]
                sink = sinks_ref[hd] * LOG2_E
                m = jnp.maximum(jnp.max(s, axis=-1, keepdims=True), sink)
                e = jnp.exp2(s - m)
                denom = jnp.sum(e, axis=-1, keepdims=True) + jnp.exp2(sink - m)
                halves.append(_dot(e.astype(BF16), v_opts[sel]) / denom)
            o_pair = jnp.where(lo_half, halves[0], halves[1])
            ss = ss + jnp.sum(o_pair * o_pair, axis=-1, keepdims=True)
            pairs.append(o_pair)
        inv = lax.rsqrt(ss * (1.0 / SWA_Q) + NORM_EPS)
        for pr, o_pair in enumerate(pairs):
            lanes = slice(pr * LANES, (pr + 1) * LANES)
            o_ref[pl.ds(r0, W), lanes] = (o_pair * inv * nw_ref[:, lanes]).astype(BF16)
        return carry

    lax.fori_loop(0, T // W, block, 0, unroll=SWA_UNROLL)


def _swa(sinks, qs, kvs, bias, nw):
    B, T, _ = qs.shape
    seq = lambda w: pl.BlockSpec((None, T, w), lambda b: (b, 0, 0))
    return pl.pallas_call(
        _swa_kernel,
        grid=(B,),
        in_specs=[pl.BlockSpec(memory_space=pltpu.SMEM), seq(SWA_Q), seq(2 * SWA_KV),
                  pl.BlockSpec(bias.shape, lambda b: (0, 0, 0, 0)),
                  pl.BlockSpec(nw.shape, lambda b: (0, 0))],
        out_specs=seq(SWA_Q),
        out_shape=jax.ShapeDtypeStruct((B, T, SWA_Q), BF16),
        compiler_params=pltpu.CompilerParams(
            dimension_semantics=("arbitrary",), vmem_limit_bytes=VMEM_LIMIT_BYTES),
        name="swa",
    )(sinks, qs, kvs, bias, nw)


GELU_K0 = math.sqrt(2.0 / math.pi)
GELU_K1 = 0.044715 * GELU_K0


def _gelu_tanh_x2(x):
    return x * (1.0 + jnp.tanh(x * (GELU_K0 + GELU_K1 * (x * x))))


def _mlp_kernel(x_ref, og_ref, os_ref, wout_ref, pmw_ref, pfw_ref, wgate_ref, wup_ref, convw_ref, convb_ref,
                wdown_ref, postw_ref, o_ref, pad_ref, carry_ref, act_ref):
    tm = x_ref.shape[0]
    halo = SUBLANES
    n_ff = wgate_ref.shape[0]

    @pl.when(pl.program_id(1) == 0)
    def _():
        carry_ref[...] = jnp.zeros(carry_ref.shape, F32)

    mix = _dot(og_ref[...], wout_ref[0:GDN_QK, :]) + _dot(os_ref[...], wout_ref[GDN_QK:, :])
    x1 = x_ref[...] + _rms(mix, pmw_ref[...])
    h = _rms(x1, pfw_ref[...]).astype(BF16)

    for c in range(n_ff):
        gate = _dot(h, wgate_ref[c])
        up = _dot(h, wup_ref[c])
        cw = convw_ref[c]
        bias = convb_ref[c]
        for half in range(FF_CHUNK // LANES):
            lanes = slice(half * LANES, (half + 1) * LANES)
            pad_ref[half, 0:halo, :] = carry_ref[c, :, lanes]
            pad_ref[half, halo:halo + tm, :] = gate[:, lanes]
            y = gate[:, lanes] * cw[FFN_CONV - 1:FFN_CONV, lanes] + bias[:, lanes]
            for j in range(FFN_CONV - 1):
                r0 = halo - (FFN_CONV - 1) + j
                y = y + pad_ref[half, r0:r0 + tm, :] * cw[j:j + 1, lanes]
            act_ref[:, c * FF_CHUNK + half * LANES:c * FF_CHUNK + (half + 1) * LANES] = (
                _gelu_tanh_x2(y) * up[:, lanes]).astype(BF16)
        carry_ref[c] = gate[tm - halo:, :]

    y = _dot(act_ref[...], wdown_ref[...])
    o_ref[...] = x1 + _rms(y, postw_ref[...])


def _mlp(x, og, osw, wout, pmw, pfw, wgate, wup, convw, convb, wdown, postw, tm):
    B, T, D = x.shape
    n_ff = wgate.shape[0]
    tok = lambda w: pl.BlockSpec((None, tm, w), lambda b, i: (b, i, 0))
    full = lambda a: pl.BlockSpec(a.shape, lambda b, i: (0,) * a.ndim, pipeline_mode=pl.Buffered(1))
    return pl.pallas_call(
        _mlp_kernel,
        grid=(B, T // tm),
        in_specs=[tok(D), tok(GDN_QK), tok(SWA_Q), full(wout), full(pmw), full(pfw), full(wgate), full(wup),
                  full(convw), full(convb), full(wdown), full(postw)],
        out_specs=tok(D),
        out_shape=jax.ShapeDtypeStruct((B, T, D), x.dtype),
        scratch_shapes=[pltpu.VMEM((FF_CHUNK // LANES, tm + SUBLANES, LANES), F32),
                        pltpu.VMEM((n_ff, SUBLANES, FF_CHUNK), F32),
                        pltpu.VMEM((tm, n_ff * FF_CHUNK), BF16)],
        compiler_params=pltpu.CompilerParams(
            dimension_semantics=("arbitrary", "arbitrary"), vmem_limit_bytes=VMEM_LIMIT_BYTES),
        name="mlp",
    )(x, og, osw, wout, pmw, pfw, wgate, wup, convw, convb, wdown, postw)


def _token_tile(T, largest):
    for tm in (1024, 512, 256, 128):
        if tm <= largest and T % tm == 0:
            return tm
    raise ValueError(f"sequence length {T} must be a multiple of 128")


def kernel(x, pre_mix_norm_w, w_in, gdn_conv_w, gdn_a_log, gdn_dt_bias, gdn_norm_w, swa_sinks, rel_bias_table,
           swa_norm_w, w_out, post_mix_norm_w, pre_ffn_norm_w, w_gate, w_up, ffn_conv_w, ffn_conv_b, w_down,
           post_ffn_norm_w):
    B, T, D = x.shape
    depth = w_in.shape[0]
    d_ff = w_gate.shape[-1]
    assert T % GDN_STEP == 0 and T % WINDOW == 0 and d_ff % FF_CHUNK == 0
    tm_in = _token_tile(T, INPROJ_TILE)
    tm_mlp = _token_tile(T, MLP_TILE)
    gdn_qkv = 3 * GDN_QK
    n_gate = 2 * GDN_HEADS
    ab0 = gdn_qkv + GDN_QK
    row = lambda a: a.reshape(1, -1).astype(F32)

    bias = _rel_bias(rel_bias_table)

    for l in range(depth):
        wl = w_in[l]
        qs0 = ab0 + n_gate
        wmain = jnp.concatenate([wl[:, :ab0], wl[:, qs0:qs0 + SWA_Q] * (SWA_HEAD_DIM ** -0.5 * LOG2_E),
                                 wl[:, qs0 + SWA_Q:]], axis=1).astype(BF16)
        wab = jnp.pad(wl[:, ab0:ab0 + n_gate], ((0, 0), (0, LANES - n_gate))).astype(BF16)
        alog_row = jnp.pad(row(gdn_a_log[l]), ((0, 0), (0, LANES - GDN_HEADS)))
        dtb_row = jnp.pad(row(gdn_dt_bias[l]), ((0, 0), (0, LANES - GDN_HEADS)))

        q, k, v, z, gb, qs, kvs = _inproj(x, row(pre_mix_norm_w[l]), wmain, wab, gdn_conv_w[l].astype(F32),
                                          alog_row, dtb_row, tm_in)
        o_g = _gdn(q, k, v, z, gb, row(gdn_norm_w[l]))
        o_s = _swa(swa_sinks[l].astype(F32), qs, kvs, bias, row(swa_norm_w[l]))

        chunked = lambda w: jnp.transpose(w.reshape(w.shape[0], d_ff // FF_CHUNK, FF_CHUNK), (1, 0, 2))
        x = _mlp(x, o_g, o_s, w_out[l].astype(BF16), row(post_mix_norm_w[l]), row(pre_ffn_norm_w[l]),
                 chunked(w_gate[l]).astype(BF16), chunked(0.5 * w_up[l]).astype(BF16),
                 chunked(ffn_conv_w[l].astype(F32)), chunked(row(ffn_conv_b[l])),
                 w_down[l].astype(BF16), row(post_ffn_norm_w[l]), tm_mlp)
    return x
```

```python
import functools
import math

import numpy as np
import jax
import jax.numpy as jnp
from jax import lax
from jax.experimental import pallas as pl
from jax.experimental.pallas import tpu as pltpu

F32 = jnp.float32
BF16 = jnp.bfloat16

GDN_HEADS = 4
GDN_DK = 128
GDN_CONV = 4
GDN_CHUNK = 64
SWA_Q_HEADS = 8
SWA_KV_HEADS = 2
SWA_HEAD_DIM = 64
WINDOW = 128
REL_BUCKETS = 32
REL_MAX_DIST = 128
FFN_CONV = 3
NORM_EPS = 1e-6
NEG_INF = -1e30
LOG2_E = math.log2(math.e)

GDN_QK = GDN_HEADS * GDN_DK
SWA_Q = SWA_Q_HEADS * SWA_HEAD_DIM
SWA_KV = SWA_KV_HEADS * SWA_HEAD_DIM

LANES = 128
SUBLANES = 8
VMEM_LIMIT_BYTES = 56 * 1024 * 1024

GDN_SUPER = 4 * GDN_CHUNK
GDN_STEP = 4 * GDN_SUPER
FF_CHUNK = 256
INPROJ_TILE = 1024
MLP_TILE = 1024
SWA_UNROLL = 4


def _rms(x, w):
    return x * lax.rsqrt(jnp.mean(x * x, axis=-1, keepdims=True) + NORM_EPS) * w


def _sigmoid(x):
    return 1.0 / (1.0 + jnp.exp2(x * (-LOG2_E)))


def _silu(x):
    return x * _sigmoid(x)


def _dot(a, b):
    return jnp.dot(a, b, preferred_element_type=F32)


def _dot_nt(a, b):
    return lax.dot_general(a, b, (((1,), (1,)), ((), ())), preferred_element_type=F32)


def _inproj_kernel(x_ref, nw_ref, wmain_ref, wab_ref, convw_ref, alog_ref, dtb_ref,
                   q_ref, k_ref, v_ref, z_ref, gb_ref, qs_ref, kvs_ref, pad_ref):
    tm = x_ref.shape[0]
    halo = SUBLANES

    @pl.when(pl.program_id(1) == 0)
    def _():
        pad_ref[:, 0:halo, :] = jnp.zeros((3 * GDN_HEADS, halo, GDN_DK), F32)

    h = _rms(x_ref[...], nw_ref[...]).astype(BF16)

    ab = _dot(h, wab_ref[...])
    lane = lax.broadcasted_iota(jnp.int32, ab.shape, 1)
    pre = ab + dtb_ref[...]
    softplus = jnp.maximum(pre, 0.0) + jnp.log(1.0 + jnp.exp(-jnp.abs(pre)))
    g = -jnp.exp(alog_ref[...]) * softplus
    gb_ref[...] = jnp.where(lane < GDN_HEADS, g, _sigmoid(ab))

    outs = (q_ref, k_ref, v_ref)
    for grp in range(3):
        c0 = grp * GDN_QK
        raw = _dot(h, wmain_ref[:, c0:c0 + GDN_QK])
        for hd in range(GDN_HEADS):
            lanes = slice(hd * GDN_DK, (hd + 1) * GDN_DK)
            slab = grp * GDN_HEADS + hd
            pad_ref[slab, halo:halo + tm, :] = raw[:, lanes]
            cw = convw_ref[:, c0 + hd * GDN_DK:c0 + (hd + 1) * GDN_DK]
            y = raw[:, lanes] * cw[GDN_CONV - 1:GDN_CONV]
            for j in range(GDN_CONV - 1):
                r0 = halo - (GDN_CONV - 1) + j
                y = y + pad_ref[slab, r0:r0 + tm, :] * cw[j:j + 1]
            pad_ref[slab, 0:halo, :] = pad_ref[slab, tm:tm + halo, :]
            y = _silu(y)
            if grp < 2:
                scale = GDN_DK ** -0.5 if grp == 0 else 1.0
                y = y * (lax.rsqrt(jnp.sum(y * y, axis=-1, keepdims=True) + NORM_EPS) * scale)
            outs[grp][:, lanes] = y.astype(BF16)

    c0 = 3 * GDN_QK
    z_ref[...] = _dot(h, wmain_ref[:, c0:c0 + GDN_QK]).astype(BF16)
    c0 += GDN_QK
    qs_ref[...] = _dot(h, wmain_ref[:, c0:c0 + SWA_Q]).astype(BF16)
    c0 += SWA_Q
    kvs_ref[...] = _dot(h, wmain_ref[:, c0:c0 + 2 * SWA_KV]).astype(BF16)


def _inproj(x, nw, wmain, wab, convw, alog_row, dtb_row, tm):
    B, T, D = x.shape
    tok = lambda w: pl.BlockSpec((None, tm, w), lambda b, i: (b, i, 0))
    full = lambda a: pl.BlockSpec(a.shape, lambda b, i: (0,) * a.ndim)
    sds = lambda w, dt: jax.ShapeDtypeStruct((B, T, w), dt)
    return pl.pallas_call(
        _inproj_kernel,
        grid=(B, T // tm),
        in_specs=[tok(D), full(nw), full(wmain), full(wab), full(convw), full(alog_row), full(dtb_row)],
        out_specs=[tok(GDN_QK), tok(GDN_QK), tok(GDN_QK), tok(GDN_QK), tok(LANES), tok(SWA_Q), tok(2 * SWA_KV)],
        out_shape=[sds(GDN_QK, BF16), sds(GDN_QK, BF16), sds(GDN_QK, BF16), sds(GDN_QK, BF16),
                   sds(LANES, F32), sds(SWA_Q, BF16), sds(2 * SWA_KV, BF16)],
        scratch_shapes=[pltpu.VMEM((3 * GDN_HEADS, tm + SUBLANES, GDN_DK), F32)],
        compiler_params=pltpu.CompilerParams(
            dimension_semantics=("arbitrary", "arbitrary"), vmem_limit_bytes=VMEM_LIMIT_BYTES),
        name="inproj",
    )(x, nw, wmain, wab, convw, alog_row, dtb_row)


def _split3(a):
    hi = a.astype(BF16)
    r1 = a - hi.astype(F32)
    mid = r1.astype(BF16)
    lo = (r1 - mid.astype(F32)).astype(BF16)
    return hi, mid, lo


def _lane_col(a, idx):
    lane = lax.broadcasted_iota(jnp.int32, a.shape, 1)
    return jnp.sum(jnp.where(lane == idx, a, 0.0), axis=-1, keepdims=True)


def _gdn_kernel(q_ref, k_ref, v_ref, z_ref, gb_ref, nw_ref, o_ref, s_ref):
    R = GDN_SUPER
    C = GDN_CHUNK
    n_chunks = R // C
    n_super = q_ref.shape[0] // R
    shift = int(math.log2(C))
    heads = range(GDN_HEADS)
    units = [(sc, hd) for sc in range(n_super) for hd in heads]

    @pl.when(pl.program_id(1) == 0)
    def _():
        s_ref[...] = jnp.zeros(s_ref.shape, F32)

    row = lax.broadcasted_iota(jnp.int32, (R, R), 0)
    col = lax.broadcasted_iota(jnp.int32, (R, R), 1)
    same = (row >> shift) == (col >> shift)
    causal = jnp.logical_and(same, row >= col)
    strict = jnp.logical_and(same, row > col)
    eye_packed = jnp.where((lax.broadcasted_iota(jnp.int32, (C, R), 1) & (C - 1))
                           == lax.broadcasted_iota(jnp.int32, (C, R), 0), 1.0, 0.0)
    sums_to = jnp.concatenate([jnp.where(jnp.logical_and(same, row <= col), 1.0, 0.0),
                               jnp.where(same, 1.0, 0.0)], axis=1).astype(BF16)

    def pack(full):
        return sum(full[j * C:(j + 1) * C] for j in range(1, n_chunks)) + full[0:C]

    def block_diag(packed):
        return jnp.where(same, jnp.concatenate([packed] * n_chunks, axis=0), jnp.zeros((R, R), BF16))

    gbs, gcum_rows, cols = [], [], []
    for sc in range(n_super):
        gb = gb_ref[sc * R:(sc + 1) * R, :]
        hi, mid, lo = (p.astype(F32)[0:SUBLANES] for p in _split3(gb.T))
        parts = jnp.concatenate([hi, mid, lo, jnp.zeros_like(hi)], axis=0).astype(BF16)
        sums = _dot(parts, sums_to)
        sums = sums[0:SUBLANES] + sums[SUBLANES:2 * SUBLANES] + sums[2 * SUBLANES:3 * SUBLANES]
        gbs.append(gb)
        gcum_rows.append(sums[:, :R])
        cols.append(jnp.concatenate([sums[:, :R], sums[:, R:], jnp.zeros((LANES - 2 * SUBLANES, R), F32)], axis=0).T)

    rows_of = lambda sc: slice(sc * R, (sc + 1) * R)
    lanes_of = lambda hd: slice(hd * GDN_DK, (hd + 1) * GDN_DK)
    wide = lambda c: jnp.broadcast_to(c, (R, GDN_DK))
    gc = [_lane_col(cols[sc], hd) for sc, hd in units]
    gl = [_lane_col(cols[sc], SUBLANES + hd) for sc, hd in units]
    beta = [_lane_col(gbs[sc], GDN_HEADS + hd) for sc, hd in units]
    n_units = range(len(units))
    decay = [jnp.exp(gc[u] - gcum_rows[sc][hd:hd + 1, :]) for u, (sc, hd) in enumerate(units)]
    qh = [q_ref[rows_of(sc), lanes_of(hd)] for sc, hd in units]
    kh = [k_ref[rows_of(sc), lanes_of(hd)] for sc, hd in units]
    kf = [kh[u].astype(F32) for u in n_units]
    kb = [kf[u] * beta[u] for u in n_units]
    eg = [jnp.exp(wide(gc[u])) for u in n_units]
    rhs = [jnp.concatenate([v_ref[rows_of(sc), lanes_of(hd)].astype(F32) * beta[u], kb[u] * eg[u]],
                           axis=1).astype(BF16) for u, (sc, hd) in enumerate(units)]

    kq = [_dot_nt(jnp.concatenate([kb[u].astype(BF16), qh[u]], axis=0), kh[u]) for u in n_units]
    a = [jnp.where(strict, kq[u][:R] * decay[u], 0.0) for u in n_units]
    qk = [jnp.where(causal, kq[u][R:] * decay[u], 0.0).astype(BF16) for u in n_units]

    a_packed = [pack(a[u]) for u in n_units]
    p = [eye_packed - a_packed[u] for u in n_units]
    xpow = [_dot(a_packed[u].astype(BF16), a[u].astype(BF16)) for u in n_units]
    for _ in range(shift - 2):
        xb = [xpow[u].astype(BF16) for u in n_units]
        y = [_dot(jnp.concatenate([p[u].astype(BF16), xb[u]], axis=0), block_diag(xb[u])) for u in n_units]
        p = [p[u] + y[u][:C] for u in n_units]
        xpow = [y[u][C:] for u in n_units]
    p = [p[u] + _dot(p[u].astype(BF16), block_diag(xpow[u].astype(BF16))) for u in n_units]

    uw = [_dot(block_diag(p[u].astype(BF16)), rhs[u]).astype(BF16) for u in n_units]
    qkuw = [_dot(qk[u], uw[u]) for u in n_units]
    q_eff = [(qh[u].astype(F32) * eg[u] - qkuw[u][:, GDN_DK:]).astype(BF16) for u in n_units]
    kd_t = [(kf[u] * jnp.exp(wide(gl[u] - gc[u]))).T.astype(BF16) for u in n_units]
    g_last = [jnp.exp(wide(gl[u])) for u in n_units]
    row_chunk = lax.broadcasted_iota(jnp.int32, (R, 2 * GDN_DK), 0) >> shift
    zero_uw = jnp.zeros((R, 2 * GDN_DK), BF16)
    nm = [[_dot(kd_t[u], jnp.where(row_chunk == c, uw[u], zero_uw)) for c in range(n_chunks)] for u in n_units]

    state = [s_ref[hd] for hd in heads]
    zero_s = jnp.zeros((GDN_DK, GDN_DK), BF16)
    for sc in range(n_super):
        o_chunks = [[] for _ in heads]
        for c in range(n_chunks):
            rows = slice(c * C, (c + 1) * C)
            for h0 in range(0, GDN_HEADS, 2):
                h1 = h0 + 1
                u0, u1 = sc * GDN_HEADS + h0, sc * GDN_HEADS + h1
                lhs = jnp.concatenate(
                    [jnp.concatenate([q_eff[u0][rows], q_eff[u1][rows]], axis=1),
                     jnp.concatenate([nm[u0][c][:, GDN_DK:], nm[u1][c][:, GDN_DK:]], axis=1).astype(BF16)], axis=0)
                s_pair = jnp.concatenate(
                    [jnp.concatenate([state[h0].astype(BF16), zero_s], axis=1),
                     jnp.concatenate([zero_s, state[h1].astype(BF16)], axis=1)], axis=0)
                r = _dot(lhs, s_pair)
                for i, (hd, u) in enumerate(((h0, u0), (h1, u1))):
                    half = slice(i * GDN_DK, (i + 1) * GDN_DK)
                    o_chunks[hd].append(r[:C, half] + qkuw[u][rows, :GDN_DK])
                    state[hd] = state[hd] * g_last[u][c * C:c * C + 1, :] - r[C:, half] + nm[u][c][:, :GDN_DK]
        for hd in heads:
            o = jnp.concatenate(o_chunks[hd], axis=0)
            o = _rms(o, nw_ref[...]) * _silu(z_ref[rows_of(sc), lanes_of(hd)].astype(F32))
            o_ref[rows_of(sc), lanes_of(hd)] = o.astype(BF16)

    for hd in heads:
        s_ref[hd] = state[hd]


def _gdn(q, k, v, z, gb, nw):
    B, T, _ = q.shape
    rows = GDN_STEP
    tok = lambda w: pl.BlockSpec((None, rows, w), lambda b, i: (b, i, 0))
    return pl.pallas_call(
        _gdn_kernel,
        grid=(B, T // rows),
        in_specs=[tok(GDN_QK), tok(GDN_QK), tok(GDN_QK), tok(GDN_QK), tok(LANES),
                  pl.BlockSpec(nw.shape, lambda b, i: (0, 0))],
        out_specs=tok(GDN_QK),
        out_shape=jax.ShapeDtypeStruct((B, T, GDN_QK), BF16),
        scratch_shapes=[pltpu.VMEM((GDN_HEADS, GDN_DK, GDN_DK), F32)],
        compiler_params=pltpu.CompilerParams(
            dimension_semantics=("arbitrary", "arbitrary"), vmem_limit_bytes=VMEM_LIMIT_BYTES),
        name="gdn",
    )(q, k, v, z, gb, nw)


def _band_tables():
    qi = np.arange(WINDOW, dtype=np.int64)[:, None]
    sj = np.arange(2 * WINDOW, dtype=np.int64)[None, :]
    dist = qi + WINDOW - sj
    in_band = (dist >= 0) & (dist < WINDOW)
    d = np.maximum(dist, 0)
    max_exact = REL_BUCKETS // 2
    ratio = np.log(np.maximum(d, 1).astype(np.float32) / np.float32(max_exact)) / np.float32(
        math.log(REL_MAX_DIST / max_exact))
    large = max_exact + (ratio.astype(np.float32) * np.float32(REL_BUCKETS - max_exact)).astype(np.int32)
    large = np.minimum(large, REL_BUCKETS - 1)
    bucket = np.where(d < max_exact, d, large).astype(np.int32)
    valid = np.stack([in_band & (sj >= WINDOW), in_band]).astype(np.int32)
    return bucket, valid


def _bias_kernel(table_ref, bucket_ref, valid_ref, o_ref):
    bucket = bucket_ref[...]
    for hd in range(SWA_Q_HEADS):
        acc = jnp.zeros(bucket.shape, F32)
        for b in range(REL_BUCKETS):
            acc = jnp.where(bucket == b, table_ref[b, hd] * LOG2_E, acc)
        for var in range(2):
            o_ref[var, hd] = jnp.where(valid_ref[var] != 0, acc, NEG_INF)


def _rel_bias(table):
    bucket, valid = _band_tables()
    return pl.pallas_call(
        _bias_kernel,
        in_specs=[pl.BlockSpec(memory_space=pltpu.SMEM),
                  pl.BlockSpec(memory_space=pltpu.VMEM), pl.BlockSpec(memory_space=pltpu.VMEM)],
        out_specs=pl.BlockSpec(memory_space=pltpu.VMEM),
        out_shape=jax.ShapeDtypeStruct((2, SWA_Q_HEADS, WINDOW, 2 * WINDOW), F32),
        name="rel_bias",
    )(table.astype(F32), jnp.asarray(bucket), jnp.asarray(valid))


def _swa_kernel(sinks_ref, qs_ref, kv_ref, bias_ref, nw_ref, o_ref):
    W = WINDOW
    T = qs_ref.shape[0]
    group = SWA_Q_HEADS // SWA_KV_HEADS
    lo_half = lax.broadcasted_iota(jnp.int32, (W, LANES), 1) < SWA_HEAD_DIM

    def block(n, carry):
        r0 = pl.multiple_of(n * W, W)
        p0 = pl.multiple_of(jnp.maximum(n - 1, 0) * W, W)
        var = jnp.minimum(n, 1)
        q = qs_ref[pl.ds(r0, W), :]
        kv = jnp.concatenate([kv_ref[pl.ds(p0, W), :], kv_ref[pl.ds(r0, W), :]], axis=0).astype(F32)
        k_pair = kv[:, :LANES]
        v_pair = kv[:, LANES:]
        k_opts = (k_pair.astype(BF16), pltpu.roll(k_pair, SWA_HEAD_DIM, axis=1).astype(BF16))
        v_opts = (v_pair.astype(BF16), pltpu.roll(v_pair, SWA_HEAD_DIM, axis=1).astype(BF16))

        pairs = []
        ss = jnp.zeros((W, 1), F32)
        for pr in range(SWA_Q_HEADS // 2):
            halves = []
            for half in range(2):
                hd = 2 * pr + half
                kvh = hd // group
                sel = 0 if half == kvh else 1
                qp = q[:, pr * LANES:(pr + 1) * LANES]
                qm = jnp.where(lo_half if half == 0 else jnp.logical_not(lo_half), qp, jnp.zeros_like(qp))
                s = _dot_nt(qm, k_opts[sel]) + bias_ref[var, hd]
                sink = sinks_ref[hd] * LOG2_E
                m = jnp.maximum(jnp.max(s, axis=-1, keepdims=True), sink)
                e = jnp.exp2(s - m)
                denom = jnp.sum(e, axis=-1, keepdims=True) + jnp.exp2(sink - m)
                halves.append(_dot(e.astype(BF16), v_opts[sel]) / denom)
            o_pair = jnp.where(lo_half, halves[0], halves[1])
            ss = ss + jnp.sum(o_pair * o_pair, axis=-1, keepdims=True)
            pairs.append(o_pair)
        inv = lax.rsqrt(ss * (1.0 / SWA_Q) + NORM_EPS)
        for pr, o_pair in enumerate(pairs):
            lanes = slice(pr * LANES, (pr + 1) * LANES)
            o_ref[pl.ds(r0, W), lanes] = (o_pair * inv * nw_ref[:, lanes]).astype(BF16)
        return carry

    lax.fori_loop(0, T // W, block, 0, unroll=SWA_UNROLL)


def _swa(sinks, qs, kvs, bias, nw):
    B, T, _ = qs.shape
    seq = lambda w: pl.BlockSpec((None, T, w), lambda b: (b, 0, 0))
    return pl.pallas_call(
        _swa_kernel,
        grid=(B,),
        in_specs=[pl.BlockSpec(memory_space=pltpu.SMEM), seq(SWA_Q), seq(2 * SWA_KV),
                  pl.BlockSpec(bias.shape, lambda b: (0, 0, 0, 0)),
                  pl.BlockSpec(nw.shape, lambda b: (0, 0))],
        out_specs=seq(SWA_Q),
        out_shape=jax.ShapeDtypeStruct((B, T, SWA_Q), BF16),
        compiler_params=pltpu.CompilerParams(
            dimension_semantics=("arbitrary",), vmem_limit_bytes=VMEM_LIMIT_BYTES),
        name="swa",
    )(sinks, qs, kvs, bias, nw)


GELU_K0 = math.sqrt(2.0 / math.pi)
GELU_K1 = 0.044715 * GELU_K0


def _gelu_tanh_x2(x):
    return x * (1.0 + jnp.tanh(x * (GELU_K0 + GELU_K1 * (x * x))))


def _mlp_kernel(x_ref, og_ref, os_ref, wout_ref, pmw_ref, pfw_ref, wgate_ref, wup_ref, convw_ref, convb_ref,
                wdown_ref, postw_ref, o_ref, pad_ref, carry_ref, act_ref):
    tm = x_ref.shape[0]
    halo = SUBLANES
    n_ff = wgate_ref.shape[1] // FF_CHUNK

    @pl.when(pl.program_id(1) == 0)
    def _():
        carry_ref[...] = jnp.zeros(carry_ref.shape, F32)

    mix = _dot(og_ref[...], wout_ref[0:GDN_QK, :]) + _dot(os_ref[...], wout_ref[GDN_QK:, :])
    x1 = x_ref[...] + _rms(mix, pmw_ref[...])
    h = _rms(x1, pfw_ref[...]).astype(BF16)

    for c in range(n_ff):
        cols = slice(c * FF_CHUNK, (c + 1) * FF_CHUNK)
        gate = _dot(h, wgate_ref[:, cols])
        up = _dot(h, wup_ref[:, cols])
        pad_ref[0:halo, :] = carry_ref[c]
        pad_ref[halo:halo + tm, :] = gate
        carry_ref[c] = gate[tm - halo:, :]
        cw = convw_ref[:, cols]
        y = gate * cw[FFN_CONV - 1:FFN_CONV] + convb_ref[:, cols]
        for j in range(FFN_CONV - 1):
            r0 = halo - (FFN_CONV - 1) + j
            y = y + pad_ref[r0:r0 + tm, :] * cw[j:j + 1]
        act_ref[:, cols] = (_gelu_tanh_x2(y) * up).astype(BF16)

    y = _dot(act_ref[...], wdown_ref[...])
    o_ref[...] = x1 + _rms(y, postw_ref[...])


def _mlp(x, og, osw, wout, pmw, pfw, wgate, wup, convw, convb, wdown, postw, tm):
    B, T, D = x.shape
    n_ff = wgate.shape[1] // FF_CHUNK
    tok = lambda w: pl.BlockSpec((None, tm, w), lambda b, i: (b, i, 0))
    full = lambda a: pl.BlockSpec(a.shape, lambda b, i: (0,) * a.ndim, pipeline_mode=pl.Buffered(1))
    return pl.pallas_call(
        _mlp_kernel,
        grid=(B, T // tm),
        in_specs=[tok(D), tok(GDN_QK), tok(SWA_Q), full(wout), full(pmw), full(pfw), full(wgate), full(wup),
                  full(convw), full(convb), full(wdown), full(postw)],
        out_specs=tok(D),
        out_shape=jax.ShapeDtypeStruct((B, T, D), x.dtype),
        scratch_shapes=[pltpu.VMEM((tm + SUBLANES, FF_CHUNK), F32),
                        pltpu.VMEM((n_ff, SUBLANES, FF_CHUNK), F32),
                        pltpu.VMEM((tm, n_ff * FF_CHUNK), BF16)],
        compiler_params=pltpu.CompilerParams(
            dimension_semantics=("arbitrary", "arbitrary"), vmem_limit_bytes=VMEM_LIMIT_BYTES),
        name="mlp",
    )(x, og, osw, wout, pmw, pfw, wgate, wup, convw, convb, wdown, postw)


def _token_tile(T, largest):
    for tm in (1024, 512, 256, 128):
        if tm <= largest and T % tm == 0:
            return tm
    raise ValueError(f"sequence length {T} must be a multiple of 128")


def kernel(x, pre_mix_norm_w, w_in, gdn_conv_w, gdn_a_log, gdn_dt_bias, gdn_norm_w, swa_sinks, rel_bias_table,
           swa_norm_w, w_out, post_mix_norm_w, pre_ffn_norm_w, w_gate, w_up, ffn_conv_w, ffn_conv_b, w_down,
           post_ffn_norm_w):
    B, T, D = x.shape
    depth = w_in.shape[0]
    d_ff = w_gate.shape[-1]
    assert T % GDN_STEP == 0 and T % WINDOW == 0 and d_ff % FF_CHUNK == 0
    tm_in = _token_tile(T, INPROJ_TILE)
    tm_mlp = _token_tile(T, MLP_TILE)
    gdn_qkv = 3 * GDN_QK
    n_gate = 2 * GDN_HEADS
    ab0 = gdn_qkv + GDN_QK
    row = lambda a: a.reshape(1, -1).astype(F32)

    bias = _rel_bias(rel_bias_table)

    for l in range(depth):
        wl = w_in[l]
        qs0 = ab0 + n_gate
        wmain = jnp.concatenate([wl[:, :ab0], wl[:, qs0:qs0 + SWA_Q] * (SWA_HEAD_DIM ** -0.5 * LOG2_E),
                                 wl[:, qs0 + SWA_Q:]], axis=1).astype(BF16)
        wab = jnp.pad(wl[:, ab0:ab0 + n_gate], ((0, 0), (0, LANES - n_gate))).astype(BF16)
        alog_row = jnp.pad(row(gdn_a_log[l]), ((0, 0), (0, LANES - GDN_HEADS)))
        dtb_row = jnp.pad(row(gdn_dt_bias[l]), ((0, 0), (0, LANES - GDN_HEADS)))

        q, k, v, z, gb, qs, kvs = _inproj(x, row(pre_mix_norm_w[l]), wmain, wab, gdn_conv_w[l].astype(F32),
                                          alog_row, dtb_row, tm_in)
        o_g = _gdn(q, k, v, z, gb, row(gdn_norm_w[l]))
        o_s = _swa(swa_sinks[l].astype(F32), qs, kvs, bias, row(swa_norm_w[l]))

        x = _mlp(x, o_g, o_s, w_out[l].astype(BF16), row(post_mix_norm_w[l]), row(pre_ffn_norm_w[l]),
                 w_gate[l].astype(BF16), (0.5 * w_up[l]).astype(BF16), ffn_conv_w[l].astype(F32), row(ffn_conv_b[l]),
                 w_down[l].astype(BF16), row(post_ffn_norm_w[l]), tm_mlp)
    return x
```

```python
import functools
import math

import numpy as np
import jax
import jax.numpy as jnp
from jax import lax
from jax.experimental import pallas as pl
from jax.experimental.pallas import tpu as pltpu

F32 = jnp.float32
BF16 = jnp.bfloat16

GDN_HEADS = 4
GDN_DK = 128
GDN_CONV = 4
GDN_CHUNK = 64
SWA_Q_HEADS = 8
SWA_KV_HEADS = 2
SWA_HEAD_DIM = 64
WINDOW = 128
REL_BUCKETS = 32
REL_MAX_DIST = 128
FFN_CONV = 3
NORM_EPS = 1e-6
NEG_INF = -1e30
LOG2_E = math.log2(math.e)

GDN_QK = GDN_HEADS * GDN_DK
SWA_Q = SWA_Q_HEADS * SWA_HEAD_DIM
SWA_KV = SWA_KV_HEADS * SWA_HEAD_DIM

LANES = 128
SUBLANES = 8
VMEM_LIMIT_BYTES = 56 * 1024 * 1024

GDN_SUPER = 4 * GDN_CHUNK
GDN_STEP = 4 * GDN_SUPER
FF_CHUNK = 256
INPROJ_TILE = 1024
MLP_TILE = 1024
MLP_SUBTILES = 2
SWA_UNROLL = 4


def _rms(x, w):
    return x * lax.rsqrt(jnp.mean(x * x, axis=-1, keepdims=True) + NORM_EPS) * w


def _sigmoid(x):
    return 1.0 / (1.0 + jnp.exp2(x * (-LOG2_E)))


def _silu(x):
    return x * _sigmoid(x)


def _dot(a, b):
    return jnp.dot(a, b, preferred_element_type=F32)


def _dot_nt(a, b):
    return lax.dot_general(a, b, (((1,), (1,)), ((), ())), preferred_element_type=F32)


def _inproj_kernel(x_ref, nw_ref, wmain_ref, wab_ref, convw_ref, alog_ref, dtb_ref,
                   q_ref, k_ref, v_ref, z_ref, gb_ref, qs_ref, kvs_ref, pad_ref):
    tm = x_ref.shape[0]
    halo = SUBLANES

    @pl.when(pl.program_id(1) == 0)
    def _():
        pad_ref[:, 0:halo, :] = jnp.zeros((3 * GDN_HEADS, halo, GDN_DK), F32)

    h = _rms(x_ref[...], nw_ref[...]).astype(BF16)

    ab = _dot(h, wab_ref[...])
    lane = lax.broadcasted_iota(jnp.int32, ab.shape, 1)
    pre = ab + dtb_ref[...]
    softplus = jnp.maximum(pre, 0.0) + jnp.log(1.0 + jnp.exp(-jnp.abs(pre)))
    g = -jnp.exp(alog_ref[...]) * softplus
    gb_ref[...] = jnp.where(lane < GDN_HEADS, g, _sigmoid(ab))

    outs = (q_ref, k_ref, v_ref)
    for grp in range(3):
        c0 = grp * GDN_QK
        raw = _dot(h, wmain_ref[:, c0:c0 + GDN_QK])
        for hd in range(GDN_HEADS):
            lanes = slice(hd * GDN_DK, (hd + 1) * GDN_DK)
            slab = grp * GDN_HEADS + hd
            pad_ref[slab, halo:halo + tm, :] = raw[:, lanes]
            cw = convw_ref[:, c0 + hd * GDN_DK:c0 + (hd + 1) * GDN_DK]
            y = raw[:, lanes] * cw[GDN_CONV - 1:GDN_CONV]
            for j in range(GDN_CONV - 1):
                r0 = halo - (GDN_CONV - 1) + j
                y = y + pad_ref[slab, r0:r0 + tm, :] * cw[j:j + 1]
            pad_ref[slab, 0:halo, :] = pad_ref[slab, tm:tm + halo, :]
            y = _silu(y)
            if grp < 2:
                scale = GDN_DK ** -0.5 if grp == 0 else 1.0
                y = y * (lax.rsqrt(jnp.sum(y * y, axis=-1, keepdims=True) + NORM_EPS) * scale)
            outs[grp][:, lanes] = y.astype(BF16)

    c0 = 3 * GDN_QK
    z_ref[...] = _dot(h, wmain_ref[:, c0:c0 + GDN_QK]).astype(BF16)
    c0 += GDN_QK
    qs_ref[...] = _dot(h, wmain_ref[:, c0:c0 + SWA_Q]).astype(BF16)
    c0 += SWA_Q
    kvs_ref[...] = _dot(h, wmain_ref[:, c0:c0 + 2 * SWA_KV]).astype(BF16)


def _inproj(x, nw, wmain, wab, convw, alog_row, dtb_row, tm):
    B, T, D = x.shape
    tok = lambda w: pl.BlockSpec((None, tm, w), lambda b, i: (b, i, 0))
    full = lambda a: pl.BlockSpec(a.shape, lambda b, i: (0,) * a.ndim)
    sds = lambda w, dt: jax.ShapeDtypeStruct((B, T, w), dt)
    return pl.pallas_call(
        _inproj_kernel,
        grid=(B, T // tm),
        in_specs=[tok(D), full(nw), full(wmain), full(wab), full(convw), full(alog_row), full(dtb_row)],
        out_specs=[tok(GDN_QK), tok(GDN_QK), tok(GDN_QK), tok(GDN_QK), tok(LANES), tok(SWA_Q), tok(2 * SWA_KV)],
        out_shape=[sds(GDN_QK, BF16), sds(GDN_QK, BF16), sds(GDN_QK, BF16), sds(GDN_QK, BF16),
                   sds(LANES, F32), sds(SWA_Q, BF16), sds(2 * SWA_KV, BF16)],
        scratch_shapes=[pltpu.VMEM((3 * GDN_HEADS, tm + SUBLANES, GDN_DK), F32)],
        compiler_params=pltpu.CompilerParams(
            dimension_semantics=("arbitrary", "arbitrary"), vmem_limit_bytes=VMEM_LIMIT_BYTES),
        name="inproj",
    )(x, nw, wmain, wab, convw, alog_row, dtb_row)


def _split3(a):
    hi = a.astype(BF16)
    r1 = a - hi.astype(F32)
    mid = r1.astype(BF16)
    lo = (r1 - mid.astype(F32)).astype(BF16)
    return hi, mid, lo


def _lane_col(a, idx):
    lane = lax.broadcasted_iota(jnp.int32, a.shape, 1)
    return jnp.sum(jnp.where(lane == idx, a, 0.0), axis=-1, keepdims=True)


def _gdn_kernel(q_ref, k_ref, v_ref, z_ref, gb_ref, nw_ref, o_ref, s_ref):
    R = GDN_SUPER
    C = GDN_CHUNK
    n_chunks = R // C
    n_super = q_ref.shape[0] // R
    shift = int(math.log2(C))
    heads = range(GDN_HEADS)
    units = [(sc, hd) for sc in range(n_super) for hd in heads]

    @pl.when(pl.program_id(1) == 0)
    def _():
        s_ref[...] = jnp.zeros(s_ref.shape, F32)

    row = lax.broadcasted_iota(jnp.int32, (R, R), 0)
    col = lax.broadcasted_iota(jnp.int32, (R, R), 1)
    same = (row >> shift) == (col >> shift)
    causal = jnp.logical_and(same, row >= col)
    strict = jnp.logical_and(same, row > col)
    eye_packed = jnp.where((lax.broadcasted_iota(jnp.int32, (C, R), 1) & (C - 1))
                           == lax.broadcasted_iota(jnp.int32, (C, R), 0), 1.0, 0.0)
    sums_to = jnp.concatenate([jnp.where(jnp.logical_and(same, row <= col), 1.0, 0.0),
                               jnp.where(same, 1.0, 0.0)], axis=1).astype(BF16)

    def pack(full):
        return sum(full[j * C:(j + 1) * C] for j in range(1, n_chunks)) + full[0:C]

    def block_diag(packed):
        return jnp.where(same, jnp.concatenate([packed] * n_chunks, axis=0), jnp.zeros((R, R), BF16))

    gbs, gcum_rows, cols = [], [], []
    for sc in range(n_super):
        gb = gb_ref[sc * R:(sc + 1) * R, :]
        hi, mid, lo = (p.astype(F32)[0:SUBLANES] for p in _split3(gb.T))
        parts = jnp.concatenate([hi, mid, lo, jnp.zeros_like(hi)], axis=0).astype(BF16)
        sums = _dot(parts, sums_to)
        sums = sums[0:SUBLANES] + sums[SUBLANES:2 * SUBLANES] + sums[2 * SUBLANES:3 * SUBLANES]
        gbs.append(gb)
        gcum_rows.append(sums[:, :R])
        cols.append(jnp.concatenate([sums[:, :R], sums[:, R:], jnp.zeros((LANES - 2 * SUBLANES, R), F32)], axis=0).T)

    rows_of = lambda sc: slice(sc * R, (sc + 1) * R)
    lanes_of = lambda hd: slice(hd * GDN_DK, (hd + 1) * GDN_DK)
    wide = lambda c: jnp.broadcast_to(c, (R, GDN_DK))
    gc = [_lane_col(cols[sc], hd) for sc, hd in units]
    gl = [_lane_col(cols[sc], SUBLANES + hd) for sc, hd in units]
    beta = [_lane_col(gbs[sc], GDN_HEADS + hd) for sc, hd in units]
    n_units = range(len(units))
    decay = [jnp.exp(gc[u] - gcum_rows[sc][hd:hd + 1, :]) for u, (sc, hd) in enumerate(units)]
    qh = [q_ref[rows_of(sc), lanes_of(hd)] for sc, hd in units]
    kh = [k_ref[rows_of(sc), lanes_of(hd)] for sc, hd in units]
    kf = [kh[u].astype(F32) for u in n_units]
    kb = [kf[u] * beta[u] for u in n_units]
    eg = [jnp.exp(wide(gc[u])) for u in n_units]
    rhs = [jnp.concatenate([v_ref[rows_of(sc), lanes_of(hd)].astype(F32) * beta[u], kb[u] * eg[u]],
                           axis=1).astype(BF16) for u, (sc, hd) in enumerate(units)]

    kq = [_dot_nt(jnp.concatenate([kb[u].astype(BF16), qh[u]], axis=0), kh[u]) for u in n_units]
    a = [jnp.where(strict, kq[u][:R] * decay[u], 0.0) for u in n_units]
    qk = [jnp.where(causal, kq[u][R:] * decay[u], 0.0).astype(BF16) for u in n_units]

    a_packed = [pack(a[u]) for u in n_units]
    p = [eye_packed - a_packed[u] for u in n_units]
    xpow = [_dot(a_packed[u].astype(BF16), a[u].astype(BF16)) for u in n_units]
    for _ in range(shift - 2):
        xb = [xpow[u].astype(BF16) for u in n_units]
        y = [_dot(jnp.concatenate([p[u].astype(BF16), xb[u]], axis=0), block_diag(xb[u])) for u in n_units]
        p = [p[u] + y[u][:C] for u in n_units]
        xpow = [y[u][C:] for u in n_units]
    p = [p[u] + _dot(p[u].astype(BF16), block_diag(xpow[u].astype(BF16))) for u in n_units]

    uw = [_dot(block_diag(p[u].astype(BF16)), rhs[u]).astype(BF16) for u in n_units]
    qkuw = [_dot(qk[u], uw[u]) for u in n_units]
    q_eff = [(qh[u].astype(F32) * eg[u] - qkuw[u][:, GDN_DK:]).astype(BF16) for u in n_units]
    kd_t = [(kf[u] * jnp.exp(wide(gl[u] - gc[u]))).T.astype(BF16) for u in n_units]
    g_last = [jnp.exp(wide(gl[u])) for u in n_units]
    row_chunk = lax.broadcasted_iota(jnp.int32, (R, 2 * GDN_DK), 0) >> shift
    zero_uw = jnp.zeros((R, 2 * GDN_DK), BF16)
    nm = [[_dot(kd_t[u], jnp.where(row_chunk == c, uw[u], zero_uw)) for c in range(n_chunks)] for u in n_units]

    state = [s_ref[hd] for hd in heads]
    zero_s = jnp.zeros((GDN_DK, GDN_DK), BF16)
    for sc in range(n_super):
        o_chunks = [[] for _ in heads]
        for c in range(n_chunks):
            rows = slice(c * C, (c + 1) * C)
            for h0 in range(0, GDN_HEADS, 2):
                h1 = h0 + 1
                u0, u1 = sc * GDN_HEADS + h0, sc * GDN_HEADS + h1
                lhs = jnp.concatenate(
                    [jnp.concatenate([q_eff[u0][rows], q_eff[u1][rows]], axis=1),
                     jnp.concatenate([nm[u0][c][:, GDN_DK:], nm[u1][c][:, GDN_DK:]], axis=1).astype(BF16)], axis=0)
                s_pair = jnp.concatenate(
                    [jnp.concatenate([state[h0].astype(BF16), zero_s], axis=1),
                     jnp.concatenate([zero_s, state[h1].astype(BF16)], axis=1)], axis=0)
                r = _dot(lhs, s_pair)
                for i, (hd, u) in enumerate(((h0, u0), (h1, u1))):
                    half = slice(i * GDN_DK, (i + 1) * GDN_DK)
                    o_chunks[hd].append(r[:C, half] + qkuw[u][rows, :GDN_DK])
                    state[hd] = state[hd] * g_last[u][c * C:c * C + 1, :] - r[C:, half] + nm[u][c][:, :GDN_DK]
        for hd in heads:
            o = jnp.concatenate(o_chunks[hd], axis=0)
            o = _rms(o, nw_ref[...]) * _silu(z_ref[rows_of(sc), lanes_of(hd)].astype(F32))
            o_ref[rows_of(sc), lanes_of(hd)] = o.astype(BF16)

    for hd in heads:
        s_ref[hd] = state[hd]


def _gdn(q, k, v, z, gb, nw):
    B, T, _ = q.shape
    rows = GDN_STEP
    tok = lambda w: pl.BlockSpec((None, rows, w), lambda b, i: (b, i, 0))
    return pl.pallas_call(
        _gdn_kernel,
        grid=(B, T // rows),
        in_specs=[tok(GDN_QK), tok(GDN_QK), tok(GDN_QK), tok(GDN_QK), tok(LANES),
                  pl.BlockSpec(nw.shape, lambda b, i: (0, 0))],
        out_specs=tok(GDN_QK),
        out_shape=jax.ShapeDtypeStruct((B, T, GDN_QK), BF16),
        scratch_shapes=[pltpu.VMEM((GDN_HEADS, GDN_DK, GDN_DK), F32)],
        compiler_params=pltpu.CompilerParams(
            dimension_semantics=("arbitrary", "arbitrary"), vmem_limit_bytes=VMEM_LIMIT_BYTES),
        name="gdn",
    )(q, k, v, z, gb, nw)


def _band_tables():
    qi = np.arange(WINDOW, dtype=np.int64)[:, None]
    sj = np.arange(2 * WINDOW, dtype=np.int64)[None, :]
    dist = qi + WINDOW - sj
    in_band = (dist >= 0) & (dist < WINDOW)
    d = np.maximum(dist, 0)
    max_exact = REL_BUCKETS // 2
    ratio = np.log(np.maximum(d, 1).astype(np.float32) / np.float32(max_exact)) / np.float32(
        math.log(REL_MAX_DIST / max_exact))
    large = max_exact + (ratio.astype(np.float32) * np.float32(REL_BUCKETS - max_exact)).astype(np.int32)
    large = np.minimum(large, REL_BUCKETS - 1)
    bucket = np.where(d < max_exact, d, large).astype(np.int32)
    valid = np.stack([in_band & (sj >= WINDOW), in_band]).astype(np.int32)
    return bucket, valid


def _bias_kernel(table_ref, bucket_ref, valid_ref, o_ref):
    bucket = bucket_ref[...]
    for hd in range(SWA_Q_HEADS):
        acc = jnp.zeros(bucket.shape, F32)
        for b in range(REL_BUCKETS):
            acc = jnp.where(bucket == b, table_ref[b, hd] * LOG2_E, acc)
        for var in range(2):
            o_ref[var, hd] = jnp.where(valid_ref[var] != 0, acc, NEG_INF)


def _rel_bias(table):
    bucket, valid = _band_tables()
    return pl.pallas_call(
        _bias_kernel,
        in_specs=[pl.BlockSpec(memory_space=pltpu.SMEM),
                  pl.BlockSpec(memory_space=pltpu.VMEM), pl.BlockSpec(memory_space=pltpu.VMEM)],
        out_specs=pl.BlockSpec(memory_space=pltpu.VMEM),
        out_shape=jax.ShapeDtypeStruct((2, SWA_Q_HEADS, WINDOW, 2 * WINDOW), F32),
        name="rel_bias",
    )(table.astype(F32), jnp.asarray(bucket), jnp.asarray(valid))


def _swa_kernel(sinks_ref, qs_ref, kv_ref, bias_ref, nw_ref, o_ref):
    W = WINDOW
    T = qs_ref.shape[0]
    group = SWA_Q_HEADS // SWA_KV_HEADS
    lo_half = lax.broadcasted_iota(jnp.int32, (W, LANES), 1) < SWA_HEAD_DIM

    def block(n, carry):
        r0 = pl.multiple_of(n * W, W)
        p0 = pl.multiple_of(jnp.maximum(n - 1, 0) * W, W)
        var = jnp.minimum(n, 1)
        q = qs_ref[pl.ds(r0, W), :]
        kv = jnp.concatenate([kv_ref[pl.ds(p0, W), :], kv_ref[pl.ds(r0, W), :]], axis=0).astype(F32)
        k_pair = kv[:, :LANES]
        v_pair = kv[:, LANES:]
        k_opts = (k_pair.astype(BF16), pltpu.roll(k_pair, SWA_HEAD_DIM, axis=1).astype(BF16))
        v_opts = (v_pair.astype(BF16), pltpu.roll(v_pair, SWA_HEAD_DIM, axis=1).astype(BF16))

        pairs = []
        ss = jnp.zeros((W, 1), F32)
        for pr in range(SWA_Q_HEADS // 2):
            halves = []
            for half in range(2):
                hd = 2 * pr + half
                kvh = hd // group
                sel = 0 if half == kvh else 1
                qp = q[:, pr * LANES:(pr + 1) * LANES]
                qm = jnp.where(lo_half if half == 0 else jnp.logical_not(lo_half), qp, jnp.zeros_like(qp))
                s = _dot_nt(qm, k_opts[sel]) + bias_ref[var, hd]
                sink = sinks_ref[hd] * LOG2_E
                m = jnp.maximum(jnp.max(s, axis=-1, keepdims=True), sink)
                e = jnp.exp2(s - m)
                denom = jnp.sum(e, axis=-1, keepdims=True) + jnp.exp2(sink - m)
                halves.append(_dot(e.astype(BF16), v_opts[sel]) / denom)
            o_pair = jnp.where(lo_half, halves[0], halves[1])
            ss = ss + jnp.sum(o_pair * o_pair, axis=-1, keepdims=True)
            pairs.append(o_pair)
        inv = lax.rsqrt(ss * (1.0 / SWA_Q) + NORM_EPS)
        for pr, o_pair in enumerate(pairs):
            lanes = slice(pr * LANES, (pr + 1) * LANES)
            o_ref[pl.ds(r0, W), lanes] = (o_pair * inv * nw_ref[:, lanes]).astype(BF16)
        return carry

    lax.fori_loop(0, T // W, block, 0, unroll=SWA_UNROLL)


def _swa(sinks, qs, kvs, bias, nw):
    B, T, _ = qs.shape
    seq = lambda w: pl.BlockSpec((None, T, w), lambda b: (b, 0, 0))
    return pl.pallas_call(
        _swa_kernel,
        grid=(B,),
        in_specs=[pl.BlockSpec(memory_space=pltpu.SMEM), seq(SWA_Q), seq(2 * SWA_KV),
                  pl.BlockSpec(bias.shape, lambda b: (0, 0, 0, 0)),
                  pl.BlockSpec(nw.shape, lambda b: (0, 0))],
        out_specs=seq(SWA_Q),
        out_shape=jax.ShapeDtypeStruct((B, T, SWA_Q), BF16),
        compiler_params=pltpu.CompilerParams(
            dimension_semantics=("arbitrary",), vmem_limit_bytes=VMEM_LIMIT_BYTES),
        name="swa",
    )(sinks, qs, kvs, bias, nw)


GELU_K0 = math.sqrt(2.0 / math.pi)
GELU_K1 = 0.044715 * GELU_K0


def _gelu_tanh_x2(x):
    return x * (1.0 + jnp.tanh(x * (GELU_K0 + GELU_K1 * (x * x))))


def _mlp_kernel(x_ref, og_ref, os_ref, wout_ref, pmw_ref, pfw_ref, wgate_ref, wup_ref, convw_ref, convb_ref,
                wdown_ref, postw_ref, o_ref, pad_ref, carry_ref, act_ref):
    tm = x_ref.shape[0]
    halo = SUBLANES
    n_ff = wgate_ref.shape[1] // FF_CHUNK
    n_sub = MLP_SUBTILES
    sub = tm // n_sub

    @pl.when(pl.program_id(1) == 0)
    def _():
        carry_ref[...] = jnp.zeros(carry_ref.shape, F32)

    x1s, hs = [], []
    for t in range(n_sub):
        rows = slice(t * sub, (t + 1) * sub)
        mix = _dot(og_ref[rows, :], wout_ref[0:GDN_QK, :]) + _dot(os_ref[rows, :], wout_ref[GDN_QK:, :])
        x1 = x_ref[rows, :] + _rms(mix, pmw_ref[...])
        x1s.append(x1)
        hs.append(_rms(x1, pfw_ref[...]).astype(BF16))

    for t in range(n_sub):
        rows = slice(t * sub, (t + 1) * sub)
        for c in range(n_ff):
            cols = slice(c * FF_CHUNK, (c + 1) * FF_CHUNK)
            gate = _dot(hs[t], wgate_ref[:, cols])
            up = _dot(hs[t], wup_ref[:, cols])
            pad_ref[t, 0:halo, :] = carry_ref[c]
            pad_ref[t, halo:halo + sub, :] = gate
            carry_ref[c] = gate[sub - halo:, :]
            cw = convw_ref[:, cols]
            y = gate * cw[FFN_CONV - 1:FFN_CONV] + convb_ref[:, cols]
            for j in range(FFN_CONV - 1):
                r0 = halo - (FFN_CONV - 1) + j
                y = y + pad_ref[t, r0:r0 + sub, :] * cw[j:j + 1]
            act_ref[rows, cols] = (_gelu_tanh_x2(y) * up).astype(BF16)
        y = _dot(act_ref[rows, :], wdown_ref[...])
        o_ref[rows, :] = x1s[t] + _rms(y, postw_ref[...])


def _mlp(x, og, osw, wout, pmw, pfw, wgate, wup, convw, convb, wdown, postw, tm):
    B, T, D = x.shape
    n_ff = wgate.shape[1] // FF_CHUNK
    tok = lambda w: pl.BlockSpec((None, tm, w), lambda b, i: (b, i, 0))
    full = lambda a: pl.BlockSpec(a.shape, lambda b, i: (0,) * a.ndim, pipeline_mode=pl.Buffered(1))
    return pl.pallas_call(
        _mlp_kernel,
        grid=(B, T // tm),
        in_specs=[tok(D), tok(GDN_QK), tok(SWA_Q), full(wout), full(pmw), full(pfw), full(wgate), full(wup),
                  full(convw), full(convb), full(wdown), full(postw)],
        out_specs=tok(D),
        out_shape=jax.ShapeDtypeStruct((B, T, D), x.dtype),
        scratch_shapes=[pltpu.VMEM((MLP_SUBTILES, tm // MLP_SUBTILES + SUBLANES, FF_CHUNK), F32),
                        pltpu.VMEM((n_ff, SUBLANES, FF_CHUNK), F32),
                        pltpu.VMEM((tm, n_ff * FF_CHUNK), BF16)],
        compiler_params=pltpu.CompilerParams(
            dimension_semantics=("arbitrary", "arbitrary"), vmem_limit_bytes=VMEM_LIMIT_BYTES),
        name="mlp",
    )(x, og, osw, wout, pmw, pfw, wgate, wup, convw, convb, wdown, postw)


def _token_tile(T, largest):
    for tm in (1024, 512, 256, 128):
        if tm <= largest and T % tm == 0:
            return tm
    raise ValueError(f"sequence length {T} must be a multiple of 128")


def kernel(x, pre_mix_norm_w, w_in, gdn_conv_w, gdn_a_log, gdn_dt_bias, gdn_norm_w, swa_sinks, rel_bias_table,
           swa_norm_w, w_out, post_mix_norm_w, pre_ffn_norm_w, w_gate, w_up, ffn_conv_w, ffn_conv_b, w_down,
           post_ffn_norm_w):
    B, T, D = x.shape
    depth = w_in.shape[0]
    d_ff = w_gate.shape[-1]
    assert T % GDN_STEP == 0 and T % WINDOW == 0 and d_ff % FF_CHUNK == 0
    tm_in = _token_tile(T, INPROJ_TILE)
    tm_mlp = _token_tile(T, MLP_TILE)
    gdn_qkv = 3 * GDN_QK
    n_gate = 2 * GDN_HEADS
    ab0 = gdn_qkv + GDN_QK
    row = lambda a: a.reshape(1, -1).astype(F32)

    bias = _rel_bias(rel_bias_table)

    for l in range(depth):
        wl = w_in[l]
        qs0 = ab0 + n_gate
        wmain = jnp.concatenate([wl[:, :ab0], wl[:, qs0:qs0 + SWA_Q] * (SWA_HEAD_DIM ** -0.5 * LOG2_E),
                                 wl[:, qs0 + SWA_Q:]], axis=1).astype(BF16)
        wab = jnp.pad(wl[:, ab0:ab0 + n_gate], ((0, 0), (0, LANES - n_gate))).astype(BF16)
        alog_row = jnp.pad(row(gdn_a_log[l]), ((0, 0), (0, LANES - GDN_HEADS)))
        dtb_row = jnp.pad(row(gdn_dt_bias[l]), ((0, 0), (0, LANES - GDN_HEADS)))

        q, k, v, z, gb, qs, kvs = _inproj(x, row(pre_mix_norm_w[l]), wmain, wab, gdn_conv_w[l].astype(F32),
                                          alog_row, dtb_row, tm_in)
        o_g = _gdn(q, k, v, z, gb, row(gdn_norm_w[l]))
        o_s = _swa(swa_sinks[l].astype(F32), qs, kvs, bias, row(swa_norm_w[l]))

        x = _mlp(x, o_g, o_s, w_out[l].astype(BF16), row(post_mix_norm_w[l]), row(pre_ffn_norm_w[l]),
                 w_gate[l].astype(BF16), (0.5 * w_up[l]).astype(BF16), ffn_conv_w[l].astype(F32), row(ffn_conv_b[l]),
                 w_down[l].astype(BF16), row(post_ffn_norm_w[l]), tm_mlp)
    return x
```

```python
import functools
import math

import numpy as np
import jax
import jax.numpy as jnp
from jax import lax
from jax.experimental import pallas as pl
from jax.experimental.pallas import tpu as pltpu

F32 = jnp.float32
BF16 = jnp.bfloat16

GDN_HEADS = 4
GDN_DK = 128
GDN_CONV = 4
GDN_CHUNK = 64
SWA_Q_HEADS = 8
SWA_KV_HEADS = 2
SWA_HEAD_DIM = 64
WINDOW = 128
REL_BUCKETS = 32
REL_MAX_DIST = 128
FFN_CONV = 3
NORM_EPS = 1e-6
NEG_INF = -1e30
LOG2_E = math.log2(math.e)

GDN_QK = GDN_HEADS * GDN_DK
SWA_Q = SWA_Q_HEADS * SWA_HEAD_DIM
SWA_KV = SWA_KV_HEADS * SWA_HEAD_DIM

LANES = 128
SUBLANES = 8
VMEM_LIMIT_BYTES = 56 * 1024 * 1024

GDN_SUPER = 4 * GDN_CHUNK
GDN_STEP = 4 * GDN_SUPER
FF_CHUNK = 256
INPROJ_TILE = 1024
MLP_TILE = 1024
INPROJ_SUBTILES = 2
MLP_SUBTILES = 2
SWA_UNROLL = 4


def _rms(x, w):
    return x * lax.rsqrt(jnp.mean(x * x, axis=-1, keepdims=True) + NORM_EPS) * w


def _sigmoid(x):
    return 1.0 / (1.0 + jnp.exp2(x * (-LOG2_E)))


def _silu(x):
    return x * _sigmoid(x)


def _dot(a, b):
    return jnp.dot(a, b, preferred_element_type=F32)


def _dot_nt(a, b):
    return lax.dot_general(a, b, (((1,), (1,)), ((), ())), preferred_element_type=F32)


def _inproj_kernel(x_ref, nw_ref, wmain_ref, wab_ref, convw_ref, alog_ref, dtb_ref,
                   q_ref, k_ref, v_ref, z_ref, gb_ref, qs_ref, kvs_ref, pad_ref):
    tm = x_ref.shape[0]
    halo = SUBLANES
    n_sub = INPROJ_SUBTILES
    sub = tm // n_sub

    @pl.when(pl.program_id(1) == 0)
    def _():
        pad_ref[:, 0:halo, :] = jnp.zeros((3 * GDN_HEADS, halo, GDN_DK), F32)

    outs = (q_ref, k_ref, v_ref)
    for t in range(n_sub):
        rows = slice(t * sub, (t + 1) * sub)
        h = _rms(x_ref[rows, :], nw_ref[...]).astype(BF16)

        ab = _dot(h, wab_ref[...])
        lane = lax.broadcasted_iota(jnp.int32, ab.shape, 1)
        pre = ab + dtb_ref[...]
        softplus = jnp.maximum(pre, 0.0) + jnp.log(1.0 + jnp.exp(-jnp.abs(pre)))
        g = -jnp.exp(alog_ref[...]) * softplus
        gb_ref[rows, :] = jnp.where(lane < GDN_HEADS, g, _sigmoid(ab))

        for grp in range(3):
            c0 = grp * GDN_QK
            raw = _dot(h, wmain_ref[:, c0:c0 + GDN_QK])
            for hd in range(GDN_HEADS):
                lanes = slice(hd * GDN_DK, (hd + 1) * GDN_DK)
                slab = grp * GDN_HEADS + hd
                pad_ref[slab, halo:halo + sub, :] = raw[:, lanes]
                cw = convw_ref[:, c0 + hd * GDN_DK:c0 + (hd + 1) * GDN_DK]
                y = raw[:, lanes] * cw[GDN_CONV - 1:GDN_CONV]
                for j in range(GDN_CONV - 1):
                    r0 = halo - (GDN_CONV - 1) + j
                    y = y + pad_ref[slab, r0:r0 + sub, :] * cw[j:j + 1]
                pad_ref[slab, 0:halo, :] = pad_ref[slab, sub:sub + halo, :]
                y = _silu(y)
                if grp < 2:
                    scale = GDN_DK ** -0.5 if grp == 0 else 1.0
                    y = y * (lax.rsqrt(jnp.sum(y * y, axis=-1, keepdims=True) + NORM_EPS) * scale)
                outs[grp][rows, lanes] = y.astype(BF16)

        c0 = 3 * GDN_QK
        z_ref[rows, :] = _dot(h, wmain_ref[:, c0:c0 + GDN_QK]).astype(BF16)
        c0 += GDN_QK
        qs_ref[rows, :] = _dot(h, wmain_ref[:, c0:c0 + SWA_Q]).astype(BF16)
        c0 += SWA_Q
        kvs_ref[rows, :] = _dot(h, wmain_ref[:, c0:c0 + 2 * SWA_KV]).astype(BF16)


def _inproj(x, nw, wmain, wab, convw, alog_row, dtb_row, tm):
    B, T, D = x.shape
    tok = lambda w: pl.BlockSpec((None, tm, w), lambda b, i: (b, i, 0))
    full = lambda a: pl.BlockSpec(a.shape, lambda b, i: (0,) * a.ndim)
    sds = lambda w, dt: jax.ShapeDtypeStruct((B, T, w), dt)
    return pl.pallas_call(
        _inproj_kernel,
        grid=(B, T // tm),
        in_specs=[tok(D), full(nw), full(wmain), full(wab), full(convw), full(alog_row), full(dtb_row)],
        out_specs=[tok(GDN_QK), tok(GDN_QK), tok(GDN_QK), tok(GDN_QK), tok(LANES), tok(SWA_Q), tok(2 * SWA_KV)],
        out_shape=[sds(GDN_QK, BF16), sds(GDN_QK, BF16), sds(GDN_QK, BF16), sds(GDN_QK, BF16),
                   sds(LANES, F32), sds(SWA_Q, BF16), sds(2 * SWA_KV, BF16)],
        scratch_shapes=[pltpu.VMEM((3 * GDN_HEADS, tm // INPROJ_SUBTILES + SUBLANES, GDN_DK), F32)],
        compiler_params=pltpu.CompilerParams(
            dimension_semantics=("arbitrary", "arbitrary"), vmem_limit_bytes=VMEM_LIMIT_BYTES),
        name="inproj",
    )(x, nw, wmain, wab, convw, alog_row, dtb_row)


def _split3(a):
    hi = a.astype(BF16)
    r1 = a - hi.astype(F32)
    mid = r1.astype(BF16)
    lo = (r1 - mid.astype(F32)).astype(BF16)
    return hi, mid, lo


def _lane_col(a, idx):
    lane = lax.broadcasted_iota(jnp.int32, a.shape, 1)
    return jnp.sum(jnp.where(lane == idx, a, 0.0), axis=-1, keepdims=True)


def _gdn_kernel(q_ref, k_ref, v_ref, z_ref, gb_ref, nw_ref, o_ref, s_ref):
    R = GDN_SUPER
    C = GDN_CHUNK
    n_chunks = R // C
    n_super = q_ref.shape[0] // R
    shift = int(math.log2(C))
    heads = range(GDN_HEADS)
    units = [(sc, hd) for sc in range(n_super) for hd in heads]

    @pl.when(pl.program_id(1) == 0)
    def _():
        s_ref[...] = jnp.zeros(s_ref.shape, F32)

    row = lax.broadcasted_iota(jnp.int32, (R, R), 0)
    col = lax.broadcasted_iota(jnp.int32, (R, R), 1)
    same = (row >> shift) == (col >> shift)
    causal = jnp.logical_and(same, row >= col)
    strict = jnp.logical_and(same, row > col)
    eye_packed = jnp.where((lax.broadcasted_iota(jnp.int32, (C, R), 1) & (C - 1))
                           == lax.broadcasted_iota(jnp.int32, (C, R), 0), 1.0, 0.0)
    sums_to = jnp.concatenate([jnp.where(jnp.logical_and(same, row <= col), 1.0, 0.0),
                               jnp.where(same, 1.0, 0.0)], axis=1).astype(BF16)

    def pack(full):
        return sum(full[j * C:(j + 1) * C] for j in range(1, n_chunks)) + full[0:C]

    def block_diag(packed):
        return jnp.where(same, jnp.concatenate([packed] * n_chunks, axis=0), jnp.zeros((R, R), BF16))

    gbs, gcum_rows, cols = [], [], []
    for sc in range(n_super):
        gb = gb_ref[sc * R:(sc + 1) * R, :]
        hi, mid, lo = (p.astype(F32)[0:SUBLANES] for p in _split3(gb.T))
        parts = jnp.concatenate([hi, mid, lo, jnp.zeros_like(hi)], axis=0).astype(BF16)
        sums = _dot(parts, sums_to)
        sums = sums[0:SUBLANES] + sums[SUBLANES:2 * SUBLANES] + sums[2 * SUBLANES:3 * SUBLANES]
        gbs.append(gb)
        gcum_rows.append(sums[:, :R])
        cols.append(jnp.concatenate([sums[:, :R], sums[:, R:], jnp.zeros((LANES - 2 * SUBLANES, R), F32)], axis=0).T)

    rows_of = lambda sc: slice(sc * R, (sc + 1) * R)
    lanes_of = lambda hd: slice(hd * GDN_DK, (hd + 1) * GDN_DK)
    wide = lambda c: jnp.broadcast_to(c, (R, GDN_DK))
    gc = [_lane_col(cols[sc], hd) for sc, hd in units]
    gl = [_lane_col(cols[sc], SUBLANES + hd) for sc, hd in units]
    beta = [_lane_col(gbs[sc], GDN_HEADS + hd) for sc, hd in units]
    n_units = range(len(units))
    decay = [jnp.exp(gc[u] - gcum_rows[sc][hd:hd + 1, :]) for u, (sc, hd) in enumerate(units)]
    qh = [q_ref[rows_of(sc), lanes_of(hd)] for sc, hd in units]
    kh = [k_ref[rows_of(sc), lanes_of(hd)] for sc, hd in units]
    kf = [kh[u].astype(F32) for u in n_units]
    kb = [kf[u] * beta[u] for u in n_units]
    eg = [jnp.exp(wide(gc[u])) for u in n_units]
    rhs = [jnp.concatenate([v_ref[rows_of(sc), lanes_of(hd)].astype(F32) * beta[u], kb[u] * eg[u]],
                           axis=1).astype(BF16) for u, (sc, hd) in enumerate(units)]

    kq = [_dot_nt(jnp.concatenate([kb[u].astype(BF16), qh[u]], axis=0), kh[u]) for u in n_units]
    a = [jnp.where(strict, kq[u][:R] * decay[u], 0.0) for u in n_units]
    qk = [jnp.where(causal, kq[u][R:] * decay[u], 0.0).astype(BF16) for u in n_units]

    a_packed = [pack(a[u]) for u in n_units]
    p = [eye_packed - a_packed[u] for u in n_units]
    xpow = [_dot(a_packed[u].astype(BF16), a[u].astype(BF16)) for u in n_units]
    for _ in range(shift - 2):
        xb = [xpow[u].astype(BF16) for u in n_units]
        y = [_dot(jnp.concatenate([p[u].astype(BF16), xb[u]], axis=0), block_diag(xb[u])) for u in n_units]
        p = [p[u] + y[u][:C] for u in n_units]
        xpow = [y[u][C:] for u in n_units]
    p = [p[u] + _dot(p[u].astype(BF16), block_diag(xpow[u].astype(BF16))) for u in n_units]

    uw = [_dot(block_diag(p[u].astype(BF16)), rhs[u]).astype(BF16) for u in n_units]
    qkuw = [_dot(qk[u], uw[u]) for u in n_units]
    q_eff = [(qh[u].astype(F32) * eg[u] - qkuw[u][:, GDN_DK:]).astype(BF16) for u in n_units]
    kd_t = [(kf[u] * jnp.exp(wide(gl[u] - gc[u]))).T.astype(BF16) for u in n_units]
    g_last = [jnp.exp(wide(gl[u])) for u in n_units]
    row_chunk = lax.broadcasted_iota(jnp.int32, (R, 2 * GDN_DK), 0) >> shift
    zero_uw = jnp.zeros((R, 2 * GDN_DK), BF16)
    nm = [[_dot(kd_t[u], jnp.where(row_chunk == c, uw[u], zero_uw)) for c in range(n_chunks)] for u in n_units]

    state = [s_ref[hd] for hd in heads]
    zero_s = jnp.zeros((GDN_DK, GDN_DK), BF16)
    for sc in range(n_super):
        o_chunks = [[] for _ in heads]
        for c in range(n_chunks):
            rows = slice(c * C, (c + 1) * C)
            for h0 in range(0, GDN_HEADS, 2):
                h1 = h0 + 1
                u0, u1 = sc * GDN_HEADS + h0, sc * GDN_HEADS + h1
                lhs = jnp.concatenate(
                    [jnp.concatenate([q_eff[u0][rows], q_eff[u1][rows]], axis=1),
                     jnp.concatenate([nm[u0][c][:, GDN_DK:], nm[u1][c][:, GDN_DK:]], axis=1).astype(BF16)], axis=0)
                s_pair = jnp.concatenate(
                    [jnp.concatenate([state[h0].astype(BF16), zero_s], axis=1),
                     jnp.concatenate([zero_s, state[h1].astype(BF16)], axis=1)], axis=0)
                r = _dot(lhs, s_pair)
                for i, (hd, u) in enumerate(((h0, u0), (h1, u1))):
                    half = slice(i * GDN_DK, (i + 1) * GDN_DK)
                    o_chunks[hd].append(r[:C, half] + qkuw[u][rows, :GDN_DK])
                    state[hd] = state[hd] * g_last[u][c * C:c * C + 1, :] - r[C:, half] + nm[u][c][:, :GDN_DK]
        for hd in heads:
            o = jnp.concatenate(o_chunks[hd], axis=0)
            o = _rms(o, nw_ref[...]) * _silu(z_ref[rows_of(sc), lanes_of(hd)].astype(F32))
            o_ref[rows_of(sc), lanes_of(hd)] = o.astype(BF16)

    for hd in heads:
        s_ref[hd] = state[hd]


def _gdn(q, k, v, z, gb, nw):
    B, T, _ = q.shape
    rows = GDN_STEP
    tok = lambda w: pl.BlockSpec((None, rows, w), lambda b, i: (b, i, 0))
    return pl.pallas_call(
        _gdn_kernel,
        grid=(B, T // rows),
        in_specs=[tok(GDN_QK), tok(GDN_QK), tok(GDN_QK), tok(GDN_QK), tok(LANES),
                  pl.BlockSpec(nw.shape, lambda b, i: (0, 0))],
        out_specs=tok(GDN_QK),
        out_shape=jax.ShapeDtypeStruct((B, T, GDN_QK), BF16),
        scratch_shapes=[pltpu.VMEM((GDN_HEADS, GDN_DK, GDN_DK), F32)],
        compiler_params=pltpu.CompilerParams(
            dimension_semantics=("arbitrary", "arbitrary"), vmem_limit_bytes=VMEM_LIMIT_BYTES),
        name="gdn",
    )(q, k, v, z, gb, nw)


def _band_tables():
    qi = np.arange(WINDOW, dtype=np.int64)[:, None]
    sj = np.arange(2 * WINDOW, dtype=np.int64)[None, :]
    dist = qi + WINDOW - sj
    in_band = (dist >= 0) & (dist < WINDOW)
    d = np.maximum(dist, 0)
    max_exact = REL_BUCKETS // 2
    ratio = np.log(np.maximum(d, 1).astype(np.float32) / np.float32(max_exact)) / np.float32(
        math.log(REL_MAX_DIST / max_exact))
    large = max_exact + (ratio.astype(np.float32) * np.float32(REL_BUCKETS - max_exact)).astype(np.int32)
    large = np.minimum(large, REL_BUCKETS - 1)
    bucket = np.where(d < max_exact, d, large).astype(np.int32)
    valid = np.stack([in_band & (sj >= WINDOW), in_band]).astype(np.int32)
    return bucket, valid


def _bias_kernel(table_ref, bucket_ref, valid_ref, o_ref):
    bucket = bucket_ref[...]
    for hd in range(SWA_Q_HEADS):
        acc = jnp.zeros(bucket.shape, F32)
        for b in range(REL_BUCKETS):
            acc = jnp.where(bucket == b, table_ref[b, hd] * LOG2_E, acc)
        for var in range(2):
            o_ref[var, hd] = jnp.where(valid_ref[var] != 0, acc, NEG_INF)


def _rel_bias(table):
    bucket, valid = _band_tables()
    return pl.pallas_call(
        _bias_kernel,
        in_specs=[pl.BlockSpec(memory_space=pltpu.SMEM),
                  pl.BlockSpec(memory_space=pltpu.VMEM), pl.BlockSpec(memory_space=pltpu.VMEM)],
        out_specs=pl.BlockSpec(memory_space=pltpu.VMEM),
        out_shape=jax.ShapeDtypeStruct((2, SWA_Q_HEADS, WINDOW, 2 * WINDOW), F32),
        name="rel_bias",
    )(table.astype(F32), jnp.asarray(bucket), jnp.asarray(valid))


def _swa_kernel(sinks_ref, qs_ref, kv_ref, bias_ref, nw_ref, o_ref):
    W = WINDOW
    T = qs_ref.shape[0]
    group = SWA_Q_HEADS // SWA_KV_HEADS
    lo_half = lax.broadcasted_iota(jnp.int32, (W, LANES), 1) < SWA_HEAD_DIM

    def block(n, carry):
        r0 = pl.multiple_of(n * W, W)
        p0 = pl.multiple_of(jnp.maximum(n - 1, 0) * W, W)
        var = jnp.minimum(n, 1)
        q = qs_ref[pl.ds(r0, W), :]
        kv = jnp.concatenate([kv_ref[pl.ds(p0, W), :], kv_ref[pl.ds(r0, W), :]], axis=0).astype(F32)
        k_pair = kv[:, :LANES]
        v_pair = kv[:, LANES:]
        k_opts = (k_pair.astype(BF16), pltpu.roll(k_pair, SWA_HEAD_DIM, axis=1).astype(BF16))
        v_opts = (v_pair.astype(BF16), pltpu.roll(v_pair, SWA_HEAD_DIM, axis=1).astype(BF16))

        pairs = []
        ss = jnp.zeros((W, 1), F32)
        for pr in range(SWA_Q_HEADS // 2):
            halves = []
            for half in range(2):
                hd = 2 * pr + half
                kvh = hd // group
                sel = 0 if half == kvh else 1
                qp = q[:, pr * LANES:(pr + 1) * LANES]
                qm = jnp.where(lo_half if half == 0 else jnp.logical_not(lo_half), qp, jnp.zeros_like(qp))
                s = _dot_nt(qm, k_opts[sel]) + bias_ref[var, hd]
                sink = sinks_ref[hd] * LOG2_E
                m = jnp.maximum(jnp.max(s, axis=-1, keepdims=True), sink)
                e = jnp.exp2(s - m)
                denom = jnp.sum(e, axis=-1, keepdims=True) + jnp.exp2(sink - m)
                halves.append(_dot(e.astype(BF16), v_opts[sel]) / denom)
            o_pair = jnp.where(lo_half, halves[0], halves[1])
            ss = ss + jnp.sum(o_pair * o_pair, axis=-1, keepdims=True)
            pairs.append(o_pair)
        inv = lax.rsqrt(ss * (1.0 / SWA_Q) + NORM_EPS)
        for pr, o_pair in enumerate(pairs):
            lanes = slice(pr * LANES, (pr + 1) * LANES)
            o_ref[pl.ds(r0, W), lanes] = (o_pair * inv * nw_ref[:, lanes]).astype(BF16)
        return carry

    lax.fori_loop(0, T // W, block, 0, unroll=SWA_UNROLL)


def _swa(sinks, qs, kvs, bias, nw):
    B, T, _ = qs.shape
    seq = lambda w: pl.BlockSpec((None, T, w), lambda b: (b, 0, 0))
    return pl.pallas_call(
        _swa_kernel,
        grid=(B,),
        in_specs=[pl.BlockSpec(memory_space=pltpu.SMEM), seq(SWA_Q), seq(2 * SWA_KV),
                  pl.BlockSpec(bias.shape, lambda b: (0, 0, 0, 0)),
                  pl.BlockSpec(nw.shape, lambda b: (0, 0))],
        out_specs=seq(SWA_Q),
        out_shape=jax.ShapeDtypeStruct((B, T, SWA_Q), BF16),
        compiler_params=pltpu.CompilerParams(
            dimension_semantics=("arbitrary",), vmem_limit_bytes=VMEM_LIMIT_BYTES),
        name="swa",
    )(sinks, qs, kvs, bias, nw)


GELU_K0 = math.sqrt(2.0 / math.pi)
GELU_K1 = 0.044715 * GELU_K0


def _gelu_tanh_x2(x):
    return x * (1.0 + jnp.tanh(x * (GELU_K0 + GELU_K1 * (x * x))))


def _mlp_kernel(x_ref, og_ref, os_ref, wout_ref, pmw_ref, pfw_ref, wgate_ref, wup_ref, convw_ref, convb_ref,
                wdown_ref, postw_ref, o_ref, pad_ref, carry_ref, act_ref):
    tm = x_ref.shape[0]
    halo = SUBLANES
    n_ff = wgate_ref.shape[1] // FF_CHUNK
    n_sub = MLP_SUBTILES
    sub = tm // n_sub

    @pl.when(pl.program_id(1) == 0)
    def _():
        carry_ref[...] = jnp.zeros(carry_ref.shape, F32)

    x1s, hs = [], []
    for t in range(n_sub):
        rows = slice(t * sub, (t + 1) * sub)
        mix = _dot(og_ref[rows, :], wout_ref[0:GDN_QK, :]) + _dot(os_ref[rows, :], wout_ref[GDN_QK:, :])
        x1 = x_ref[rows, :] + _rms(mix, pmw_ref[...])
        x1s.append(x1)
        hs.append(_rms(x1, pfw_ref[...]).astype(BF16))

    for t in range(n_sub):
        rows = slice(t * sub, (t + 1) * sub)
        for c in range(n_ff):
            cols = slice(c * FF_CHUNK, (c + 1) * FF_CHUNK)
            gate = _dot(hs[t], wgate_ref[:, cols])
            up = _dot(hs[t], wup_ref[:, cols])
            pad_ref[t, 0:halo, :] = carry_ref[c]
            pad_ref[t, halo:halo + sub, :] = gate
            carry_ref[c] = gate[sub - halo:, :]
            cw = convw_ref[:, cols]
            y = gate * cw[FFN_CONV - 1:FFN_CONV] + convb_ref[:, cols]
            for j in range(FFN_CONV - 1):
                r0 = halo - (FFN_CONV - 1) + j
                y = y + pad_ref[t, r0:r0 + sub, :] * cw[j:j + 1]
            act_ref[rows, cols] = (_gelu_tanh_x2(y) * up).astype(BF16)
        y = _dot(act_ref[rows, :], wdown_ref[...])
        o_ref[rows, :] = x1s[t] + _rms(y, postw_ref[...])


def _mlp(x, og, osw, wout, pmw, pfw, wgate, wup, convw, convb, wdown, postw, tm):
    B, T, D = x.shape
    n_ff = wgate.shape[1] // FF_CHUNK
    tok = lambda w: pl.BlockSpec((None, tm, w), lambda b, i: (b, i, 0))
    full = lambda a: pl.BlockSpec(a.shape, lambda b, i: (0,) * a.ndim, pipeline_mode=pl.Buffered(1))
    return pl.pallas_call(
        _mlp_kernel,
        grid=(B, T // tm),
        in_specs=[tok(D), tok(GDN_QK), tok(SWA_Q), full(wout), full(pmw), full(pfw), full(wgate), full(wup),
                  full(convw), full(convb), full(wdown), full(postw)],
        out_specs=tok(D),
        out_shape=jax.ShapeDtypeStruct((B, T, D), x.dtype),
        scratch_shapes=[pltpu.VMEM((MLP_SUBTILES, tm // MLP_SUBTILES + SUBLANES, FF_CHUNK), F32),
                        pltpu.VMEM((n_ff, SUBLANES, FF_CHUNK), F32),
                        pltpu.VMEM((tm, n_ff * FF_CHUNK), BF16)],
        compiler_params=pltpu.CompilerParams(
            dimension_semantics=("arbitrary", "arbitrary"), vmem_limit_bytes=VMEM_LIMIT_BYTES),
        name="mlp",
    )(x, og, osw, wout, pmw, pfw, wgate, wup, convw, convb, wdown, postw)


def _token_tile(T, largest):
    for tm in (1024, 512, 256, 128):
        if tm <= largest and T % tm == 0:
            return tm
    raise ValueError(f"sequence length {T} must be a multiple of 128")


def kernel(x, pre_mix_norm_w, w_in, gdn_conv_w, gdn_a_log, gdn_dt_bias, gdn_norm_w, swa_sinks, rel_bias_table,
           swa_norm_w, w_out, post_mix_norm_w, pre_ffn_norm_w, w_gate, w_up, ffn_conv_w, ffn_conv_b, w_down,
           post_ffn_norm_w):
    B, T, D = x.shape
    depth = w_in.shape[0]
    d_ff = w_gate.shape[-1]
    assert T % GDN_STEP == 0 and T % WINDOW == 0 and d_ff % FF_CHUNK == 0
    tm_in = _token_tile(T, INPROJ_TILE)
    tm_mlp = _token_tile(T, MLP_TILE)
    gdn_qkv = 3 * GDN_QK
    n_gate = 2 * GDN_HEADS
    ab0 = gdn_qkv + GDN_QK
    row = lambda a: a.reshape(1, -1).astype(F32)

    bias = _rel_bias(rel_bias_table)

    for l in range(depth):
        wl = w_in[l]
        qs0 = ab0 + n_gate
        wmain = jnp.concatenate([wl[:, :ab0], wl[:, qs0:qs0 + SWA_Q] * (SWA_HEAD_DIM ** -0.5 * LOG2_E),
                                 wl[:, qs0 + SWA_Q:]], axis=1).astype(BF16)
        wab = jnp.pad(wl[:, ab0:ab0 + n_gate], ((0, 0), (0, LANES - n_gate))).astype(BF16)
        alog_row = jnp.pad(row(gdn_a_log[l]), ((0, 0), (0, LANES - GDN_HEADS)))
        dtb_row = jnp.pad(row(gdn_dt_bias[l]), ((0, 0), (0, LANES - GDN_HEADS)))

        q, k, v, z, gb, qs, kvs = _inproj(x, row(pre_mix_norm_w[l]), wmain, wab, gdn_conv_w[l].astype(F32),
                                          alog_row, dtb_row, tm_in)
        o_g = _gdn(q, k, v, z, gb, row(gdn_norm_w[l]))
        o_s = _swa(swa_sinks[l].astype(F32), qs, kvs, bias, row(swa_norm_w[l]))

        x = _mlp(x, o_g, o_s, w_out[l].astype(BF16), row(post_mix_norm_w[l]), row(pre_ffn_norm_w[l]),
                 w_gate[l].astype(BF16), (0.5 * w_up[l]).astype(BF16), ffn_conv_w[l].astype(F32), row(ffn_conv_b[l]),
                 w_down[l].astype(BF16), row(post_ffn_norm_w[l]), tm_mlp)
    return x
```

```python
import functools
import math

import numpy as np
import jax
import jax.numpy as jnp
from jax import lax
from jax.experimental import pallas as pl
from jax.experimental.pallas import tpu as pltpu

F32 = jnp.float32
BF16 = jnp.bfloat16

GDN_HEADS = 4
GDN_DK = 128
GDN_CONV = 4
GDN_CHUNK = 64
SWA_Q_HEADS = 8
SWA_KV_HEADS = 2
SWA_HEAD_DIM = 64
WINDOW = 128
REL_BUCKETS = 32
REL_MAX_DIST = 128
FFN_CONV = 3
NORM_EPS = 1e-6
NEG_INF = -1e30
LOG2_E = math.log2(math.e)

GDN_QK = GDN_HEADS * GDN_DK
SWA_Q = SWA_Q_HEADS * SWA_HEAD_DIM
SWA_KV = SWA_KV_HEADS * SWA_HEAD_DIM

LANES = 128
SUBLANES = 8
VMEM_LIMIT_BYTES = 56 * 1024 * 1024

GDN_SUPER = 4 * GDN_CHUNK
GDN_STEP = 4 * GDN_SUPER
FF_CHUNK = 256
INPROJ_TILE = 1024
MLP_TILE = 1024
MLP_SUBTILES = 2
SWA_UNROLL = 4


def _rms(x, w):
    return x * lax.rsqrt(jnp.mean(x * x, axis=-1, keepdims=True) + NORM_EPS) * w


def _sigmoid(x):
    return 1.0 / (1.0 + jnp.exp2(x * (-LOG2_E)))


def _silu(x):
    return x * _sigmoid(x)


def _dot(a, b):
    return jnp.dot(a, b, preferred_element_type=F32)


def _dot_nt(a, b):
    return lax.dot_general(a, b, (((1,), (1,)), ((), ())), preferred_element_type=F32)


def _inproj_kernel(x_ref, nw_ref, wmain_ref, wab_ref, convw_ref, alog_ref, dtb_ref,
                   qkvz_ref, gb_ref, qs_ref, kvs_ref, pad_ref):
    tm = x_ref.shape[0]
    halo = SUBLANES

    @pl.when(pl.program_id(1) == 0)
    def _():
        pad_ref[:, 0:halo, :] = jnp.zeros((3 * GDN_HEADS, halo, GDN_DK), F32)

    h = _rms(x_ref[...], nw_ref[...]).astype(BF16)

    ab = _dot(h, wab_ref[...])
    lane = lax.broadcasted_iota(jnp.int32, ab.shape, 1)
    pre = ab + dtb_ref[...]
    softplus = jnp.maximum(pre, 0.0) + jnp.log(1.0 + jnp.exp(-jnp.abs(pre)))
    g = -jnp.exp(alog_ref[...]) * softplus
    gb_ref[...] = jnp.where(lane < GDN_HEADS, g, _sigmoid(ab))

    for grp in range(3):
        c0 = grp * GDN_QK
        raw = _dot(h, wmain_ref[:, c0:c0 + GDN_QK])
        for hd in range(GDN_HEADS):
            lanes = slice(hd * GDN_DK, (hd + 1) * GDN_DK)
            slab = grp * GDN_HEADS + hd
            pad_ref[slab, halo:halo + tm, :] = raw[:, lanes]
            cw = convw_ref[:, c0 + hd * GDN_DK:c0 + (hd + 1) * GDN_DK]
            y = raw[:, lanes] * cw[GDN_CONV - 1:GDN_CONV]
            for j in range(GDN_CONV - 1):
                r0 = halo - (GDN_CONV - 1) + j
                y = y + pad_ref[slab, r0:r0 + tm, :] * cw[j:j + 1]
            pad_ref[slab, 0:halo, :] = pad_ref[slab, tm:tm + halo, :]
            y = _silu(y)
            if grp < 2:
                scale = GDN_DK ** -0.5 if grp == 0 else 1.0
                y = y * (lax.rsqrt(jnp.sum(y * y, axis=-1, keepdims=True) + NORM_EPS) * scale)
            qkvz_ref[:, c0 + hd * GDN_DK:c0 + (hd + 1) * GDN_DK] = y.astype(BF16)

    c0 = 3 * GDN_QK
    qkvz_ref[:, c0:c0 + GDN_QK] = _dot(h, wmain_ref[:, c0:c0 + GDN_QK]).astype(BF16)
    c0 += GDN_QK
    qs_ref[...] = _dot(h, wmain_ref[:, c0:c0 + SWA_Q]).astype(BF16)
    c0 += SWA_Q
    kvs_ref[...] = _dot(h, wmain_ref[:, c0:c0 + 2 * SWA_KV]).astype(BF16)


def _inproj(x, nw, wmain, wab, convw, alog_row, dtb_row, tm):
    B, T, D = x.shape
    tok = lambda w: pl.BlockSpec((None, tm, w), lambda b, i: (b, i, 0))
    full = lambda a: pl.BlockSpec(a.shape, lambda b, i: (0,) * a.ndim)
    sds = lambda w, dt: jax.ShapeDtypeStruct((B, T, w), dt)
    return pl.pallas_call(
        _inproj_kernel,
        grid=(B, T // tm),
        in_specs=[tok(D), full(nw), full(wmain), full(wab), full(convw), full(alog_row), full(dtb_row)],
        out_specs=[tok(4 * GDN_QK), tok(LANES), tok(SWA_Q), tok(2 * SWA_KV)],
        out_shape=[sds(4 * GDN_QK, BF16), sds(LANES, F32), sds(SWA_Q, BF16), sds(2 * SWA_KV, BF16)],
        scratch_shapes=[pltpu.VMEM((3 * GDN_HEADS, tm + SUBLANES, GDN_DK), F32)],
        compiler_params=pltpu.CompilerParams(
            dimension_semantics=("arbitrary", "arbitrary"), vmem_limit_bytes=VMEM_LIMIT_BYTES),
        name="inproj",
    )(x, nw, wmain, wab, convw, alog_row, dtb_row)


def _split3(a):
    hi = a.astype(BF16)
    r1 = a - hi.astype(F32)
    mid = r1.astype(BF16)
    lo = (r1 - mid.astype(F32)).astype(BF16)
    return hi, mid, lo


def _lane_col(a, idx):
    lane = lax.broadcasted_iota(jnp.int32, a.shape, 1)
    return jnp.sum(jnp.where(lane == idx, a, 0.0), axis=-1, keepdims=True)


def _gdn_kernel(qkvz_ref, gb_ref, nw_ref, o_ref, s_ref):
    R = GDN_SUPER
    C = GDN_CHUNK
    n_chunks = R // C
    n_super = qkvz_ref.shape[0] // R
    shift = int(math.log2(C))
    heads = range(GDN_HEADS)
    units = [(sc, hd) for sc in range(n_super) for hd in heads]

    @pl.when(pl.program_id(1) == 0)
    def _():
        s_ref[...] = jnp.zeros(s_ref.shape, F32)

    row = lax.broadcasted_iota(jnp.int32, (R, R), 0)
    col = lax.broadcasted_iota(jnp.int32, (R, R), 1)
    same = (row >> shift) == (col >> shift)
    causal = jnp.logical_and(same, row >= col)
    strict = jnp.logical_and(same, row > col)
    eye_packed = jnp.where((lax.broadcasted_iota(jnp.int32, (C, R), 1) & (C - 1))
                           == lax.broadcasted_iota(jnp.int32, (C, R), 0), 1.0, 0.0)
    sums_to = jnp.concatenate([jnp.where(jnp.logical_and(same, row <= col), 1.0, 0.0),
                               jnp.where(same, 1.0, 0.0)], axis=1).astype(BF16)

    def pack(full):
        return sum(full[j * C:(j + 1) * C] for j in range(1, n_chunks)) + full[0:C]

    def block_diag(packed):
        return jnp.where(same, jnp.concatenate([packed] * n_chunks, axis=0), jnp.zeros((R, R), BF16))

    gbs, gcum_rows, cols = [], [], []
    for sc in range(n_super):
        gb = gb_ref[sc * R:(sc + 1) * R, :]
        hi, mid, lo = (p.astype(F32)[0:SUBLANES] for p in _split3(gb.T))
        parts = jnp.concatenate([hi, mid, lo, jnp.zeros_like(hi)], axis=0).astype(BF16)
        sums = _dot(parts, sums_to)
        sums = sums[0:SUBLANES] + sums[SUBLANES:2 * SUBLANES] + sums[2 * SUBLANES:3 * SUBLANES]
        gbs.append(gb)
        gcum_rows.append(sums[:, :R])
        cols.append(jnp.concatenate([sums[:, :R], sums[:, R:], jnp.zeros((LANES - 2 * SUBLANES, R), F32)], axis=0).T)

    rows_of = lambda sc: slice(sc * R, (sc + 1) * R)
    lanes_of = lambda hd, group=0: slice(group * GDN_QK + hd * GDN_DK, group * GDN_QK + (hd + 1) * GDN_DK)
    wide = lambda c: jnp.broadcast_to(c, (R, GDN_DK))
    gc = [_lane_col(cols[sc], hd) for sc, hd in units]
    gl = [_lane_col(cols[sc], SUBLANES + hd) for sc, hd in units]
    beta = [_lane_col(gbs[sc], GDN_HEADS + hd) for sc, hd in units]
    n_units = range(len(units))
    decay = [jnp.exp(gc[u] - gcum_rows[sc][hd:hd + 1, :]) for u, (sc, hd) in enumerate(units)]
    qh = [qkvz_ref[rows_of(sc), lanes_of(hd, 0)] for sc, hd in units]
    kh = [qkvz_ref[rows_of(sc), lanes_of(hd, 1)] for sc, hd in units]
    kf = [kh[u].astype(F32) for u in n_units]
    kb = [kf[u] * beta[u] for u in n_units]
    eg = [jnp.exp(wide(gc[u])) for u in n_units]
    rhs = [jnp.concatenate([qkvz_ref[rows_of(sc), lanes_of(hd, 2)].astype(F32) * beta[u], kb[u] * eg[u]],
                           axis=1).astype(BF16) for u, (sc, hd) in enumerate(units)]

    kq = [_dot_nt(jnp.concatenate([kb[u].astype(BF16), qh[u]], axis=0), kh[u]) for u in n_units]
    a = [jnp.where(strict, kq[u][:R] * decay[u], 0.0) for u in n_units]
    qk = [jnp.where(causal, kq[u][R:] * decay[u], 0.0).astype(BF16) for u in n_units]

    a_packed = [pack(a[u]) for u in n_units]
    p = [eye_packed - a_packed[u] for u in n_units]
    xpow = [_dot(a_packed[u].astype(BF16), a[u].astype(BF16)) for u in n_units]
    for _ in range(shift - 2):
        xb = [xpow[u].astype(BF16) for u in n_units]
        y = [_dot(jnp.concatenate([p[u].astype(BF16), xb[u]], axis=0), block_diag(xb[u])) for u in n_units]
        p = [p[u] + y[u][:C] for u in n_units]
        xpow = [y[u][C:] for u in n_units]
    p = [p[u] + _dot(p[u].astype(BF16), block_diag(xpow[u].astype(BF16))) for u in n_units]

    uw = [_dot(block_diag(p[u].astype(BF16)), rhs[u]).astype(BF16) for u in n_units]
    qkuw = [_dot(qk[u], uw[u]) for u in n_units]
    q_eff = [(qh[u].astype(F32) * eg[u] - qkuw[u][:, GDN_DK:]).astype(BF16) for u in n_units]
    kd_t = [(kf[u] * jnp.exp(wide(gl[u] - gc[u]))).T.astype(BF16) for u in n_units]
    g_last = [jnp.exp(wide(gl[u])) for u in n_units]
    row_chunk = lax.broadcasted_iota(jnp.int32, (R, 2 * GDN_DK), 0) >> shift
    zero_uw = jnp.zeros((R, 2 * GDN_DK), BF16)
    nm = [[_dot(kd_t[u], jnp.where(row_chunk == c, uw[u], zero_uw)) for c in range(n_chunks)] for u in n_units]

    state = [s_ref[hd] for hd in heads]
    zero_s = jnp.zeros((GDN_DK, GDN_DK), BF16)
    for sc in range(n_super):
        o_chunks = [[] for _ in heads]
        for c in range(n_chunks):
            rows = slice(c * C, (c + 1) * C)
            for h0 in range(0, GDN_HEADS, 2):
                h1 = h0 + 1
                u0, u1 = sc * GDN_HEADS + h0, sc * GDN_HEADS + h1
                lhs = jnp.concatenate(
                    [jnp.concatenate([q_eff[u0][rows], q_eff[u1][rows]], axis=1),
                     jnp.concatenate([nm[u0][c][:, GDN_DK:], nm[u1][c][:, GDN_DK:]], axis=1).astype(BF16)], axis=0)
                s_pair = jnp.concatenate(
                    [jnp.concatenate([state[h0].astype(BF16), zero_s], axis=1),
                     jnp.concatenate([zero_s, state[h1].astype(BF16)], axis=1)], axis=0)
                r = _dot(lhs, s_pair)
                for i, (hd, u) in enumerate(((h0, u0), (h1, u1))):
                    half = slice(i * GDN_DK, (i + 1) * GDN_DK)
                    o_chunks[hd].append(r[:C, half] + qkuw[u][rows, :GDN_DK])
                    state[hd] = state[hd] * g_last[u][c * C:c * C + 1, :] - r[C:, half] + nm[u][c][:, :GDN_DK]
        for hd in heads:
            o = jnp.concatenate(o_chunks[hd], axis=0)
            o = _rms(o, nw_ref[...]) * _silu(qkvz_ref[rows_of(sc), lanes_of(hd, 3)].astype(F32))
            o_ref[rows_of(sc), lanes_of(hd)] = o.astype(BF16)

    for hd in heads:
        s_ref[hd] = state[hd]


def _gdn(qkvz, gb, nw):
    B, T, _ = qkvz.shape
    rows = GDN_STEP
    tok = lambda w: pl.BlockSpec((None, rows, w), lambda b, i: (b, i, 0))
    return pl.pallas_call(
        _gdn_kernel,
        grid=(B, T // rows),
        in_specs=[tok(4 * GDN_QK), tok(LANES), pl.BlockSpec(nw.shape, lambda b, i: (0, 0))],
        out_specs=tok(GDN_QK),
        out_shape=jax.ShapeDtypeStruct((B, T, GDN_QK), BF16),
        scratch_shapes=[pltpu.VMEM((GDN_HEADS, GDN_DK, GDN_DK), F32)],
        compiler_params=pltpu.CompilerParams(
            dimension_semantics=("arbitrary", "arbitrary"), vmem_limit_bytes=VMEM_LIMIT_BYTES),
        name="gdn",
    )(qkvz, gb, nw)


def _band_tables():
    qi = np.arange(WINDOW, dtype=np.int64)[:, None]
    sj = np.arange(2 * WINDOW, dtype=np.int64)[None, :]
    dist = qi + WINDOW - sj
    in_band = (dist >= 0) & (dist < WINDOW)
    d = np.maximum(dist, 0)
    max_exact = REL_BUCKETS // 2
    ratio = np.log(np.maximum(d, 1).astype(np.float32) / np.float32(max_exact)) / np.float32(
        math.log(REL_MAX_DIST / max_exact))
    large = max_exact + (ratio.astype(np.float32) * np.float32(REL_BUCKETS - max_exact)).astype(np.int32)
    large = np.minimum(large, REL_BUCKETS - 1)
    bucket = np.where(d < max_exact, d, large).astype(np.int32)
    valid = np.stack([in_band & (sj >= WINDOW), in_band]).astype(np.int32)
    return bucket, valid


def _bias_kernel(table_ref, bucket_ref, valid_ref, o_ref):
    bucket = bucket_ref[...]
    for hd in range(SWA_Q_HEADS):
        acc = jnp.zeros(bucket.shape, F32)
        for b in range(REL_BUCKETS):
            acc = jnp.where(bucket == b, table_ref[b, hd] * LOG2_E, acc)
        for var in range(2):
            o_ref[var, hd] = jnp.where(valid_ref[var] != 0, acc, NEG_INF)


def _rel_bias(table):
    bucket, valid = _band_tables()
    return pl.pallas_call(
        _bias_kernel,
        in_specs=[pl.BlockSpec(memory_space=pltpu.SMEM),
                  pl.BlockSpec(memory_space=pltpu.VMEM), pl.BlockSpec(memory_space=pltpu.VMEM)],
        out_specs=pl.BlockSpec(memory_space=pltpu.VMEM),
        out_shape=jax.ShapeDtypeStruct((2, SWA_Q_HEADS, WINDOW, 2 * WINDOW), F32),
        name="rel_bias",
    )(table.astype(F32), jnp.asarray(bucket), jnp.asarray(valid))


def _swa_kernel(sinks_ref, qs_ref, kv_ref, bias_ref, nw_ref, o_ref):
    W = WINDOW
    T = qs_ref.shape[0]
    group = SWA_Q_HEADS // SWA_KV_HEADS
    lo_half = lax.broadcasted_iota(jnp.int32, (W, LANES), 1) < SWA_HEAD_DIM

    def block(n, carry):
        r0 = pl.multiple_of(n * W, W)
        p0 = pl.multiple_of(jnp.maximum(n - 1, 0) * W, W)
        var = jnp.minimum(n, 1)
        q = qs_ref[pl.ds(r0, W), :]
        kv = jnp.concatenate([kv_ref[pl.ds(p0, W), :], kv_ref[pl.ds(r0, W), :]], axis=0).astype(F32)
        k_pair = kv[:, :LANES]
        v_pair = kv[:, LANES:]
        k_opts = (k_pair.astype(BF16), pltpu.roll(k_pair, SWA_HEAD_DIM, axis=1).astype(BF16))
        v_opts = (v_pair.astype(BF16), pltpu.roll(v_pair, SWA_HEAD_DIM, axis=1).astype(BF16))

        pairs = []
        ss = jnp.zeros((W, 1), F32)
        for pr in range(SWA_Q_HEADS // 2):
            halves = []
            for half in range(2):
                hd = 2 * pr + half
                kvh = hd // group
                sel = 0 if half == kvh else 1
                qp = q[:, pr * LANES:(pr + 1) * LANES]
                qm = jnp.where(lo_half if half == 0 else jnp.logical_not(lo_half), qp, jnp.zeros_like(qp))
                s = _dot_nt(qm, k_opts[sel]) + bias_ref[var, hd]
                sink = sinks_ref[hd] * LOG2_E
                m = jnp.maximum(jnp.max(s, axis=-1, keepdims=True), sink)
                e = jnp.exp2(s - m)
                denom = jnp.sum(e, axis=-1, keepdims=True) + jnp.exp2(sink - m)
                halves.append(_dot(e.astype(BF16), v_opts[sel]) / denom)
            o_pair = jnp.where(lo_half, halves[0], halves[1])
            ss = ss + jnp.sum(o_pair * o_pair, axis=-1, keepdims=True)
            pairs.append(o_pair)
        inv = lax.rsqrt(ss * (1.0 / SWA_Q) + NORM_EPS)
        for pr, o_pair in enumerate(pairs):
            lanes = slice(pr * LANES, (pr + 1) * LANES)
            o_ref[pl.ds(r0, W), lanes] = (o_pair * inv * nw_ref[:, lanes]).astype(BF16)
        return carry

    lax.fori_loop(0, T // W, block, 0, unroll=SWA_UNROLL)


def _swa(sinks, qs, kvs, bias, nw):
    B, T, _ = qs.shape
    seq = lambda w: pl.BlockSpec((None, T, w), lambda b: (b, 0, 0))
    return pl.pallas_call(
        _swa_kernel,
        grid=(B,),
        in_specs=[pl.BlockSpec(memory_space=pltpu.SMEM), seq(SWA_Q), seq(2 * SWA_KV),
                  pl.BlockSpec(bias.shape, lambda b: (0, 0, 0, 0)),
                  pl.BlockSpec(nw.shape, lambda b: (0, 0))],
        out_specs=seq(SWA_Q),
        out_shape=jax.ShapeDtypeStruct((B, T, SWA_Q), BF16),
        compiler_params=pltpu.CompilerParams(
            dimension_semantics=("arbitrary",), vmem_limit_bytes=VMEM_LIMIT_BYTES),
        name="swa",
    )(sinks, qs, kvs, bias, nw)


GELU_K0 = math.sqrt(2.0 / math.pi)
GELU_K1 = 0.044715 * GELU_K0


def _gelu_tanh_x2(x):
    return x * (1.0 + jnp.tanh(x * (GELU_K0 + GELU_K1 * (x * x))))


def _mlp_kernel(x_ref, og_ref, os_ref, wout_ref, pmw_ref, pfw_ref, wgate_ref, wup_ref, convw_ref, convb_ref,
                wdown_ref, postw_ref, o_ref, pad_ref, carry_ref, act_ref):
    tm = x_ref.shape[0]
    halo = SUBLANES
    n_ff = wgate_ref.shape[1] // FF_CHUNK
    n_sub = MLP_SUBTILES
    sub = tm // n_sub

    @pl.when(pl.program_id(1) == 0)
    def _():
        carry_ref[...] = jnp.zeros(carry_ref.shape, F32)

    x1s, hs = [], []
    for t in range(n_sub):
        rows = slice(t * sub, (t + 1) * sub)
        mix = _dot(og_ref[rows, :], wout_ref[0:GDN_QK, :]) + _dot(os_ref[rows, :], wout_ref[GDN_QK:, :])
        x1 = x_ref[rows, :] + _rms(mix, pmw_ref[...])
        x1s.append(x1)
        hs.append(_rms(x1, pfw_ref[...]).astype(BF16))

    for t in range(n_sub):
        rows = slice(t * sub, (t + 1) * sub)
        for c in range(n_ff):
            cols = slice(c * FF_CHUNK, (c + 1) * FF_CHUNK)
            gate = _dot(hs[t], wgate_ref[:, cols])
            up = _dot(hs[t], wup_ref[:, cols])
            pad_ref[t, 0:halo, :] = carry_ref[c]
            pad_ref[t, halo:halo + sub, :] = gate
            carry_ref[c] = gate[sub - halo:, :]
            cw = convw_ref[:, cols]
            y = gate * cw[FFN_CONV - 1:FFN_CONV] + convb_ref[:, cols]
            for j in range(FFN_CONV - 1):
                r0 = halo - (FFN_CONV - 1) + j
                y = y + pad_ref[t, r0:r0 + sub, :] * cw[j:j + 1]
            act_ref[rows, cols] = (_gelu_tanh_x2(y) * up).astype(BF16)
        y = _dot(act_ref[rows, :], wdown_ref[...])
        o_ref[rows, :] = x1s[t] + _rms(y, postw_ref[...])


def _mlp(x, og, osw, wout, pmw, pfw, wgate, wup, convw, convb, wdown, postw, tm):
    B, T, D = x.shape
    n_ff = wgate.shape[1] // FF_CHUNK
    tok = lambda w: pl.BlockSpec((None, tm, w), lambda b, i: (b, i, 0))
    full = lambda a: pl.BlockSpec(a.shape, lambda b, i: (0,) * a.ndim, pipeline_mode=pl.Buffered(1))
    return pl.pallas_call(
        _mlp_kernel,
        grid=(B, T // tm),
        in_specs=[tok(D), tok(GDN_QK), tok(SWA_Q), full(wout), full(pmw), full(pfw), full(wgate), full(wup),
                  full(convw), full(convb), full(wdown), full(postw)],
        out_specs=tok(D),
        out_shape=jax.ShapeDtypeStruct((B, T, D), x.dtype),
        scratch_shapes=[pltpu.VMEM((MLP_SUBTILES, tm // MLP_SUBTILES + SUBLANES, FF_CHUNK), F32),
                        pltpu.VMEM((n_ff, SUBLANES, FF_CHUNK), F32),
                        pltpu.VMEM((tm, n_ff * FF_CHUNK), BF16)],
        compiler_params=pltpu.CompilerParams(
            dimension_semantics=("arbitrary", "arbitrary"), vmem_limit_bytes=VMEM_LIMIT_BYTES),
        name="mlp",
    )(x, og, osw, wout, pmw, pfw, wgate, wup, convw, convb, wdown, postw)


def _token_tile(T, largest):
    for tm in (1024, 512, 256, 128):
        if tm <= largest and T % tm == 0:
            return tm
    raise ValueError(f"sequence length {T} must be a multiple of 128")


def kernel(x, pre_mix_norm_w, w_in, gdn_conv_w, gdn_a_log, gdn_dt_bias, gdn_norm_w, swa_sinks, rel_bias_table,
           swa_norm_w, w_out, post_mix_norm_w, pre_ffn_norm_w, w_gate, w_up, ffn_conv_w, ffn_conv_b, w_down,
           post_ffn_norm_w):
    B, T, D = x.shape
    depth = w_in.shape[0]
    d_ff = w_gate.shape[-1]
    assert T % GDN_STEP == 0 and T % WINDOW == 0 and d_ff % FF_CHUNK == 0
    tm_in = _token_tile(T, INPROJ_TILE)
    tm_mlp = _token_tile(T, MLP_TILE)
    gdn_qkv = 3 * GDN_QK
    n_gate = 2 * GDN_HEADS
    ab0 = gdn_qkv + GDN_QK
    row = lambda a: a.reshape(1, -1).astype(F32)

    bias = _rel_bias(rel_bias_table)

    for l in range(depth):
        wl = w_in[l]
        qs0 = ab0 + n_gate
        wmain = jnp.concatenate([wl[:, :ab0], wl[:, qs0:qs0 + SWA_Q] * (SWA_HEAD_DIM ** -0.5 * LOG2_E),
                                 wl[:, qs0 + SWA_Q:]], axis=1).astype(BF16)
        wab = jnp.pad(wl[:, ab0:ab0 + n_gate], ((0, 0), (0, LANES - n_gate))).astype(BF16)
        alog_row = jnp.pad(row(gdn_a_log[l]), ((0, 0), (0, LANES - GDN_HEADS)))
        dtb_row = jnp.pad(row(gdn_dt_bias[l]), ((0, 0), (0, LANES - GDN_HEADS)))

        qkvz, gb, qs, kvs = _inproj(x, row(pre_mix_norm_w[l]), wmain, wab, gdn_conv_w[l].astype(F32),
                                          alog_row, dtb_row, tm_in)
        o_g = _gdn(qkvz, gb, row(gdn_norm_w[l]))
        o_s = _swa(swa_sinks[l].astype(F32), qs, kvs, bias, row(swa_norm_w[l]))

        x = _mlp(x, o_g, o_s, w_out[l].astype(BF16), row(post_mix_norm_w[l]), row(pre_ffn_norm_w[l]),
                 w_gate[l].astype(BF16), (0.5 * w_up[l]).astype(BF16), ffn_conv_w[l].astype(F32), row(ffn_conv_b[l]),
                 w_down[l].astype(BF16), row(post_ffn_norm_w[l]), tm_mlp)
    return x
```

```python
import functools
import math

import numpy as np
import jax
import jax.numpy as jnp
from jax import lax
from jax.experimental import pallas as pl
from jax.experimental.pallas import tpu as pltpu

F32 = jnp.float32
BF16 = jnp.bfloat16

GDN_HEADS = 4
GDN_DK = 128
GDN_CONV = 4
GDN_CHUNK = 64
SWA_Q_HEADS = 8
SWA_KV_HEADS = 2
SWA_HEAD_DIM = 64
WINDOW = 128
REL_BUCKETS = 32
REL_MAX_DIST = 128
FFN_CONV = 3
NORM_EPS = 1e-6
NEG_INF = -1e30
LOG2_E = math.log2(math.e)

GDN_QK = GDN_HEADS * GDN_DK
SWA_Q = SWA_Q_HEADS * SWA_HEAD_DIM
SWA_KV = SWA_KV_HEADS * SWA_HEAD_DIM

LANES = 128
SUBLANES = 8
VMEM_LIMIT_BYTES = 56 * 1024 * 1024

GDN_SUPER = 4 * GDN_CHUNK
GDN_STEP = 4 * GDN_SUPER
FF_CHUNK = 256
INPROJ_TILE = 1024
MLP_TILE = 1024
MLP_SUBTILES = 2


def _rms(x, w):
    return x * lax.rsqrt(jnp.mean(x * x, axis=-1, keepdims=True) + NORM_EPS) * w


def _sigmoid(x):
    return 1.0 / (1.0 + jnp.exp2(x * (-LOG2_E)))


def _silu(x):
    return x * _sigmoid(x)


def _dot(a, b):
    return jnp.dot(a, b, preferred_element_type=F32)


def _dot_nt(a, b):
    return lax.dot_general(a, b, (((1,), (1,)), ((), ())), preferred_element_type=F32)


def _inproj_kernel(x_ref, nw_ref, wmain_ref, wab_ref, convw_ref, alog_ref, dtb_ref,
                   q_ref, k_ref, v_ref, z_ref, gb_ref, qs_ref, kvs_ref, pad_ref):
    tm = x_ref.shape[0]
    halo = SUBLANES

    @pl.when(pl.program_id(1) == 0)
    def _():
        pad_ref[:, 0:halo, :] = jnp.zeros((3 * GDN_HEADS, halo, GDN_DK), F32)

    h = _rms(x_ref[...], nw_ref[...]).astype(BF16)

    ab = _dot(h, wab_ref[...])
    lane = lax.broadcasted_iota(jnp.int32, ab.shape, 1)
    pre = ab + dtb_ref[...]
    softplus = jnp.maximum(pre, 0.0) + jnp.log(1.0 + jnp.exp(-jnp.abs(pre)))
    g = -jnp.exp(alog_ref[...]) * softplus
    gb_ref[...] = jnp.where(lane < GDN_HEADS, g, _sigmoid(ab))

    outs = (q_ref, k_ref, v_ref)
    for grp in range(3):
        c0 = grp * GDN_QK
        raw = _dot(h, wmain_ref[:, c0:c0 + GDN_QK])
        for hd in range(GDN_HEADS):
            lanes = slice(hd * GDN_DK, (hd + 1) * GDN_DK)
            slab = grp * GDN_HEADS + hd
            pad_ref[slab, halo:halo + tm, :] = raw[:, lanes]
            cw = convw_ref[:, c0 + hd * GDN_DK:c0 + (hd + 1) * GDN_DK]
            y = raw[:, lanes] * cw[GDN_CONV - 1:GDN_CONV]
            for j in range(GDN_CONV - 1):
                r0 = halo - (GDN_CONV - 1) + j
                y = y + pad_ref[slab, r0:r0 + tm, :] * cw[j:j + 1]
            pad_ref[slab, 0:halo, :] = pad_ref[slab, tm:tm + halo, :]
            y = _silu(y)
            if grp < 2:
                scale = GDN_DK ** -0.5 if grp == 0 else 1.0
                y = y * (lax.rsqrt(jnp.sum(y * y, axis=-1, keepdims=True) + NORM_EPS) * scale)
            outs[grp][:, lanes] = y.astype(BF16)

    c0 = 3 * GDN_QK
    z_ref[...] = _dot(h, wmain_ref[:, c0:c0 + GDN_QK]).astype(BF16)
    c0 += GDN_QK
    qs_ref[...] = _dot(h, wmain_ref[:, c0:c0 + SWA_Q]).astype(BF16)
    c0 += SWA_Q
    kvs_ref[...] = _dot(h, wmain_ref[:, c0:c0 + 2 * SWA_KV]).astype(BF16)


def _inproj(x, nw, wmain, wab, convw, alog_row, dtb_row, tm):
    B, T, D = x.shape
    tok = lambda w: pl.BlockSpec((None, tm, w), lambda b, i: (b, i, 0))
    full = lambda a: pl.BlockSpec(a.shape, lambda b, i: (0,) * a.ndim)
    sds = lambda w, dt: jax.ShapeDtypeStruct((B, T, w), dt)
    return pl.pallas_call(
        _inproj_kernel,
        grid=(B, T // tm),
        in_specs=[tok(D), full(nw), full(wmain), full(wab), full(convw), full(alog_row), full(dtb_row)],
        out_specs=[tok(GDN_QK), tok(GDN_QK), tok(GDN_QK), tok(GDN_QK), tok(LANES), tok(SWA_Q), tok(2 * SWA_KV)],
        out_shape=[sds(GDN_QK, BF16), sds(GDN_QK, BF16), sds(GDN_QK, BF16), sds(GDN_QK, BF16),
                   sds(LANES, F32), sds(SWA_Q, BF16), sds(2 * SWA_KV, BF16)],
        scratch_shapes=[pltpu.VMEM((3 * GDN_HEADS, tm + SUBLANES, GDN_DK), F32)],
        compiler_params=pltpu.CompilerParams(
            dimension_semantics=("arbitrary", "arbitrary"), vmem_limit_bytes=VMEM_LIMIT_BYTES),
        name="inproj",
    )(x, nw, wmain, wab, convw, alog_row, dtb_row)


def _split3(a):
    hi = a.astype(BF16)
    r1 = a - hi.astype(F32)
    mid = r1.astype(BF16)
    lo = (r1 - mid.astype(F32)).astype(BF16)
    return hi, mid, lo


def _lane_col(a, idx):
    lane = lax.broadcasted_iota(jnp.int32, a.shape, 1)
    return jnp.sum(jnp.where(lane == idx, a, 0.0), axis=-1, keepdims=True)


def _gdn_kernel(q_ref, k_ref, v_ref, z_ref, gb_ref, nw_ref, o_ref, s_ref):
    R = GDN_SUPER
    C = GDN_CHUNK
    n_chunks = R // C
    n_super = q_ref.shape[0] // R
    shift = int(math.log2(C))
    heads = range(GDN_HEADS)
    units = [(sc, hd) for sc in range(n_super) for hd in heads]

    @pl.when(pl.program_id(1) == 0)
    def _():
        s_ref[...] = jnp.zeros(s_ref.shape, F32)

    row = lax.broadcasted_iota(jnp.int32, (R, R), 0)
    col = lax.broadcasted_iota(jnp.int32, (R, R), 1)
    same = (row >> shift) == (col >> shift)
    causal = jnp.logical_and(same, row >= col)
    strict = jnp.logical_and(same, row > col)
    eye_packed = jnp.where((lax.broadcasted_iota(jnp.int32, (C, R), 1) & (C - 1))
                           == lax.broadcasted_iota(jnp.int32, (C, R), 0), 1.0, 0.0)
    sums_to = jnp.concatenate([jnp.where(jnp.logical_and(same, row <= col), 1.0, 0.0),
                               jnp.where(same, 1.0, 0.0)], axis=1).astype(BF16)

    def pack(full):
        return sum(full[j * C:(j + 1) * C] for j in range(1, n_chunks)) + full[0:C]

    def block_diag(packed):
        return jnp.where(same, jnp.concatenate([packed] * n_chunks, axis=0), jnp.zeros((R, R), BF16))

    gbs, gcum_rows, cols = [], [], []
    for sc in range(n_super):
        gb = gb_ref[sc * R:(sc + 1) * R, :]
        hi, mid, lo = (p.astype(F32)[0:SUBLANES] for p in _split3(gb.T))
        parts = jnp.concatenate([hi, mid, lo, jnp.zeros_like(hi)], axis=0).astype(BF16)
        sums = _dot(parts, sums_to)
        sums = sums[0:SUBLANES] + sums[SUBLANES:2 * SUBLANES] + sums[2 * SUBLANES:3 * SUBLANES]
        gbs.append(gb)
        gcum_rows.append(sums[:, :R])
        cols.append(jnp.concatenate([sums[:, :R], sums[:, R:], jnp.zeros((LANES - 2 * SUBLANES, R), F32)], axis=0).T)

    rows_of = lambda sc: slice(sc * R, (sc + 1) * R)
    lanes_of = lambda hd: slice(hd * GDN_DK, (hd + 1) * GDN_DK)
    wide = lambda c: jnp.broadcast_to(c, (R, GDN_DK))
    gc = [_lane_col(cols[sc], hd) for sc, hd in units]
    gl = [_lane_col(cols[sc], SUBLANES + hd) for sc, hd in units]
    beta = [_lane_col(gbs[sc], GDN_HEADS + hd) for sc, hd in units]
    n_units = range(len(units))
    decay = [jnp.exp(gc[u] - gcum_rows[sc][hd:hd + 1, :]) for u, (sc, hd) in enumerate(units)]
    qh = [q_ref[rows_of(sc), lanes_of(hd)] for sc, hd in units]
    kh = [k_ref[rows_of(sc), lanes_of(hd)] for sc, hd in units]
    kf = [kh[u].astype(F32) for u in n_units]
    kb = [kf[u] * beta[u] for u in n_units]
    eg = [jnp.exp(wide(gc[u])) for u in n_units]
    rhs = [jnp.concatenate([v_ref[rows_of(sc), lanes_of(hd)].astype(F32) * beta[u], kb[u] * eg[u]],
                           axis=1).astype(BF16) for u, (sc, hd) in enumerate(units)]

    kq = [_dot_nt(jnp.concatenate([kb[u].astype(BF16), qh[u]], axis=0), kh[u]) for u in n_units]
    a = [jnp.where(strict, kq[u][:R] * decay[u], 0.0) for u in n_units]
    qk = [jnp.where(causal, kq[u][R:] * decay[u], 0.0).astype(BF16) for u in n_units]

    a_packed = [pack(a[u]) for u in n_units]
    p = [eye_packed - a_packed[u] for u in n_units]
    xpow = [_dot(a_packed[u].astype(BF16), a[u].astype(BF16)) for u in n_units]
    for _ in range(shift - 2):
        xb = [xpow[u].astype(BF16) for u in n_units]
        y = [_dot(jnp.concatenate([p[u].astype(BF16), xb[u]], axis=0), block_diag(xb[u])) for u in n_units]
        p = [p[u] + y[u][:C] for u in n_units]
        xpow = [y[u][C:] for u in n_units]
    p = [p[u] + _dot(p[u].astype(BF16), block_diag(xpow[u].astype(BF16))) for u in n_units]

    uw = [_dot(block_diag(p[u].astype(BF16)), rhs[u]).astype(BF16) for u in n_units]
    qkuw = [_dot(qk[u], uw[u]) for u in n_units]
    q_eff = [(qh[u].astype(F32) * eg[u] - qkuw[u][:, GDN_DK:]).astype(BF16) for u in n_units]
    kd_t = [(kf[u] * jnp.exp(wide(gl[u] - gc[u]))).T.astype(BF16) for u in n_units]
    g_last = [jnp.exp(wide(gl[u])) for u in n_units]
    row_chunk = lax.broadcasted_iota(jnp.int32, (R, 2 * GDN_DK), 0) >> shift
    zero_uw = jnp.zeros((R, 2 * GDN_DK), BF16)
    nm = [[_dot(kd_t[u], jnp.where(row_chunk == c, uw[u], zero_uw)) for c in range(n_chunks)] for u in n_units]

    state = [s_ref[hd] for hd in heads]
    zero_s = jnp.zeros((GDN_DK, GDN_DK), BF16)
    for sc in range(n_super):
        o_chunks = [[] for _ in heads]
        for c in range(n_chunks):
            rows = slice(c * C, (c + 1) * C)
            for h0 in range(0, GDN_HEADS, 2):
                h1 = h0 + 1
                u0, u1 = sc * GDN_HEADS + h0, sc * GDN_HEADS + h1
                lhs = jnp.concatenate(
                    [jnp.concatenate([q_eff[u0][rows], q_eff[u1][rows]], axis=1),
                     jnp.concatenate([nm[u0][c][:, GDN_DK:], nm[u1][c][:, GDN_DK:]], axis=1).astype(BF16)], axis=0)
                s_pair = jnp.concatenate(
                    [jnp.concatenate([state[h0].astype(BF16), zero_s], axis=1),
                     jnp.concatenate([zero_s, state[h1].astype(BF16)], axis=1)], axis=0)
                r = _dot(lhs, s_pair)
                for i, (hd, u) in enumerate(((h0, u0), (h1, u1))):
                    half = slice(i * GDN_DK, (i + 1) * GDN_DK)
                    o_chunks[hd].append(r[:C, half] + qkuw[u][rows, :GDN_DK])
                    state[hd] = state[hd] * g_last[u][c * C:c * C + 1, :] - r[C:, half] + nm[u][c][:, :GDN_DK]
        for hd in heads:
            o = jnp.concatenate(o_chunks[hd], axis=0)
            o = _rms(o, nw_ref[...]) * _silu(z_ref[rows_of(sc), lanes_of(hd)].astype(F32))
            o_ref[rows_of(sc), lanes_of(hd)] = o.astype(BF16)

    for hd in heads:
        s_ref[hd] = state[hd]


def _band_tables():
    qi = np.arange(WINDOW, dtype=np.int64)[:, None]
    sj = np.arange(2 * WINDOW, dtype=np.int64)[None, :]
    dist = qi + WINDOW - sj
    in_band = (dist >= 0) & (dist < WINDOW)
    d = np.maximum(dist, 0)
    max_exact = REL_BUCKETS // 2
    ratio = np.log(np.maximum(d, 1).astype(np.float32) / np.float32(max_exact)) / np.float32(
        math.log(REL_MAX_DIST / max_exact))
    large = max_exact + (ratio.astype(np.float32) * np.float32(REL_BUCKETS - max_exact)).astype(np.int32)
    large = np.minimum(large, REL_BUCKETS - 1)
    bucket = np.where(d < max_exact, d, large).astype(np.int32)
    valid = np.stack([in_band & (sj >= WINDOW), in_band]).astype(np.int32)
    return bucket, valid


def _bias_kernel(table_ref, bucket_ref, valid_ref, o_ref):
    bucket = bucket_ref[...]
    for hd in range(SWA_Q_HEADS):
        acc = jnp.zeros(bucket.shape, F32)
        for b in range(REL_BUCKETS):
            acc = jnp.where(bucket == b, table_ref[b, hd] * LOG2_E, acc)
        for var in range(2):
            o_ref[var, hd] = jnp.where(valid_ref[var] != 0, acc, NEG_INF)


def _rel_bias(table):
    bucket, valid = _band_tables()
    return pl.pallas_call(
        _bias_kernel,
        in_specs=[pl.BlockSpec(memory_space=pltpu.SMEM),
                  pl.BlockSpec(memory_space=pltpu.VMEM), pl.BlockSpec(memory_space=pltpu.VMEM)],
        out_specs=pl.BlockSpec(memory_space=pltpu.VMEM),
        out_shape=jax.ShapeDtypeStruct((2, SWA_Q_HEADS, WINDOW, 2 * WINDOW), F32),
        name="rel_bias",
    )(table.astype(F32), jnp.asarray(bucket), jnp.asarray(valid))


def _swa_block(n, q, kv_ref, bias_ref, sinks_ref, nw_ref):
    W = WINDOW
    group = SWA_Q_HEADS // SWA_KV_HEADS
    lo_half = lax.broadcasted_iota(jnp.int32, (W, LANES), 1) < SWA_HEAD_DIM
    r0 = pl.multiple_of(n * W, W)
    p0 = pl.multiple_of(jnp.maximum(n - 1, 0) * W, W)
    var = jnp.minimum(n, 1)
    kv = jnp.concatenate([kv_ref[pl.ds(p0, W), :], kv_ref[pl.ds(r0, W), :]], axis=0).astype(F32)
    k_pair = kv[:, :LANES]
    v_pair = kv[:, LANES:]
    k_opts = (k_pair.astype(BF16), pltpu.roll(k_pair, SWA_HEAD_DIM, axis=1).astype(BF16))
    v_opts = (v_pair.astype(BF16), pltpu.roll(v_pair, SWA_HEAD_DIM, axis=1).astype(BF16))

    pairs = []
    ss = jnp.zeros((W, 1), F32)
    for pr in range(SWA_Q_HEADS // 2):
        halves = []
        for half in range(2):
            hd = 2 * pr + half
            kvh = hd // group
            sel = 0 if half == kvh else 1
            qp = q[:, pr * LANES:(pr + 1) * LANES]
            qm = jnp.where(lo_half if half == 0 else jnp.logical_not(lo_half), qp, jnp.zeros_like(qp))
            s = _dot_nt(qm, k_opts[sel]) + bias_ref[var, hd]
            sink = sinks_ref[hd] * LOG2_E
            m = jnp.maximum(jnp.max(s, axis=-1, keepdims=True), sink)
            e = jnp.exp2(s - m)
            denom = jnp.sum(e, axis=-1, keepdims=True) + jnp.exp2(sink - m)
            halves.append(_dot(e.astype(BF16), v_opts[sel]) / denom)
        o_pair = jnp.where(lo_half, halves[0], halves[1])
        ss = ss + jnp.sum(o_pair * o_pair, axis=-1, keepdims=True)
        pairs.append(o_pair)
    inv = lax.rsqrt(ss * (1.0 / SWA_Q) + NORM_EPS)
    return jnp.concatenate([(o_pair * inv * nw_ref[:, pr * LANES:(pr + 1) * LANES]).astype(BF16)
                            for pr, o_pair in enumerate(pairs)], axis=1)


def _mix_kernel(q_ref, k_ref, v_ref, z_ref, gb_ref, gnw_ref, sinks_ref, qs_ref, kv_ref, bias_ref, snw_ref,
                og_ref, os_ref, s_ref):
    _gdn_kernel(q_ref, k_ref, v_ref, z_ref, gb_ref, gnw_ref, og_ref, s_ref)
    W = WINDOW
    blocks = qs_ref.shape[0] // W
    first = pl.program_id(1) * blocks
    for j in range(blocks):
        rows = slice(j * W, (j + 1) * W)
        os_ref[rows, :] = _swa_block(first + j, qs_ref[rows, :], kv_ref, bias_ref, sinks_ref, snw_ref)


def _mix(q, k, v, z, gb, gnw, sinks, qs, kvs, bias, snw):
    B, T, _ = q.shape
    rows = GDN_STEP
    tok = lambda w: pl.BlockSpec((None, rows, w), lambda b, i: (b, i, 0))
    const = lambda a: pl.BlockSpec(a.shape, lambda b, i: (0,) * a.ndim)
    return pl.pallas_call(
        _mix_kernel,
        grid=(B, T // rows),
        in_specs=[tok(GDN_QK), tok(GDN_QK), tok(GDN_QK), tok(GDN_QK), tok(LANES), const(gnw),
                  pl.BlockSpec(memory_space=pltpu.SMEM), tok(SWA_Q),
                  pl.BlockSpec((None, T, 2 * SWA_KV), lambda b, i: (b, 0, 0)), const(bias), const(snw)],
        out_specs=[tok(GDN_QK), tok(SWA_Q)],
        out_shape=[jax.ShapeDtypeStruct((B, T, GDN_QK), BF16), jax.ShapeDtypeStruct((B, T, SWA_Q), BF16)],
        scratch_shapes=[pltpu.VMEM((GDN_HEADS, GDN_DK, GDN_DK), F32)],
        compiler_params=pltpu.CompilerParams(
            dimension_semantics=("arbitrary", "arbitrary"), vmem_limit_bytes=VMEM_LIMIT_BYTES),
        name="mix",
    )(q, k, v, z, gb, gnw, sinks, qs, kvs, bias, snw)


GELU_K0 = math.sqrt(2.0 / math.pi)
GELU_K1 = 0.044715 * GELU_K0


def _gelu_tanh_x2(x):
    return x * (1.0 + jnp.tanh(x * (GELU_K0 + GELU_K1 * (x * x))))


def _mlp_kernel(x_ref, og_ref, os_ref, wout_ref, pmw_ref, pfw_ref, wgate_ref, wup_ref, convw_ref, convb_ref,
                wdown_ref, postw_ref, o_ref, pad_ref, carry_ref, act_ref):
    tm = x_ref.shape[0]
    halo = SUBLANES
    n_ff = wgate_ref.shape[1] // FF_CHUNK
    n_sub = MLP_SUBTILES
    sub = tm // n_sub

    @pl.when(pl.program_id(1) == 0)
    def _():
        carry_ref[...] = jnp.zeros(carry_ref.shape, F32)

    x1s, hs = [], []
    for t in range(n_sub):
        rows = slice(t * sub, (t + 1) * sub)
        mix = _dot(og_ref[rows, :], wout_ref[0:GDN_QK, :]) + _dot(os_ref[rows, :], wout_ref[GDN_QK:, :])
        x1 = x_ref[rows, :] + _rms(mix, pmw_ref[...])
        x1s.append(x1)
        hs.append(_rms(x1, pfw_ref[...]).astype(BF16))

    for t in range(n_sub):
        rows = slice(t * sub, (t + 1) * sub)
        for c in range(n_ff):
            cols = slice(c * FF_CHUNK, (c + 1) * FF_CHUNK)
            gate = _dot(hs[t], wgate_ref[:, cols])
            up = _dot(hs[t], wup_ref[:, cols])
            pad_ref[t, 0:halo, :] = carry_ref[c]
            pad_ref[t, halo:halo + sub, :] = gate
            carry_ref[c] = gate[sub - halo:, :]
            cw = convw_ref[:, cols]
            y = gate * cw[FFN_CONV - 1:FFN_CONV] + convb_ref[:, cols]
            for j in range(FFN_CONV - 1):
                r0 = halo - (FFN_CONV - 1) + j
                y = y + pad_ref[t, r0:r0 + sub, :] * cw[j:j + 1]
            act_ref[rows, cols] = (_gelu_tanh_x2(y) * up).astype(BF16)
        y = _dot(act_ref[rows, :], wdown_ref[...])
        o_ref[rows, :] = x1s[t] + _rms(y, postw_ref[...])


def _mlp(x, og, osw, wout, pmw, pfw, wgate, wup, convw, convb, wdown, postw, tm):
    B, T, D = x.shape
    n_ff = wgate.shape[1] // FF_CHUNK
    tok = lambda w: pl.BlockSpec((None, tm, w), lambda b, i: (b, i, 0))
    full = lambda a: pl.BlockSpec(a.shape, lambda b, i: (0,) * a.ndim, pipeline_mode=pl.Buffered(1))
    return pl.pallas_call(
        _mlp_kernel,
        grid=(B, T // tm),
        in_specs=[tok(D), tok(GDN_QK), tok(SWA_Q), full(wout), full(pmw), full(pfw), full(wgate), full(wup),
                  full(convw), full(convb), full(wdown), full(postw)],
        out_specs=tok(D),
        out_shape=jax.ShapeDtypeStruct((B, T, D), x.dtype),
        scratch_shapes=[pltpu.VMEM((MLP_SUBTILES, tm // MLP_SUBTILES + SUBLANES, FF_CHUNK), F32),
                        pltpu.VMEM((n_ff, SUBLANES, FF_CHUNK), F32),
                        pltpu.VMEM((tm, n_ff * FF_CHUNK), BF16)],
        compiler_params=pltpu.CompilerParams(
            dimension_semantics=("arbitrary", "arbitrary"), vmem_limit_bytes=VMEM_LIMIT_BYTES),
        name="mlp",
    )(x, og, osw, wout, pmw, pfw, wgate, wup, convw, convb, wdown, postw)


def _token_tile(T, largest):
    for tm in (1024, 512, 256, 128):
        if tm <= largest and T % tm == 0:
            return tm
    raise ValueError(f"sequence length {T} must be a multiple of 128")


def kernel(x, pre_mix_norm_w, w_in, gdn_conv_w, gdn_a_log, gdn_dt_bias, gdn_norm_w, swa_sinks, rel_bias_table,
           swa_norm_w, w_out, post_mix_norm_w, pre_ffn_norm_w, w_gate, w_up, ffn_conv_w, ffn_conv_b, w_down,
           post_ffn_norm_w):
    B, T, D = x.shape
    depth = w_in.shape[0]
    d_ff = w_gate.shape[-1]
    assert T % GDN_STEP == 0 and T % WINDOW == 0 and d_ff % FF_CHUNK == 0
    tm_in = _token_tile(T, INPROJ_TILE)
    tm_mlp = _token_tile(T, MLP_TILE)
    gdn_qkv = 3 * GDN_QK
    n_gate = 2 * GDN_HEADS
    ab0 = gdn_qkv + GDN_QK
    row = lambda a: a.reshape(1, -1).astype(F32)

    bias = _rel_bias(rel_bias_table)

    for l in range(depth):
        wl = w_in[l]
        qs0 = ab0 + n_gate
        wmain = jnp.concatenate([wl[:, :ab0], wl[:, qs0:qs0 + SWA_Q] * (SWA_HEAD_DIM ** -0.5 * LOG2_E),
                                 wl[:, qs0 + SWA_Q:]], axis=1).astype(BF16)
        wab = jnp.pad(wl[:, ab0:ab0 + n_gate], ((0, 0), (0, LANES - n_gate))).astype(BF16)
        alog_row = jnp.pad(row(gdn_a_log[l]), ((0, 0), (0, LANES - GDN_HEADS)))
        dtb_row = jnp.pad(row(gdn_dt_bias[l]), ((0, 0), (0, LANES - GDN_HEADS)))

        q, k, v, z, gb, qs, kvs = _inproj(x, row(pre_mix_norm_w[l]), wmain, wab, gdn_conv_w[l].astype(F32),
                                          alog_row, dtb_row, tm_in)
        o_g, o_s = _mix(q, k, v, z, gb, row(gdn_norm_w[l]), swa_sinks[l].astype(F32), qs, kvs, bias,
                        row(swa_norm_w[l]))

        x = _mlp(x, o_g, o_s, w_out[l].astype(BF16), row(post_mix_norm_w[l]), row(pre_ffn_norm_w[l]),
                 w_gate[l].astype(BF16), (0.5 * w_up[l]).astype(BF16), ffn_conv_w[l].astype(F32), row(ffn_conv_b[l]),
                 w_down[l].astype(BF16), row(post_ffn_norm_w[l]), tm_mlp)
    return x
```

```python
import functools
import math

import numpy as np
import jax
import jax.numpy as jnp
from jax import lax
from jax.experimental import pallas as pl
from jax.experimental.pallas import tpu as pltpu

F32 = jnp.float32
BF16 = jnp.bfloat16

GDN_HEADS = 4
GDN_DK = 128
GDN_CONV = 4
GDN_CHUNK = 64
SWA_Q_HEADS = 8
SWA_KV_HEADS = 2
SWA_HEAD_DIM = 64
WINDOW = 128
REL_BUCKETS = 32
REL_MAX_DIST = 128
FFN_CONV = 3
NORM_EPS = 1e-6
NEG_INF = -1e30
LOG2_E = math.log2(math.e)

GDN_QK = GDN_HEADS * GDN_DK
SWA_Q = SWA_Q_HEADS * SWA_HEAD_DIM
SWA_KV = SWA_KV_HEADS * SWA_HEAD_DIM

LANES = 128
SUBLANES = 8
VMEM_LIMIT_BYTES = 56 * 1024 * 1024

GDN_SUPER = 4 * GDN_CHUNK
GDN_STEP = 4 * GDN_SUPER
FF_CHUNK = 256
MLP_TILE = 1024
MLP_SUBTILES = 2


def _rms(x, w):
    return x * lax.rsqrt(jnp.mean(x * x, axis=-1, keepdims=True) + NORM_EPS) * w


def _sigmoid(x):
    return 1.0 / (1.0 + jnp.exp2(x * (-LOG2_E)))


def _silu(x):
    return x * _sigmoid(x)


def _dot(a, b):
    return jnp.dot(a, b, preferred_element_type=F32)


def _dot_nt(a, b):
    return lax.dot_general(a, b, (((1,), (1,)), ((), ())), preferred_element_type=F32)


def _inproj_kernel(x_ref, nw_ref, wmain_ref, wab_ref, convw_ref, alog_ref, dtb_ref,
                   q_ref, k_ref, v_ref, z_ref, gb_ref, qs_ref, kvs_ref, pad_ref):
    tm = x_ref.shape[0]
    halo = SUBLANES

    @pl.when(pl.program_id(1) == 0)
    def _():
        pad_ref[:, 0:halo, :] = jnp.zeros((3 * GDN_HEADS, halo, GDN_DK), F32)

    h = _rms(x_ref[...], nw_ref[...]).astype(BF16)

    ab = _dot(h, wab_ref[...])
    lane = lax.broadcasted_iota(jnp.int32, ab.shape, 1)
    pre = ab + dtb_ref[...]
    softplus = jnp.maximum(pre, 0.0) + jnp.log(1.0 + jnp.exp(-jnp.abs(pre)))
    g = -jnp.exp(alog_ref[...]) * softplus
    gb_ref[...] = jnp.where(lane < GDN_HEADS, g, _sigmoid(ab))

    outs = (q_ref, k_ref, v_ref)
    for grp in range(3):
        c0 = grp * GDN_QK
        raw = _dot(h, wmain_ref[:, c0:c0 + GDN_QK])
        for hd in range(GDN_HEADS):
            lanes = slice(hd * GDN_DK, (hd + 1) * GDN_DK)
            slab = grp * GDN_HEADS + hd
            pad_ref[slab, halo:halo + tm, :] = raw[:, lanes]
            cw = convw_ref[:, c0 + hd * GDN_DK:c0 + (hd + 1) * GDN_DK]
            y = raw[:, lanes] * cw[GDN_CONV - 1:GDN_CONV]
            for j in range(GDN_CONV - 1):
                r0 = halo - (GDN_CONV - 1) + j
                y = y + pad_ref[slab, r0:r0 + tm, :] * cw[j:j + 1]
            pad_ref[slab, 0:halo, :] = pad_ref[slab, tm:tm + halo, :]
            y = _silu(y)
            if grp < 2:
                scale = GDN_DK ** -0.5 if grp == 0 else 1.0
                y = y * (lax.rsqrt(jnp.sum(y * y, axis=-1, keepdims=True) + NORM_EPS) * scale)
            outs[grp][:, lanes] = y.astype(BF16)

    c0 = 3 * GDN_QK
    z_ref[...] = _dot(h, wmain_ref[:, c0:c0 + GDN_QK]).astype(BF16)
    c0 += GDN_QK
    qs_ref[...] = _dot(h, wmain_ref[:, c0:c0 + SWA_Q]).astype(BF16)
    c0 += SWA_Q
    kvs_ref[...] = _dot(h, wmain_ref[:, c0:c0 + 2 * SWA_KV]).astype(BF16)


def _split3(a):
    hi = a.astype(BF16)
    r1 = a - hi.astype(F32)
    mid = r1.astype(BF16)
    lo = (r1 - mid.astype(F32)).astype(BF16)
    return hi, mid, lo


def _lane_col(a, idx):
    lane = lax.broadcasted_iota(jnp.int32, a.shape, 1)
    return jnp.sum(jnp.where(lane == idx, a, 0.0), axis=-1, keepdims=True)


def _gdn_kernel(q_ref, k_ref, v_ref, z_ref, gb_ref, nw_ref, o_ref, s_ref):
    R = GDN_SUPER
    C = GDN_CHUNK
    n_chunks = R // C
    n_super = q_ref.shape[0] // R
    shift = int(math.log2(C))
    heads = range(GDN_HEADS)
    units = [(sc, hd) for sc in range(n_super) for hd in heads]

    @pl.when(pl.program_id(1) == 0)
    def _():
        s_ref[...] = jnp.zeros(s_ref.shape, F32)

    row = lax.broadcasted_iota(jnp.int32, (R, R), 0)
    col = lax.broadcasted_iota(jnp.int32, (R, R), 1)
    same = (row >> shift) == (col >> shift)
    causal = jnp.logical_and(same, row >= col)
    strict = jnp.logical_and(same, row > col)
    eye_packed = jnp.where((lax.broadcasted_iota(jnp.int32, (C, R), 1) & (C - 1))
                           == lax.broadcasted_iota(jnp.int32, (C, R), 0), 1.0, 0.0)
    sums_to = jnp.concatenate([jnp.where(jnp.logical_and(same, row <= col), 1.0, 0.0),
                               jnp.where(same, 1.0, 0.0)], axis=1).astype(BF16)

    def pack(full):
        return sum(full[j * C:(j + 1) * C] for j in range(1, n_chunks)) + full[0:C]

    def block_diag(packed):
        return jnp.where(same, jnp.concatenate([packed] * n_chunks, axis=0), jnp.zeros((R, R), BF16))

    gbs, gcum_rows, cols = [], [], []
    for sc in range(n_super):
        gb = gb_ref[sc * R:(sc + 1) * R, :]
        hi, mid, lo = (p.astype(F32)[0:SUBLANES] for p in _split3(gb.T))
        parts = jnp.concatenate([hi, mid, lo, jnp.zeros_like(hi)], axis=0).astype(BF16)
        sums = _dot(parts, sums_to)
        sums = sums[0:SUBLANES] + sums[SUBLANES:2 * SUBLANES] + sums[2 * SUBLANES:3 * SUBLANES]
        gbs.append(gb)
        gcum_rows.append(sums[:, :R])
        cols.append(jnp.concatenate([sums[:, :R], sums[:, R:], jnp.zeros((LANES - 2 * SUBLANES, R), F32)], axis=0).T)

    rows_of = lambda sc: slice(sc * R, (sc + 1) * R)
    lanes_of = lambda hd: slice(hd * GDN_DK, (hd + 1) * GDN_DK)
    wide = lambda c: jnp.broadcast_to(c, (R, GDN_DK))
    gc = [_lane_col(cols[sc], hd) for sc, hd in units]
    gl = [_lane_col(cols[sc], SUBLANES + hd) for sc, hd in units]
    beta = [_lane_col(gbs[sc], GDN_HEADS + hd) for sc, hd in units]
    n_units = range(len(units))
    decay = [jnp.exp(gc[u] - gcum_rows[sc][hd:hd + 1, :]) for u, (sc, hd) in enumerate(units)]
    qh = [q_ref[rows_of(sc), lanes_of(hd)] for sc, hd in units]
    kh = [k_ref[rows_of(sc), lanes_of(hd)] for sc, hd in units]
    kf = [kh[u].astype(F32) for u in n_units]
    kb = [kf[u] * beta[u] for u in n_units]
    eg = [jnp.exp(wide(gc[u])) for u in n_units]
    rhs = [jnp.concatenate([v_ref[rows_of(sc), lanes_of(hd)].astype(F32) * beta[u], kb[u] * eg[u]],
                           axis=1).astype(BF16) for u, (sc, hd) in enumerate(units)]

    kq = [_dot_nt(jnp.concatenate([kb[u].astype(BF16), qh[u]], axis=0), kh[u]) for u in n_units]
    a = [jnp.where(strict, kq[u][:R] * decay[u], 0.0) for u in n_units]
    qk = [jnp.where(causal, kq[u][R:] * decay[u], 0.0).astype(BF16) for u in n_units]

    a_packed = [pack(a[u]) for u in n_units]
    p = [eye_packed - a_packed[u] for u in n_units]
    xpow = [_dot(a_packed[u].astype(BF16), a[u].astype(BF16)) for u in n_units]
    for _ in range(shift - 2):
        xb = [xpow[u].astype(BF16) for u in n_units]
        y = [_dot(jnp.concatenate([p[u].astype(BF16), xb[u]], axis=0), block_diag(xb[u])) for u in n_units]
        p = [p[u] + y[u][:C] for u in n_units]
        xpow = [y[u][C:] for u in n_units]
    p = [p[u] + _dot(p[u].astype(BF16), block_diag(xpow[u].astype(BF16))) for u in n_units]

    uw = [_dot(block_diag(p[u].astype(BF16)), rhs[u]).astype(BF16) for u in n_units]
    qkuw = [_dot(qk[u], uw[u]) for u in n_units]
    q_eff = [(qh[u].astype(F32) * eg[u] - qkuw[u][:, GDN_DK:]).astype(BF16) for u in n_units]
    kd_t = [(kf[u] * jnp.exp(wide(gl[u] - gc[u]))).T.astype(BF16) for u in n_units]
    g_last = [jnp.exp(wide(gl[u])) for u in n_units]
    row_chunk = lax.broadcasted_iota(jnp.int32, (R, 2 * GDN_DK), 0) >> shift
    zero_uw = jnp.zeros((R, 2 * GDN_DK), BF16)
    nm = [[_dot(kd_t[u], jnp.where(row_chunk == c, uw[u], zero_uw)) for c in range(n_chunks)] for u in n_units]

    state = [s_ref[hd] for hd in heads]
    zero_s = jnp.zeros((GDN_DK, GDN_DK), BF16)
    for sc in range(n_super):
        o_chunks = [[] for _ in heads]
        for c in range(n_chunks):
            rows = slice(c * C, (c + 1) * C)
            for h0 in range(0, GDN_HEADS, 2):
                h1 = h0 + 1
                u0, u1 = sc * GDN_HEADS + h0, sc * GDN_HEADS + h1
                lhs = jnp.concatenate(
                    [jnp.concatenate([q_eff[u0][rows], q_eff[u1][rows]], axis=1),
                     jnp.concatenate([nm[u0][c][:, GDN_DK:], nm[u1][c][:, GDN_DK:]], axis=1).astype(BF16)], axis=0)
                s_pair = jnp.concatenate(
                    [jnp.concatenate([state[h0].astype(BF16), zero_s], axis=1),
                     jnp.concatenate([zero_s, state[h1].astype(BF16)], axis=1)], axis=0)
                r = _dot(lhs, s_pair)
                for i, (hd, u) in enumerate(((h0, u0), (h1, u1))):
                    half = slice(i * GDN_DK, (i + 1) * GDN_DK)
                    o_chunks[hd].append(r[:C, half] + qkuw[u][rows, :GDN_DK])
                    state[hd] = state[hd] * g_last[u][c * C:c * C + 1, :] - r[C:, half] + nm[u][c][:, :GDN_DK]
        for hd in heads:
            o = jnp.concatenate(o_chunks[hd], axis=0)
            o = _rms(o, nw_ref[...]) * _silu(z_ref[rows_of(sc), lanes_of(hd)].astype(F32))
            o_ref[rows_of(sc), lanes_of(hd)] = o.astype(BF16)

    for hd in heads:
        s_ref[hd] = state[hd]


def _band_tables():
    qi = np.arange(WINDOW, dtype=np.int64)[:, None]
    sj = np.arange(2 * WINDOW, dtype=np.int64)[None, :]
    dist = qi + WINDOW - sj
    in_band = (dist >= 0) & (dist < WINDOW)
    d = np.maximum(dist, 0)
    max_exact = REL_BUCKETS // 2
    ratio = np.log(np.maximum(d, 1).astype(np.float32) / np.float32(max_exact)) / np.float32(
        math.log(REL_MAX_DIST / max_exact))
    large = max_exact + (ratio.astype(np.float32) * np.float32(REL_BUCKETS - max_exact)).astype(np.int32)
    large = np.minimum(large, REL_BUCKETS - 1)
    bucket = np.where(d < max_exact, d, large).astype(np.int32)
    valid = np.stack([in_band & (sj >= WINDOW), in_band]).astype(np.int32)
    return bucket, valid


def _bias_kernel(table_ref, bucket_ref, valid_ref, o_ref):
    bucket = bucket_ref[...]
    for hd in range(SWA_Q_HEADS):
        acc = jnp.zeros(bucket.shape, F32)
        for b in range(REL_BUCKETS):
            acc = jnp.where(bucket == b, table_ref[b, hd] * LOG2_E, acc)
        for var in range(2):
            o_ref[var, hd] = jnp.where(valid_ref[var] != 0, acc, NEG_INF)


def _rel_bias(table):
    bucket, valid = _band_tables()
    return pl.pallas_call(
        _bias_kernel,
        in_specs=[pl.BlockSpec(memory_space=pltpu.SMEM),
                  pl.BlockSpec(memory_space=pltpu.VMEM), pl.BlockSpec(memory_space=pltpu.VMEM)],
        out_specs=pl.BlockSpec(memory_space=pltpu.VMEM),
        out_shape=jax.ShapeDtypeStruct((2, SWA_Q_HEADS, WINDOW, 2 * WINDOW), F32),
        name="rel_bias",
    )(table.astype(F32), jnp.asarray(bucket), jnp.asarray(valid))


def _swa_block(var, q, kv, bias_ref, sinks_ref, nw_ref):
    W = WINDOW
    group = SWA_Q_HEADS // SWA_KV_HEADS
    lo_half = lax.broadcasted_iota(jnp.int32, (W, LANES), 1) < SWA_HEAD_DIM
    kv = kv.astype(F32)
    k_pair = kv[:, :LANES]
    v_pair = kv[:, LANES:]
    k_opts = (k_pair.astype(BF16), pltpu.roll(k_pair, SWA_HEAD_DIM, axis=1).astype(BF16))
    v_opts = (v_pair.astype(BF16), pltpu.roll(v_pair, SWA_HEAD_DIM, axis=1).astype(BF16))

    pairs = []
    ss = jnp.zeros((W, 1), F32)
    for pr in range(SWA_Q_HEADS // 2):
        halves = []
        for half in range(2):
            hd = 2 * pr + half
            kvh = hd // group
            sel = 0 if half == kvh else 1
            qp = q[:, pr * LANES:(pr + 1) * LANES]
            qm = jnp.where(lo_half if half == 0 else jnp.logical_not(lo_half), qp, jnp.zeros_like(qp))
            s = _dot_nt(qm, k_opts[sel]) + bias_ref[var, hd]
            sink = sinks_ref[hd] * LOG2_E
            m = jnp.maximum(jnp.max(s, axis=-1, keepdims=True), sink)
            e = jnp.exp2(s - m)
            denom = jnp.sum(e, axis=-1, keepdims=True) + jnp.exp2(sink - m)
            halves.append(_dot(e.astype(BF16), v_opts[sel]) / denom)
        o_pair = jnp.where(lo_half, halves[0], halves[1])
        ss = ss + jnp.sum(o_pair * o_pair, axis=-1, keepdims=True)
        pairs.append(o_pair)
    inv = lax.rsqrt(ss * (1.0 / SWA_Q) + NORM_EPS)
    return jnp.concatenate([(o_pair * inv * nw_ref[:, pr * LANES:(pr + 1) * LANES]).astype(BF16)
                            for pr, o_pair in enumerate(pairs)], axis=1)


def _front_kernel(x_ref, nw_ref, wmain_ref, wab_ref, convw_ref, alog_ref, dtb_ref, gnw_ref, sinks_ref, bias_ref,
                  snw_ref, og_ref, os_ref, pad_ref, s_ref, q_s, k_s, v_s, z_s, gb_s, qs_s, kv_s):
    W = WINDOW
    rows = x_ref.shape[0]
    blocks = rows // W

    @pl.when(pl.program_id(1) == 0)
    def _():
        kv_s[0:W, :] = jnp.zeros((W, 2 * SWA_KV), BF16)

    _inproj_kernel(x_ref, nw_ref, wmain_ref, wab_ref, convw_ref, alog_ref, dtb_ref,
                   q_s, k_s, v_s, z_s, gb_s, qs_s, kv_s.at[pl.ds(W, rows)], pad_ref)
    _gdn_kernel(q_s, k_s, v_s, z_s, gb_s, gnw_ref, og_ref, s_ref)
    first = pl.program_id(1) * blocks
    for j in range(blocks):
        os_ref[j * W:(j + 1) * W, :] = _swa_block(jnp.minimum(first + j, 1), qs_s[j * W:(j + 1) * W, :],
                                                  kv_s[j * W:(j + 2) * W, :], bias_ref, sinks_ref, snw_ref)
    kv_s[0:W, :] = kv_s[rows:rows + W, :]


def _front(x, nw, wmain, wab, convw, alog_row, dtb_row, gnw, sinks, bias, snw):
    B, T, D = x.shape
    rows = GDN_STEP
    tok = lambda w: pl.BlockSpec((None, rows, w), lambda b, i: (b, i, 0))
    const = lambda a: pl.BlockSpec(a.shape, lambda b, i: (0,) * a.ndim)
    act = lambda w, dt: pltpu.VMEM((rows, w), dt)
    return pl.pallas_call(
        _front_kernel,
        grid=(B, T // rows),
        in_specs=[tok(D), const(nw), const(wmain), const(wab), const(convw), const(alog_row), const(dtb_row),
                  const(gnw), pl.BlockSpec(memory_space=pltpu.SMEM), const(bias), const(snw)],
        out_specs=[tok(GDN_QK), tok(SWA_Q)],
        out_shape=[jax.ShapeDtypeStruct((B, T, GDN_QK), BF16), jax.ShapeDtypeStruct((B, T, SWA_Q), BF16)],
        scratch_shapes=[pltpu.VMEM((3 * GDN_HEADS, rows + SUBLANES, GDN_DK), F32),
                        pltpu.VMEM((GDN_HEADS, GDN_DK, GDN_DK), F32),
                        act(GDN_QK, BF16), act(GDN_QK, BF16), act(GDN_QK, BF16), act(GDN_QK, BF16),
                        act(LANES, F32), act(SWA_Q, BF16), pltpu.VMEM((rows + WINDOW, 2 * SWA_KV), BF16)],
        compiler_params=pltpu.CompilerParams(
            dimension_semantics=("arbitrary", "arbitrary"), vmem_limit_bytes=VMEM_LIMIT_BYTES),
        name="front",
    )(x, nw, wmain, wab, convw, alog_row, dtb_row, gnw, sinks, bias, snw)


GELU_K0 = math.sqrt(2.0 / math.pi)
GELU_K1 = 0.044715 * GELU_K0


def _gelu_tanh_x2(x):
    return x * (1.0 + jnp.tanh(x * (GELU_K0 + GELU_K1 * (x * x))))


def _mlp_kernel(x_ref, og_ref, os_ref, wout_ref, pmw_ref, pfw_ref, wgate_ref, wup_ref, convw_ref, convb_ref,
                wdown_ref, postw_ref, o_ref, pad_ref, carry_ref, act_ref):
    tm = x_ref.shape[0]
    halo = SUBLANES
    n_ff = wgate_ref.shape[1] // FF_CHUNK
    n_sub = MLP_SUBTILES
    sub = tm // n_sub

    @pl.when(pl.program_id(1) == 0)
    def _():
        carry_ref[...] = jnp.zeros(carry_ref.shape, F32)

    x1s, hs = [], []
    for t in range(n_sub):
        rows = slice(t * sub, (t + 1) * sub)
        mix = _dot(og_ref[rows, :], wout_ref[0:GDN_QK, :]) + _dot(os_ref[rows, :], wout_ref[GDN_QK:, :])
        x1 = x_ref[rows, :] + _rms(mix, pmw_ref[...])
        x1s.append(x1)
        hs.append(_rms(x1, pfw_ref[...]).astype(BF16))

    for t in range(n_sub):
        rows = slice(t * sub, (t + 1) * sub)
        for c in range(n_ff):
            cols = slice(c * FF_CHUNK, (c + 1) * FF_CHUNK)
            gate = _dot(hs[t], wgate_ref[:, cols])
            up = _dot(hs[t], wup_ref[:, cols])
            pad_ref[t, 0:halo, :] = carry_ref[c]
            pad_ref[t, halo:halo + sub, :] = gate
            carry_ref[c] = gate[sub - halo:, :]
            cw = convw_ref[:, cols]
            y = gate * cw[FFN_CONV - 1:FFN_CONV] + convb_ref[:, cols]
            for j in range(FFN_CONV - 1):
                r0 = halo - (FFN_CONV - 1) + j
                y = y + pad_ref[t, r0:r0 + sub, :] * cw[j:j + 1]
            act_ref[rows, cols] = (_gelu_tanh_x2(y) * up).astype(BF16)
        y = _dot(act_ref[rows, :], wdown_ref[...])
        o_ref[rows, :] = x1s[t] + _rms(y, postw_ref[...])


def _mlp(x, og, osw, wout, pmw, pfw, wgate, wup, convw, convb, wdown, postw, tm):
    B, T, D = x.shape
    n_ff = wgate.shape[1] // FF_CHUNK
    tok = lambda w: pl.BlockSpec((None, tm, w), lambda b, i: (b, i, 0))
    full = lambda a: pl.BlockSpec(a.shape, lambda b, i: (0,) * a.ndim, pipeline_mode=pl.Buffered(1))
    return pl.pallas_call(
        _mlp_kernel,
        grid=(B, T // tm),
        in_specs=[tok(D), tok(GDN_QK), tok(SWA_Q), full(wout), full(pmw), full(pfw), full(wgate), full(wup),
                  full(convw), full(convb), full(wdown), full(postw)],
        out_specs=tok(D),
        out_shape=jax.ShapeDtypeStruct((B, T, D), x.dtype),
        scratch_shapes=[pltpu.VMEM((MLP_SUBTILES, tm // MLP_SUBTILES + SUBLANES, FF_CHUNK), F32),
                        pltpu.VMEM((n_ff, SUBLANES, FF_CHUNK), F32),
                        pltpu.VMEM((tm, n_ff * FF_CHUNK), BF16)],
        compiler_params=pltpu.CompilerParams(
            dimension_semantics=("arbitrary", "arbitrary"), vmem_limit_bytes=VMEM_LIMIT_BYTES),
        name="mlp",
    )(x, og, osw, wout, pmw, pfw, wgate, wup, convw, convb, wdown, postw)


def _token_tile(T, largest):
    for tm in (1024, 512, 256, 128):
        if tm <= largest and T % tm == 0:
            return tm
    raise ValueError(f"sequence length {T} must be a multiple of 128")


def kernel(x, pre_mix_norm_w, w_in, gdn_conv_w, gdn_a_log, gdn_dt_bias, gdn_norm_w, swa_sinks, rel_bias_table,
           swa_norm_w, w_out, post_mix_norm_w, pre_ffn_norm_w, w_gate, w_up, ffn_conv_w, ffn_conv_b, w_down,
           post_ffn_norm_w):
    B, T, D = x.shape
    depth = w_in.shape[0]
    d_ff = w_gate.shape[-1]
    assert T % GDN_STEP == 0 and T % WINDOW == 0 and d_ff % FF_CHUNK == 0
    tm_mlp = _token_tile(T, MLP_TILE)
    gdn_qkv = 3 * GDN_QK
    n_gate = 2 * GDN_HEADS
    ab0 = gdn_qkv + GDN_QK
    row = lambda a: a.reshape(1, -1).astype(F32)

    bias = _rel_bias(rel_bias_table)

    for l in range(depth):
        wl = w_in[l]
        qs0 = ab0 + n_gate
        wmain = jnp.concatenate([wl[:, :ab0], wl[:, qs0:qs0 + SWA_Q] * (SWA_HEAD_DIM ** -0.5 * LOG2_E),
                                 wl[:, qs0 + SWA_Q:]], axis=1).astype(BF16)
        wab = jnp.pad(wl[:, ab0:ab0 + n_gate], ((0, 0), (0, LANES - n_gate))).astype(BF16)
        alog_row = jnp.pad(row(gdn_a_log[l]), ((0, 0), (0, LANES - GDN_HEADS)))
        dtb_row = jnp.pad(row(gdn_dt_bias[l]), ((0, 0), (0, LANES - GDN_HEADS)))

        o_g, o_s = _front(x, row(pre_mix_norm_w[l]), wmain, wab, gdn_conv_w[l].astype(F32), alog_row, dtb_row,
                          row(gdn_norm_w[l]), swa_sinks[l].astype(F32), bias, row(swa_norm_w[l]))


        x = _mlp(x, o_g, o_s, w_out[l].astype(BF16), row(post_mix_norm_w[l]), row(pre_ffn_norm_w[l]),
                 w_gate[l].astype(BF16), (0.5 * w_up[l]).astype(BF16), ffn_conv_w[l].astype(F32), row(ffn_conv_b[l]),
                 w_down[l].astype(BF16), row(post_ffn_norm_w[l]), tm_mlp)
    return x
```
